```python
import math
import jax, jax.numpy as jnp
from jax import lax
import numpy as np

D_MODEL = 2048
BATCH = 4
SEQ = 2048
DEPTH = 2

CONV_CH = D_MODEL // 2
CONV_W = 31
ATT_HD = 128
ATT_HEADS = D_MODEL // 256
ATT_W = ATT_HEADS * ATT_HD
ROPE_DIMS = ATT_HD // 4
ROPE_THETA = 500000.0
MOBA_BLOCK = 256
MOBA_TOPK = 3
MOBA_Q_CHUNK = 16
RET_HEADS = 8
RET_V_W = D_MODEL // 2
RET_DV = RET_V_W // RET_HEADS
RET_DK = RET_DV // 2
RET_QK_W = RET_HEADS * RET_DK
RET_CHUNK = 128
RET_ROT_THETA = 10000.0
D_FF = int(math.ceil(8 * D_MODEL / 3 / 256)) * 256
N_BRANCH = 3
N_IN = 2 * CONV_CH + 3 * ATT_W + 2 * RET_QK_W + 2 * RET_V_W + N_BRANCH * D_MODEL
EPS = 1e-6

kernel_name = "hybrid_conv_moba_retention_macaron"


def rms_norm(x, g):
    x32 = x.astype(jnp.float32)
    y = x32 * lax.rsqrt(jnp.mean(x32 * x32, axis=-1, keepdims=True) + EPS)
    return (y * g.astype(jnp.float32)).astype(x.dtype)


def layer_norm(x, g, b):
    x32 = x.astype(jnp.float32)
    mu = jnp.mean(x32, axis=-1, keepdims=True)
    xc = x32 - mu
    y = xc * lax.rsqrt(jnp.mean(xc * xc, axis=-1, keepdims=True) + EPS)
    return (y * g.astype(jnp.float32) + b.astype(jnp.float32)).astype(x.dtype)


def swiglu(h, w_gate, w_up, w_down):
    return (jax.nn.silu(h @ w_gate) * (h @ w_up)) @ w_down


def rotary(x, pos, n_rot, theta):
    half = n_rot // 2
    inv = 1.0 / (theta ** (jnp.arange(half, dtype=jnp.float32) / half))
    ang = pos.astype(jnp.float32)[:, None] * inv[None, :]
    cos = jnp.cos(ang).astype(x.dtype)
    sin = jnp.sin(ang).astype(x.dtype)
    x1, x2, xp = x[..., :half], x[..., half:n_rot], x[..., n_rot:]
    return jnp.concatenate([x1 * cos - x2 * sin, x2 * cos + x1 * sin, xp], axis=-1)


def to_heads(t, n, d):
    b, s, _ = t.shape
    return t.reshape(b, s, n, d).transpose(0, 2, 1, 3)


def from_heads(t):
    b, n, s, d = t.shape
    return t.transpose(0, 2, 1, 3).reshape(b, s, n * d)


def conv_module(a, gate, dw_w, dw_b, ln_g, ln_b):
    u = a * jax.nn.sigmoid(gate)
    y = lax.conv_general_dilated(
        u, dw_w[:, None, :], window_strides=(1,), padding=[(CONV_W - 1, 0)],
        dimension_numbers=("NWC", "WIO", "NWC"), feature_group_count=CONV_CH)
    y = layer_norm(y + dw_b, ln_g, ln_b)
    return jax.nn.silu(y)


def moba_attention(q, k, v):
    b_, h_, s_, hd = q.shape
    nb = -(-s_ // MOBA_BLOCK)
    sp = nb * MOBA_BLOCK
    pad = sp - s_
    if pad:
        widths = ((0, 0), (0, 0), (0, pad), (0, 0))
        q, k, v = jnp.pad(q, widths), jnp.pad(k, widths), jnp.pad(v, widths)
    kb = k.reshape(b_, h_, nb, MOBA_BLOCK, hd)
    vb = v.reshape(b_, h_, nb, MOBA_BLOCK, hd)
    t = jnp.arange(sp, dtype=jnp.int32)
    qblk = t // MOBA_BLOCK
    own = jnp.broadcast_to(qblk[:, None], (b_, h_, sp, 1))
    n_sel = min(MOBA_TOPK, nb - 1)
    if n_sel > 0:
        kmean = jnp.mean(kb.astype(jnp.float32), axis=3)
        gate = jnp.einsum("bhtd,bhnd->bhtn", q.astype(jnp.float32), kmean)
        past = jnp.arange(nb, dtype=jnp.int32)[None, :] < qblk[:, None]
        gate = jnp.where(past, gate, -jnp.inf)
        _, sel = lax.top_k(gate, n_sel)
        sel_ok = sel < qblk[:, None]
        blocks = jnp.concatenate([sel.astype(jnp.int32), own], axis=-1)
        ok = jnp.concatenate([sel_ok, jnp.ones_like(own, dtype=bool)], axis=-1)
    else:
        blocks = own
        ok = jnp.ones_like(own, dtype=bool)
    ns = blocks.shape[-1]
    nc = sp // MOBA_Q_CHUNK

    def chunks(a):
        return jnp.moveaxis(a.reshape(b_, h_, nc, MOBA_Q_CHUNK, *a.shape[3:]), 2, 0)

    bi = jnp.arange(b_)[:, None, None, None]
    hi = jnp.arange(h_)[None, :, None, None]
    key_off = jnp.arange(MOBA_BLOCK, dtype=jnp.int32)
    scale = hd ** -0.5

    def step(args):
        qc, blk, okc, tc = args
        kg = kb[bi, hi, blk]
        vg = vb[bi, hi, blk]
        s = jnp.einsum("bhqd,bhqnkd->bhqnk", qc, kg).astype(jnp.float32) * scale
        kpos = blk[..., None] * MOBA_BLOCK + key_off
        allowed = okc[..., None] & (kpos <= tc[:, None, None])
        s = jnp.where(allowed, s, -jnp.inf)
        p = jax.nn.softmax(s.reshape(b_, h_, MOBA_Q_CHUNK, ns * MOBA_BLOCK), axis=-1)
        p = p.reshape(s.shape).astype(vg.dtype)
        return jnp.einsum("bhqnk,bhqnkd->bhqd", p, vg)

    out = lax.map(step, (chunks(q), chunks(blocks), chunks(ok), t.reshape(nc, MOBA_Q_CHUNK)))
    out = jnp.moveaxis(out, 0, 2).reshape(b_, h_, sp, hd)
    return out[:, :, :s_]


def retention(q, k, v):
    b_, h_, s_, dk = q.shape
    dv = v.shape[-1]
    c = RET_CHUNK
    nc = s_ // c
    log_g = jnp.log1p(-(2.0 ** (-5.0 - jnp.arange(h_, dtype=jnp.float32))))
    idx = jnp.arange(c, dtype=jnp.float32)
    diff = idx[:, None] - idx[None, :]
    decay_mask = jnp.exp(jnp.where(diff >= 0, log_g[:, None, None] * diff, -jnp.inf))
    q_decay = jnp.exp(log_g[:, None] * (idx + 1.0))
    k_decay = jnp.exp(log_g[:, None] * (c - 1.0 - idx))
    chunk_decay = jnp.exp(log_g * c)
    qc = q.astype(jnp.float32).reshape(b_, h_, nc, c, dk)
    kc = k.astype(jnp.float32).reshape(b_, h_, nc, c, dk)
    vc = v.astype(jnp.float32).reshape(b_, h_, nc, c, dv)
    inner = jnp.einsum("bhncd,bhnmd->bhncm", qc, kc) * decay_mask[None, :, None]
    inner_out = jnp.einsum("bhncm,bhnme->bhnce", inner, vc)
    kv = jnp.einsum("bhnmd,bhnme->bhnde", kc * k_decay[None, :, None, :, None], vc)

    def scan_fn(state, kv_i):
        return state * chunk_decay[None, :, None, None] + kv_i, state

    _, r_prev = lax.scan(scan_fn, jnp.zeros((b_, h_, dk, dv), jnp.float32), jnp.moveaxis(kv, 2, 0))
    r_prev = jnp.moveaxis(r_prev, 0, 2)
    cross = jnp.einsum("bhncd,bhnde->bhnce", qc * q_decay[None, :, None, :, None], r_prev)
    return (inner_out + cross).reshape(b_, h_, s_, dv)


def mixer(h, w_in, conv_dw_w, conv_dw_b, conv_ln_g, conv_ln_b, ret_norm_g,
          w_br_conv, w_br_att, w_br_ret, gate_b, w_out):
    b_, s_, _ = h.shape
    sizes = [CONV_CH, CONV_CH, ATT_W, ATT_W, ATT_W, RET_QK_W, RET_QK_W, RET_V_W, RET_V_W,
             N_BRANCH * D_MODEL]
    offsets = np.cumsum(sizes)[:-1].tolist()
    proj = h @ w_in
    conv_a, conv_g, qa, ka, va, qr, kr, vr, gr, gates = jnp.split(proj, offsets, axis=-1)
    pos = jnp.arange(s_, dtype=jnp.int32)

    y_conv = conv_module(conv_a, conv_g, conv_dw_w, conv_dw_b, conv_ln_g, conv_ln_b)

    q = rotary(to_heads(qa, ATT_HEADS, ATT_HD), pos, ROPE_DIMS, ROPE_THETA)
    k = rotary(to_heads(ka, ATT_HEADS, ATT_HD), pos, ROPE_DIMS, ROPE_THETA)
    v = to_heads(va, ATT_HEADS, ATT_HD)
    y_att = from_heads(moba_attention(q, k, v))

    rq = rotary(to_heads(qr, RET_HEADS, RET_DK), pos, RET_DK, RET_ROT_THETA)
    rk = rotary(to_heads(kr, RET_HEADS, RET_DK), pos, RET_DK, RET_ROT_THETA) * (RET_DK ** -0.5)
    rv = to_heads(vr, RET_HEADS, RET_DV)
    ro = retention(rq, rk, rv)
    ro = ro * lax.rsqrt(jnp.mean(ro * ro, axis=-1, keepdims=True) + EPS)
    ro = from_heads(ro) * ret_norm_g.astype(jnp.float32)
    y_ret = (jax.nn.silu(gr.astype(jnp.float32)) * ro).astype(h.dtype)

    g = jax.nn.sigmoid(gates + gate_b).reshape(b_, s_, N_BRANCH, D_MODEL)
    merged = (g[:, :, 0] * (y_conv @ w_br_conv)
              + g[:, :, 1] * (y_att @ w_br_att)
              + g[:, :, 2] * (y_ret @ w_br_ret))
    return merged @ w_out


def setup_inputs(seed: int = 0) -> dict:
    key = jax.random.key(seed)
    ks = jax.random.split(key, 32)
    L, D, F = DEPTH, D_MODEL, D_FF

    def w(k, shape, fan_in):
        return jax.random.normal(k, shape, jnp.float32) * (fan_in ** -0.5)

    def gain(k, shape):
        return 1.0 + 0.05 * jax.random.normal(k, shape, jnp.float32)

    def small(k, shape):
        return 0.02 * jax.random.normal(k, shape, jnp.float32)

    return {
        "x": jax.random.normal(ks[0], (BATCH, SEQ, D), jnp.float32),
        "ffn1_pre_g": gain(ks[1], (L, D)),
        "ffn1_w_gate": w(ks[2], (L, D, F), D),
        "ffn1_w_up": w(ks[3], (L, D, F), D),
        "ffn1_w_down": w(ks[4], (L, F, D), F),
        "ffn1_post_g": gain(ks[5], (L, D)),
        "mix_pre_g": gain(ks[6], (L, D)),
        "w_in": w(ks[7], (L, D, N_IN), D),
        "conv_dw_w": w(ks[8], (L, CONV_W, CONV_CH), CONV_W),
        "conv_dw_b": small(ks[9], (L, CONV_CH)),
        "conv_ln_g": gain(ks[10], (L, CONV_CH)),
        "conv_ln_b": small(ks[11], (L, CONV_CH)),
        "ret_norm_g": gain(ks[12], (L, RET_V_W)),
        "w_br_conv": w(ks[13], (L, CONV_CH, D), CONV_CH),
        "w_br_att": w(ks[14], (L, ATT_W, D), ATT_W),
        "w_br_ret": w(ks[15], (L, RET_V_W, D), RET_V_W),
        "gate_b": small(ks[16], (L, N_BRANCH * D)),
        "w_out": w(ks[17], (L, D, D), D),
        "mix_post_g": gain(ks[18], (L, D)),
        "ffn2_pre_g": gain(ks[19], (L, D)),
        "ffn2_w_gate": w(ks[20], (L, D, F), D),
        "ffn2_w_up": w(ks[21], (L, D, F), D),
        "ffn2_w_down": w(ks[22], (L, F, D), F),
        "ffn2_post_g": gain(ks[23], (L, D)),
    }


def reference(x, ffn1_pre_g, ffn1_w_gate, ffn1_w_up, ffn1_w_down, ffn1_post_g,
              mix_pre_g, w_in, conv_dw_w, conv_dw_b, conv_ln_g, conv_ln_b, ret_norm_g,
              w_br_conv, w_br_att, w_br_ret, gate_b, w_out, mix_post_g,
              ffn2_pre_g, ffn2_w_gate, ffn2_w_up, ffn2_w_down, ffn2_post_g):
    for l in range(DEPTH):
        f = swiglu(rms_norm(x, ffn1_pre_g[l]), ffn1_w_gate[l], ffn1_w_up[l], ffn1_w_down[l])
        x = x + 0.5 * rms_norm(f, ffn1_post_g[l])
        m = mixer(rms_norm(x, mix_pre_g[l]), w_in[l], conv_dw_w[l], conv_dw_b[l], conv_ln_g[l],
                  conv_ln_b[l], ret_norm_g[l], w_br_conv[l], w_br_att[l], w_br_ret[l],
                  gate_b[l], w_out[l])
        x = x + rms_norm(m, mix_post_g[l])
        f = swiglu(rms_norm(x, ffn2_pre_g[l]), ffn2_w_gate[l], ffn2_w_up[l], ffn2_w_down[l])
        x = x + 0.5 * rms_norm(f, ffn2_post_g[l])
    return x
```

```python
import functools
import math

import jax
import jax.numpy as jnp
from jax import lax
from jax.experimental import pallas as pl
from jax.experimental.pallas import tpu as pltpu

F32 = jnp.float32
BF16 = jnp.bfloat16

EPS = 1e-6
LANES = 128
V7X_VMEM_LIMIT_BYTES = 56 * 1024 * 1024
MASK_VALUE = -1e30

CONV_WIDTH = 31
CONV_HALO = 32
ATT_HEAD_DIM = 128
ATT_ROPE_DIMS = ATT_HEAD_DIM // 4
ATT_ROPE_THETA = 500000.0
MOBA_BLOCK = 256
MOBA_TOPK = 3
RET_HEADS = 8
RET_CHUNK = 128
RET_ROT_THETA = 10000.0
N_BRANCH = 3


def _params(*semantics):
    return pltpu.CompilerParams(dimension_semantics=semantics,
                                vmem_limit_bytes=V7X_VMEM_LIMIT_BYTES)


def _rms(x, g):
    return x * lax.rsqrt(jnp.mean(x * x, axis=-1, keepdims=True) + EPS) * g


def _silu(x):
    return x * jax.nn.sigmoid(x)


def _dot(a, b):
    return jnp.dot(a, b, preferred_element_type=F32)


def _dot_nt(a, b, precision=None):
    return lax.dot_general(a, b, (((1,), (1,)), ((), ())), precision=precision,
                           preferred_element_type=F32)


def _rope(x, cos, sin_signed, half, group):
    n = x.shape[-1]
    lane = lax.broadcasted_iota(jnp.int32, x.shape, x.ndim - 1)
    first = (lane % group) < half
    partner = jnp.where(first, pltpu.roll(x, n - half, x.ndim - 1), pltpu.roll(x, half, x.ndim - 1))
    return x * cos + partner * sin_signed


def _ffn_body(x_ref, pre_ref, wg_ref, wu_ref, wd_ref, post_ref, o_ref, h_ref, acc_ref):
    j = pl.program_id(1)

    @pl.when(j == 0)
    def _():
        h_ref[...] = _rms(x_ref[...], pre_ref[...]).astype(BF16)
        acc_ref[...] = jnp.zeros_like(acc_ref)

    h = h_ref[...]
    g = _dot(h, wg_ref[...])
    u = _dot(h, wu_ref[...])
    a = (_silu(g) * u).astype(BF16)
    acc_ref[...] += _dot(a, wd_ref[...])

    @pl.when(j == pl.num_programs(1) - 1)
    def _():
        o_ref[...] = x_ref[...] + 0.5 * _rms(acc_ref[...], post_ref[...])


def _ffn(x, pre_g, wg, wu, wd, post_g, *, tm=512, tf=512):
    m, d = x.shape
    f = wg.shape[1]
    return pl.pallas_call(
        _ffn_body,
        grid=(m // tm, f // tf),
        in_specs=[
            pl.BlockSpec((tm, d), lambda i, j: (i, 0)),
            pl.BlockSpec((1, d), lambda i, j: (0, 0)),
            pl.BlockSpec((d, tf), lambda i, j: (0, j)),
            pl.BlockSpec((d, tf), lambda i, j: (0, j)),
            pl.BlockSpec((tf, d), lambda i, j: (j, 0)),
            pl.BlockSpec((1, d), lambda i, j: (0, 0)),
        ],
        out_specs=pl.BlockSpec((tm, d), lambda i, j: (i, 0)),
        out_shape=jax.ShapeDtypeStruct((m, d), F32),
        scratch_shapes=[pltpu.VMEM((tm, d), BF16), pltpu.VMEM((tm, d), F32)],
        compiler_params=_params("parallel", "arbitrary"),
        name="ffn",
    )(x, pre_g, wg, wu, wd, post_g)


def _inproj_body(x_ref, g_ref, w_ref, o_ref, h_ref):
    @pl.when(pl.program_id(1) == 0)
    def _():
        h_ref[...] = _rms(x_ref[...], g_ref[...]).astype(BF16)

    o_ref[...] = _dot(h_ref[...], w_ref[...])


def _inproj(x, g, w, *, tm=1024, tn=1024):
    m, d = x.shape
    n = w.shape[1]
    return pl.pallas_call(
        _inproj_body,
        grid=(m // tm, n // tn),
        in_specs=[
            pl.BlockSpec((tm, d), lambda i, j: (i, 0)),
            pl.BlockSpec((1, d), lambda i, j: (0, 0)),
            pl.BlockSpec((d, tn), lambda i, j: (0, j)),
        ],
        out_specs=pl.BlockSpec((tm, tn), lambda i, j: (i, j)),
        out_shape=jax.ShapeDtypeStruct((m, n), F32),
        scratch_shapes=[pltpu.VMEM((tm, d), BF16)],
        compiler_params=_params("parallel", "arbitrary"),
        name="inproj",
    )(x, g, w)


def _conv_body(a_ref, g_ref, ap_ref, gp_ref, w_ref, b_ref, lng_ref, lnb_ref, o_ref,
               u_ref, y_ref, *, ts, rows_per_acc):
    nck = u_ref.shape[0]
    u = a_ref[...] * jax.nn.sigmoid(g_ref[...])
    up = ap_ref[...] * jax.nn.sigmoid(gp_ref[...])
    up = jnp.where(pl.program_id(1) > 0, up, 0.0)
    for c in range(nck):
        u_ref[c, 0:CONV_HALO, :] = up[:, c * LANES:(c + 1) * LANES]
        u_ref[c, CONV_HALO:CONV_HALO + ts, :] = u[:, c * LANES:(c + 1) * LANES]

    first = CONV_HALO - (CONV_WIDTH - 1)

    def chunk(c, carry):
        for r in range(ts // rows_per_acc):
            r0 = r * rows_per_acc
            acc = jnp.zeros((rows_per_acc, LANES), F32)
            for w in range(CONV_WIDTH):
                acc = acc + u_ref[c, r0 + first + w:r0 + first + w + rows_per_acc, :] * w_ref[c, w:w + 1, :]
            y_ref[c, r0:r0 + rows_per_acc, :] = acc
        return carry

    lax.fori_loop(0, nck, chunk, 0)

    y = jnp.concatenate([y_ref[c] for c in range(nck)], axis=-1) + b_ref[...]
    mu = jnp.mean(y, axis=-1, keepdims=True)
    yc = y - mu
    yn = yc * lax.rsqrt(jnp.mean(yc * yc, axis=-1, keepdims=True) + EPS) * lng_ref[...] + lnb_ref[...]
    o_ref[...] = _silu(yn).astype(BF16)


def _conv(proj, w3, b, ln_g, ln_b, *, batch, seq, ch, ts=256, rows_per_acc=64):
    m = proj.shape[0]
    nck = ch // LANES
    tiles = seq // ts
    halo_per_tile = ts // CONV_HALO

    def cur(col):
        return pl.BlockSpec((ts, ch), lambda bi, i: (bi * tiles + i, col))

    def prev(col):
        return pl.BlockSpec(
            (CONV_HALO, ch),
            lambda bi, i: (jnp.maximum((bi * tiles + i) * halo_per_tile - 1, 0), col))

    vec = pl.BlockSpec((1, ch), lambda bi, i: (0, 0))
    return pl.pallas_call(
        functools.partial(_conv_body, ts=ts, rows_per_acc=rows_per_acc),
        grid=(batch, tiles),
        in_specs=[cur(0), cur(1), prev(0), prev(1),
                  pl.BlockSpec((nck, CONV_WIDTH, LANES), lambda bi, i: (0, 0, 0)),
                  vec, vec, vec],
        out_specs=pl.BlockSpec((ts, ch), lambda bi, i: (bi * tiles + i, 0)),
        out_shape=jax.ShapeDtypeStruct((m, ch), BF16),
        scratch_shapes=[pltpu.VMEM((nck, CONV_HALO + ts, LANES), F32),
                        pltpu.VMEM((nck, ts, LANES), F32)],
        compiler_params=_params("parallel", "parallel"),
        name="conv",
    )(proj, proj, proj, proj, w3, b, ln_g, ln_b)


def _moba_body(q_ref, k_ref, v_ref, cq_ref, sq_ref, ck_ref, sk_ref, o_ref,
               kb_ref, vt_ref, km_ref, bias_ref, *, nb, n_sel, scale):
    blk = MOBA_BLOCK
    half = ATT_ROPE_DIMS // 2
    i = pl.program_id(2)

    @pl.when(i == 0)
    def _():
        for j in range(nb):
            rows = slice(j * blk, (j + 1) * blk)
            kj = _rope(k_ref[rows, :], ck_ref[rows, :], sk_ref[rows, :], half, ATT_HEAD_DIM)
            kb_ref[j] = kj.astype(BF16)
            km_ref[j:j + 1, :] = jnp.mean(kj, axis=0, keepdims=True)
            vt_ref[j] = v_ref[rows, :].T.astype(BF16)

    q = _rope(q_ref[...], cq_ref[...], sq_ref[...], half, ATT_HEAD_DIM)
    qb = q.astype(BF16)

    gate = _dot_nt(km_ref[...], q, precision=lax.Precision.HIGHEST)
    rows = lax.broadcasted_iota(jnp.int32, (nb, blk), 0)
    rank = jnp.zeros((nb, blk), jnp.int32)
    for jp in range(nb):
        gj = gate[jp:jp + 1, :]
        beats = (gj > gate) | ((gj == gate) & (jp < rows))
        rank = rank + jnp.where(beats & (jp < i), 1, 0)
    sel = (rank < n_sel) & (rows < i)
    bias_ref[...] = jnp.where(sel, 0.0, MASK_VALUE)

    s = _dot_nt(kb_ref[i], qb) * scale
    kidx = lax.broadcasted_iota(jnp.int32, (blk, blk), 0)
    qidx = lax.broadcasted_iota(jnp.int32, (blk, blk), 1)
    s = jnp.where(kidx <= qidx, s, MASK_VALUE)
    m0 = jnp.max(s, axis=0, keepdims=True)
    p = jnp.exp(s - m0)
    l0 = jnp.sum(p, axis=0, keepdims=True)
    acc0 = _dot(vt_ref[i], p.astype(BF16))

    def past_block(j, carry):
        m, l, acc = carry
        sj = _dot_nt(kb_ref[j], qb) * scale + bias_ref[pl.ds(j, 1), :]
        m_new = jnp.maximum(m, jnp.max(sj, axis=0, keepdims=True))
        alpha = jnp.exp(m - m_new)
        pj = jnp.exp(sj - m_new)
        l = alpha * l + jnp.sum(pj, axis=0, keepdims=True)
        acc = acc * alpha + _dot(vt_ref[j], pj.astype(BF16))
        return m_new, l, acc

    _, l, acc = lax.fori_loop(0, i, past_block, (m0, l0, acc0))
    o_ref[...] = (acc / l).T.astype(BF16)


def _moba(proj, cos_t, sin_t, *, batch, seq, heads, q_col, k_col, v_col):
    m = proj.shape[0]
    hd = ATT_HEAD_DIM
    nb = seq // MOBA_BLOCK
    n_sel = min(MOBA_TOPK, nb - 1)
    qblk = pl.BlockSpec((MOBA_BLOCK, hd), lambda b, h, i: (b * nb + i, q_col // hd + h))
    tab_q = pl.BlockSpec((MOBA_BLOCK, hd), lambda b, h, i: (i, 0))
    tab_k = pl.BlockSpec((seq, hd), lambda b, h, i: (0, 0))
    return pl.pallas_call(
        functools.partial(_moba_body, nb=nb, n_sel=n_sel, scale=hd ** -0.5),
        grid=(batch, heads, nb),
        in_specs=[
            qblk,
            pl.BlockSpec((seq, hd), lambda b, h, i: (b, k_col // hd + h)),
            pl.BlockSpec((seq, hd), lambda b, h, i: (b, v_col // hd + h)),
            tab_q, tab_q, tab_k, tab_k,
        ],
        out_specs=pl.BlockSpec((MOBA_BLOCK, hd), lambda b, h, i: (b * nb + i, h)),
        out_shape=jax.ShapeDtypeStruct((m, heads * hd), BF16),
        scratch_shapes=[
            pltpu.VMEM((nb, MOBA_BLOCK, hd), BF16),
            pltpu.VMEM((nb, hd, MOBA_BLOCK), BF16),
            pltpu.VMEM((nb, hd), F32),
            pltpu.VMEM((nb, MOBA_BLOCK), F32),
        ],
        compiler_params=_params("parallel", "parallel", "arbitrary"),
        name="moba",
    )(proj, proj, proj, cos_t, sin_t, cos_t, sin_t)


def _ret_body(q_ref, k_ref, v_ref, gr_ref, cos_ref, sin_ref, dm_ref, qd_ref, kd_ref, cd_ref,
              ng_ref, o_ref, st_ref, *, heads, dk, dv, chunks):
    c = RET_CHUNK
    per_group = LANES // dk

    @pl.when(pl.program_id(1) == 0)
    def _():
        st_ref[...] = jnp.zeros_like(st_ref)

    for cc in range(chunks):
        rows = slice(cc * c, (cc + 1) * c)
        cos = cos_ref[rows, :]
        sin = sin_ref[rows, :]
        for hg in range(heads // per_group):
            lanes = slice(hg * LANES, (hg + 1) * LANES)
            qr = _rope(q_ref[rows, lanes], cos, sin, dk // 2, dk)
            kr = _rope(k_ref[rows, lanes], cos, sin, dk // 2, dk) * (dk ** -0.5)
            krt = kr.T
            for hh in range(per_group):
                h = hg * per_group + hh
                hl = slice(hh * dk, (hh + 1) * dk)
                vl = slice(h * dv, (h + 1) * dv)
                qh = qr[:, hl]
                vb = v_ref[rows, vl].astype(BF16)
                inner = _dot_nt(qh.astype(BF16), kr[:, hl].astype(BF16)) * dm_ref[h]
                ro = _dot(inner.astype(BF16), vb)
                st = st_ref[h]
                ro = ro + _dot((qh * qd_ref[h]).astype(BF16), st.astype(BF16))
                kv = _dot((krt[hl, :] * kd_ref[h]).astype(BF16), vb)
                st_ref[h] = st * cd_ref[h] + kv
                ro = ro * lax.rsqrt(jnp.mean(ro * ro, axis=-1, keepdims=True) + EPS) * ng_ref[:, vl]
                o_ref[rows, vl] = (_silu(gr_ref[rows, vl]) * ro).astype(BF16)


def _retention(proj, cos_t, sin_t, dm, qd, kd, cd, norm_g, *, batch, seq, heads, dk, dv,
               q_col, k_col, v_col, g_col, chunks=4):
    m = proj.shape[0]
    ts = chunks * RET_CHUNK
    tiles = seq // ts
    qk_w = heads * dk
    v_w = heads * dv

    def tok(width, col):
        return pl.BlockSpec((ts, width), lambda b, t: (b * tiles + t, col // width))

    def const(shape):
        return pl.BlockSpec(shape, lambda b, t: (0,) * len(shape))

    return pl.pallas_call(
        functools.partial(_ret_body, heads=heads, dk=dk, dv=dv, chunks=chunks),
        grid=(batch, tiles),
        in_specs=[
            tok(qk_w, q_col), tok(qk_w, k_col), tok(v_w, v_col), tok(v_w, g_col),
            pl.BlockSpec((ts, LANES), lambda b, t: (t, 0)),
            pl.BlockSpec((ts, LANES), lambda b, t: (t, 0)),
            const(dm.shape), const(qd.shape), const(kd.shape), const(cd.shape),
            const((1, v_w)),
        ],
        out_specs=pl.BlockSpec((ts, v_w), lambda b, t: (b * tiles + t, 0)),
        out_shape=jax.ShapeDtypeStruct((m, v_w), BF16),
        scratch_shapes=[pltpu.VMEM((heads, dk, dv), F32)],
        compiler_params=_params("parallel", "arbitrary"),
        name="retention",
    )(proj, proj, proj, proj, cos_t, sin_t, dm, qd, kd, cd, norm_g)


def _merge_body(x_ref, yc_ref, ya_ref, yr_ref, wc_ref, wa_ref, wr_ref, g0_ref, g1_ref, g2_ref,
                gb_ref, wo_ref, post_ref, o_ref, mg_ref, *, tn):
    j = pl.program_id(1)
    gb = gb_ref[...]
    merged = (jax.nn.sigmoid(g0_ref[...] + gb[0:1, :]) * _dot(yc_ref[...], wc_ref[...])
              + jax.nn.sigmoid(g1_ref[...] + gb[1:2, :]) * _dot(ya_ref[...], wa_ref[...])
              + jax.nn.sigmoid(g2_ref[...] + gb[2:3, :]) * _dot(yr_ref[...], wr_ref[...]))
    mg_ref[j] = merged.astype(BF16)

    @pl.when(j == pl.num_programs(1) - 1)
    def _():
        nt = mg_ref.shape[0]
        mo = _dot(mg_ref[0], wo_ref[0:tn, :])
        for t in range(1, nt):
            mo = mo + _dot(mg_ref[t], wo_ref[t * tn:(t + 1) * tn, :])
        o_ref[...] = x_ref[...] + _rms(mo, post_ref[...])


def _merge(x, yc, ya, yr, wc, wa, wr, proj, gate_b3, wo, post_g, *, gates_col, tm=256, tn=512):
    m, d = x.shape
    nt = d // tn

    def ytile(a):
        return pl.BlockSpec((tm, a.shape[1]), lambda i, j: (i, 0))

    def wtile(w):
        return pl.BlockSpec((w.shape[0], tn), lambda i, j: (0, j))

    def gtile(br):
        return pl.BlockSpec((tm, tn), lambda i, j: (i, (gates_col + br * d) // tn + j))

    return pl.pallas_call(
        functools.partial(_merge_body, tn=tn),
        grid=(m // tm, nt),
        in_specs=[
            pl.BlockSpec((tm, d), lambda i, j: (i, 0)),
            ytile(yc), ytile(ya), ytile(yr), wtile(wc), wtile(wa), wtile(wr),
            gtile(0), gtile(1), gtile(2),
            pl.BlockSpec((None, N_BRANCH, tn), lambda i, j: (j, 0, 0)),
            pl.BlockSpec((d, d), lambda i, j: (0, 0)),
            pl.BlockSpec((1, d), lambda i, j: (0, 0)),
        ],
        out_specs=pl.BlockSpec((tm, d), lambda i, j: (i, 0)),
        out_shape=jax.ShapeDtypeStruct((m, d), F32),
        scratch_shapes=[pltpu.VMEM((nt, tm, tn), BF16)],
        compiler_params=_params("parallel", "arbitrary"),
        name="merge",
    )(x, yc, ya, yr, wc, wa, wr, proj, proj, proj, gate_b3, wo, post_g)


def _rope_tables(seq, n_rot, theta, group):
    half = n_rot // 2
    inv = 1.0 / (theta ** (jnp.arange(half, dtype=F32) / half))
    ang = jnp.arange(seq, dtype=jnp.int32).astype(F32)[:, None] * inv[None, :]
    cos, sin = jnp.cos(ang), jnp.sin(ang)
    rest = group - n_rot
    cos_g = jnp.concatenate([cos, cos, jnp.ones((seq, rest), F32)], axis=-1)
    sin_g = jnp.concatenate([-sin, sin, jnp.zeros((seq, rest), F32)], axis=-1)
    reps = LANES // group
    return jnp.tile(cos_g, (1, reps)), jnp.tile(sin_g, (1, reps))


def _retention_constants(heads):
    c = RET_CHUNK
    log_g = jnp.log1p(-(2.0 ** (-5.0 - jnp.arange(heads, dtype=F32))))
    idx = jnp.arange(c, dtype=F32)
    diff = idx[:, None] - idx[None, :]
    decay_mask = jnp.exp(jnp.where(diff >= 0, log_g[:, None, None] * diff, -jnp.inf))
    q_decay = jnp.exp(log_g[:, None] * (idx + 1.0))[:, :, None]
    k_decay = jnp.exp(log_g[:, None] * (c - 1.0 - idx))[:, None, :]
    chunk_decay = jnp.broadcast_to(jnp.exp(log_g * c)[:, None, None], (heads, 1, LANES))
    return decay_mask, q_decay, k_decay, chunk_decay


def kernel(x, ffn1_pre_g, ffn1_w_gate, ffn1_w_up, ffn1_w_down, ffn1_post_g, mix_pre_g, w_in, conv_dw_w, conv_dw_b, conv_ln_g, conv_ln_b, ret_norm_g, w_br_conv, w_br_att, w_br_ret, gate_b, w_out, mix_post_g, ffn2_pre_g, ffn2_w_gate, ffn2_w_up, ffn2_w_down, ffn2_post_g):
    batch, seq, d = x.shape
    depth = w_in.shape[0]
    conv_ch = conv_dw_w.shape[2]
    att_w = w_br_att.shape[1]
    ret_v_w = w_br_ret.shape[1]
    att_heads = att_w // ATT_HEAD_DIM
    ret_dv = ret_v_w // RET_HEADS
    ret_dk = ret_dv // 2
    ret_qk_w = RET_HEADS * ret_dk
    sizes = [conv_ch, conv_ch, att_w, att_w, att_w, ret_qk_w, ret_qk_w, ret_v_w, ret_v_w, N_BRANCH * d]
    assert sum(sizes) == w_in.shape[2]
    cols = [0]
    for s in sizes[:-1]:
        cols.append(cols[-1] + s)
    (_, _, qa_col, ka_col, va_col, qr_col, kr_col, vr_col, gr_col, gates_col) = cols

    att_cos, att_sin = _rope_tables(seq, ATT_ROPE_DIMS, ATT_ROPE_THETA, ATT_HEAD_DIM)
    ret_cos, ret_sin = _rope_tables(seq, ret_dk, RET_ROT_THETA, ret_dk)
    dm, qd, kd, cd = _retention_constants(RET_HEADS)

    bf = lambda w: w.astype(BF16)
    row = lambda v: v.reshape(1, -1)
    merge_tn = 512

    xf = x.reshape(batch * seq, d)
    for l in range(depth):
        xf = _ffn(xf, row(ffn1_pre_g[l]), bf(ffn1_w_gate[l]), bf(ffn1_w_up[l]), bf(ffn1_w_down[l]),
                  row(ffn1_post_g[l]))
        proj = _inproj(xf, row(mix_pre_g[l]), bf(w_in[l]))
        w3 = conv_dw_w[l].reshape(CONV_WIDTH, conv_ch // LANES, LANES).transpose(1, 0, 2)
        y_conv = _conv(proj, w3, row(conv_dw_b[l]), row(conv_ln_g[l]), row(conv_ln_b[l]),
                       batch=batch, seq=seq, ch=conv_ch)
        y_att = _moba(proj, att_cos, att_sin, batch=batch, seq=seq, heads=att_heads,
                      q_col=qa_col, k_col=ka_col, v_col=va_col)
        y_ret = _retention(proj, ret_cos, ret_sin, dm, qd, kd, cd, row(ret_norm_g[l]),
                           batch=batch, seq=seq, heads=RET_HEADS, dk=ret_dk, dv=ret_dv,
                           q_col=qr_col, k_col=kr_col, v_col=vr_col, g_col=gr_col)
        gate_b3 = gate_b[l].reshape(N_BRANCH, d // merge_tn, merge_tn).transpose(1, 0, 2)
        xf = _merge(xf, y_conv, y_att, y_ret, bf(w_br_conv[l]), bf(w_br_att[l]), bf(w_br_ret[l]),
                    proj, gate_b3, bf(w_out[l]), row(mix_post_g[l]), gates_col=gates_col, tn=merge_tn)
        xf = _ffn(xf, row(ffn2_pre_g[l]), bf(ffn2_w_gate[l]), bf(ffn2_w_up[l]), bf(ffn2_w_down[l]),
                  row(ffn2_post_g[l]))
    return xf.reshape(batch, seq, d)
```

```python
import functools
import math

import jax
import jax.numpy as jnp
from jax import lax
from jax.experimental import pallas as pl
from jax.experimental.pallas import tpu as pltpu

F32 = jnp.float32
BF16 = jnp.bfloat16

EPS = 1e-6
LANES = 128
V7X_VMEM_LIMIT_BYTES = 56 * 1024 * 1024
MASK_VALUE = -1e30

CONV_WIDTH = 31
CONV_HALO = 32
ATT_HEAD_DIM = 128
ATT_ROPE_DIMS = ATT_HEAD_DIM // 4
ATT_ROPE_THETA = 500000.0
MOBA_BLOCK = 256
MOBA_TOPK = 3
RET_HEADS = 8
RET_CHUNK = 128
RET_ROT_THETA = 10000.0
N_BRANCH = 3


def _params(*semantics):
    return pltpu.CompilerParams(dimension_semantics=semantics,
                                vmem_limit_bytes=V7X_VMEM_LIMIT_BYTES)


def _rms(x, g):
    return x * lax.rsqrt(jnp.mean(x * x, axis=-1, keepdims=True) + EPS) * g


def _silu(x):
    return x * jax.nn.sigmoid(x)


def _dot(a, b):
    return jnp.dot(a, b, preferred_element_type=F32)


def _dot_nt(a, b, precision=None):
    return lax.dot_general(a, b, (((1,), (1,)), ((), ())), precision=precision,
                           preferred_element_type=F32)


def _rope(x, cos, sin_signed, half, group):
    n = x.shape[-1]
    lane = lax.broadcasted_iota(jnp.int32, x.shape, x.ndim - 1)
    first = (lane % group) < half
    partner = jnp.where(first, pltpu.roll(x, n - half, x.ndim - 1), pltpu.roll(x, half, x.ndim - 1))
    return x * cos + partner * sin_signed


def _ffn_body(x_ref, pre_ref, wg_ref, wu_ref, wd_ref, post_ref, o_ref, h_ref, acc_ref):
    j = pl.program_id(1)

    @pl.when(j == 0)
    def _():
        h_ref[...] = _rms(x_ref[...], pre_ref[...]).astype(BF16)
        acc_ref[...] = jnp.zeros_like(acc_ref)

    h = h_ref[...]
    g = _dot(h, wg_ref[...])
    u = _dot(h, wu_ref[...])
    a = (_silu(g) * u).astype(BF16)
    acc_ref[...] += _dot(a, wd_ref[...])

    @pl.when(j == pl.num_programs(1) - 1)
    def _():
        o_ref[...] = x_ref[...] + 0.5 * _rms(acc_ref[...], post_ref[...])


def _ffn(x, pre_g, wg, wu, wd, post_g, *, tm=512, tf=512):
    m, d = x.shape
    f = wg.shape[1]
    return pl.pallas_call(
        _ffn_body,
        grid=(m // tm, f // tf),
        in_specs=[
            pl.BlockSpec((tm, d), lambda i, j: (i, 0)),
            pl.BlockSpec((1, d), lambda i, j: (0, 0)),
            pl.BlockSpec((d, tf), lambda i, j: (0, j)),
            pl.BlockSpec((d, tf), lambda i, j: (0, j)),
            pl.BlockSpec((tf, d), lambda i, j: (j, 0)),
            pl.BlockSpec((1, d), lambda i, j: (0, 0)),
        ],
        out_specs=pl.BlockSpec((tm, d), lambda i, j: (i, 0)),
        out_shape=jax.ShapeDtypeStruct((m, d), F32),
        scratch_shapes=[pltpu.VMEM((tm, d), BF16), pltpu.VMEM((tm, d), F32)],
        compiler_params=_params("parallel", "arbitrary"),
        name="ffn",
    )(x, pre_g, wg, wu, wd, post_g)


def _inproj_body(x_ref, g_ref, w_ref, o_ref, h_ref):
    @pl.when(pl.program_id(1) == 0)
    def _():
        h_ref[...] = _rms(x_ref[...], g_ref[...]).astype(BF16)

    o_ref[...] = _dot(h_ref[...], w_ref[...])


def _inproj(x, g, w, *, tm=1024, tn=1024):
    m, d = x.shape
    n = w.shape[1]
    return pl.pallas_call(
        _inproj_body,
        grid=(m // tm, n // tn),
        in_specs=[
            pl.BlockSpec((tm, d), lambda i, j: (i, 0)),
            pl.BlockSpec((1, d), lambda i, j: (0, 0)),
            pl.BlockSpec((d, tn), lambda i, j: (0, j)),
        ],
        out_specs=pl.BlockSpec((tm, tn), lambda i, j: (i, j)),
        out_shape=jax.ShapeDtypeStruct((m, n), F32),
        scratch_shapes=[pltpu.VMEM((tm, d), BF16)],
        compiler_params=_params("parallel", "arbitrary"),
        name="inproj",
    )(x, g, w)


def _conv_body(a_ref, g_ref, ap_ref, gp_ref, w_ref, b_ref, lng_ref, lnb_ref, o_ref,
               u_ref, y_ref, *, ts, rows_per_acc):
    nck = u_ref.shape[0]
    u = a_ref[...] * jax.nn.sigmoid(g_ref[...])
    up = ap_ref[...] * jax.nn.sigmoid(gp_ref[...])
    up = jnp.where(pl.program_id(1) > 0, up, 0.0)
    for c in range(nck):
        u_ref[c, 0:CONV_HALO, :] = up[:, c * LANES:(c + 1) * LANES]
        u_ref[c, CONV_HALO:CONV_HALO + ts, :] = u[:, c * LANES:(c + 1) * LANES]

    first = CONV_HALO - (CONV_WIDTH - 1)

    def chunk(c, carry):
        for r in range(ts // rows_per_acc):
            r0 = r * rows_per_acc
            acc = jnp.zeros((rows_per_acc, LANES), F32)
            for w in range(CONV_WIDTH):
                acc = acc + u_ref[c, r0 + first + w:r0 + first + w + rows_per_acc, :] * w_ref[c, w:w + 1, :]
            y_ref[c, r0:r0 + rows_per_acc, :] = acc
        return carry

    lax.fori_loop(0, nck, chunk, 0)

    y = jnp.concatenate([y_ref[c] for c in range(nck)], axis=-1) + b_ref[...]
    mu = jnp.mean(y, axis=-1, keepdims=True)
    yc = y - mu
    yn = yc * lax.rsqrt(jnp.mean(yc * yc, axis=-1, keepdims=True) + EPS) * lng_ref[...] + lnb_ref[...]
    o_ref[...] = _silu(yn).astype(BF16)


def _conv(proj, w3, b, ln_g, ln_b, *, batch, seq, ch, ts=256, rows_per_acc=64):
    m = proj.shape[0]
    nck = ch // LANES
    tiles = seq // ts
    halo_per_tile = ts // CONV_HALO

    def cur(col):
        return pl.BlockSpec((ts, ch), lambda bi, i: (bi * tiles + i, col))

    def prev(col):
        return pl.BlockSpec(
            (CONV_HALO, ch),
            lambda bi, i: (jnp.maximum((bi * tiles + i) * halo_per_tile - 1, 0), col))

    vec = pl.BlockSpec((1, ch), lambda bi, i: (0, 0))
    return pl.pallas_call(
        functools.partial(_conv_body, ts=ts, rows_per_acc=rows_per_acc),
        grid=(batch, tiles),
        in_specs=[cur(0), cur(1), prev(0), prev(1),
                  pl.BlockSpec((nck, CONV_WIDTH, LANES), lambda bi, i: (0, 0, 0)),
                  vec, vec, vec],
        out_specs=pl.BlockSpec((ts, ch), lambda bi, i: (bi * tiles + i, 0)),
        out_shape=jax.ShapeDtypeStruct((m, ch), BF16),
        scratch_shapes=[pltpu.VMEM((nck, CONV_HALO + ts, LANES), F32),
                        pltpu.VMEM((nck, ts, LANES), F32)],
        compiler_params=_params("parallel", "parallel"),
        name="conv",
    )(proj, proj, proj, proj, w3, b, ln_g, ln_b)


def _moba_body(q_ref, k_ref, v_ref, cos_ref, sin_ref, o_ref, kb_ref, vt_ref, km_ref, *, nb, n_sel, scale):
    blk = MOBA_BLOCK
    half = ATT_ROPE_DIMS // 2

    for j in range(nb):
        rows = slice(j * blk, (j + 1) * blk)
        kj = _rope(k_ref[rows, :], cos_ref[rows, :], sin_ref[rows, :], half, ATT_HEAD_DIM)
        kb_ref[rows, :] = kj.astype(BF16)
        km_ref[j:j + 1, :] = jnp.mean(kj, axis=0, keepdims=True)
        vt_ref[:, rows] = v_ref[rows, :].T.astype(BF16)

    kidx = lax.broadcasted_iota(jnp.int32, (blk, blk), 0)
    qidx = lax.broadcasted_iota(jnp.int32, (blk, blk), 1)
    causal = kidx <= qidx

    for i in range(nb):
        rows = slice(i * blk, (i + 1) * blk)
        nk = (i + 1) * blk
        q = _rope(q_ref[rows, :], cos_ref[rows, :], sin_ref[rows, :], half, ATT_HEAD_DIM)
        s = _dot_nt(kb_ref[0:nk, :], q.astype(BF16)) * scale
        parts = []
        if i > n_sel:
            gate = _dot_nt(km_ref[...], q, precision=lax.Precision.HIGHEST)
            brow = lax.broadcasted_iota(jnp.int32, (nb, blk), 0)
            rank = jnp.zeros((nb, blk), jnp.int32)
            for jp in range(i):
                gj = gate[jp:jp + 1, :]
                rank = rank + jnp.where((gj > gate) | ((gj == gate) & (jp < brow)), 1, 0)
            bias = jnp.where(rank < n_sel, 0.0, MASK_VALUE)
            for j in range(i):
                parts.append(s[j * blk:(j + 1) * blk, :] + bias[j:j + 1, :])
        elif i > 0:
            parts.append(s[0:i * blk, :])
        parts.append(jnp.where(causal, s[i * blk:nk, :], MASK_VALUE))
        s = jnp.concatenate(parts, axis=0) if len(parts) > 1 else parts[0]
        m = jnp.max(s, axis=0, keepdims=True)
        p = jnp.exp(s - m)
        l = jnp.sum(p, axis=0, keepdims=True)
        acc = _dot(vt_ref[:, 0:nk], p.astype(BF16))
        o_ref[rows, :] = (acc / l).T.astype(BF16)


def _moba(proj, cos_t, sin_t, *, batch, seq, heads, q_col, k_col, v_col):
    m = proj.shape[0]
    hd = ATT_HEAD_DIM
    nb = seq // MOBA_BLOCK
    n_sel = min(MOBA_TOPK, nb - 1)
    table = pl.BlockSpec((seq, hd), lambda b, h: (0, 0))

    def head(col):
        return pl.BlockSpec((seq, hd), lambda b, h: (b, col // hd + h))

    return pl.pallas_call(
        functools.partial(_moba_body, nb=nb, n_sel=n_sel, scale=hd ** -0.5),
        grid=(batch, heads),
        in_specs=[head(q_col), head(k_col), head(v_col), table, table],
        out_specs=pl.BlockSpec((seq, hd), lambda b, h: (b, h)),
        out_shape=jax.ShapeDtypeStruct((m, heads * hd), BF16),
        scratch_shapes=[
            pltpu.VMEM((seq, hd), BF16),
            pltpu.VMEM((hd, seq), BF16),
            pltpu.VMEM((nb, hd), F32),
        ],
        compiler_params=_params("parallel", "parallel"),
        name="moba",
    )(proj, proj, proj, cos_t, sin_t)


def _ret_body(q_ref, k_ref, v_ref, gr_ref, cos_ref, sin_ref, dm_ref, qd_ref, kd_ref, cd_ref,
              ng_ref, o_ref, st_ref, *, heads, dk, dv, chunks):
    c = RET_CHUNK
    per_group = LANES // dk

    @pl.when(pl.program_id(1) == 0)
    def _():
        st_ref[...] = jnp.zeros_like(st_ref)

    for cc in range(chunks):
        rows = slice(cc * c, (cc + 1) * c)
        cos = cos_ref[rows, :]
        sin = sin_ref[rows, :]
        for hg in range(heads // per_group):
            lanes = slice(hg * LANES, (hg + 1) * LANES)
            qr = _rope(q_ref[rows, lanes], cos, sin, dk // 2, dk)
            kr = _rope(k_ref[rows, lanes], cos, sin, dk // 2, dk) * (dk ** -0.5)
            krt = kr.T
            for hh in range(per_group):
                h = hg * per_group + hh
                hl = slice(hh * dk, (hh + 1) * dk)
                vl = slice(h * dv, (h + 1) * dv)
                qh = qr[:, hl]
                vb = v_ref[rows, vl].astype(BF16)
                inner = _dot_nt(qh.astype(BF16), kr[:, hl].astype(BF16)) * dm_ref[h]
                ro = _dot(inner.astype(BF16), vb)
                st = st_ref[h]
                ro = ro + _dot((qh * qd_ref[h]).astype(BF16), st.astype(BF16))
                kv = _dot((krt[hl, :] * kd_ref[h]).astype(BF16), vb)
                st_ref[h] = st * cd_ref[h] + kv
                ro = ro * lax.rsqrt(jnp.mean(ro * ro, axis=-1, keepdims=True) + EPS) * ng_ref[:, vl]
                o_ref[rows, vl] = (_silu(gr_ref[rows, vl]) * ro).astype(BF16)


def _retention(proj, cos_t, sin_t, dm, qd, kd, cd, norm_g, *, batch, seq, heads, dk, dv,
               q_col, k_col, v_col, g_col, chunks=4):
    m = proj.shape[0]
    ts = chunks * RET_CHUNK
    tiles = seq // ts
    qk_w = heads * dk
    v_w = heads * dv

    def tok(width, col):
        return pl.BlockSpec((ts, width), lambda b, t: (b * tiles + t, col // width))

    def const(shape):
        return pl.BlockSpec(shape, lambda b, t: (0,) * len(shape))

    return pl.pallas_call(
        functools.partial(_ret_body, heads=heads, dk=dk, dv=dv, chunks=chunks),
        grid=(batch, tiles),
        in_specs=[
            tok(qk_w, q_col), tok(qk_w, k_col), tok(v_w, v_col), tok(v_w, g_col),
            pl.BlockSpec((ts, LANES), lambda b, t: (t, 0)),
            pl.BlockSpec((ts, LANES), lambda b, t: (t, 0)),
            const(dm.shape), const(qd.shape), const(kd.shape), const(cd.shape),
            const((1, v_w)),
        ],
        out_specs=pl.BlockSpec((ts, v_w), lambda b, t: (b * tiles + t, 0)),
        out_shape=jax.ShapeDtypeStruct((m, v_w), BF16),
        scratch_shapes=[pltpu.VMEM((heads, dk, dv), F32)],
        compiler_params=_params("parallel", "arbitrary"),
        name="retention",
    )(proj, proj, proj, proj, cos_t, sin_t, dm, qd, kd, cd, norm_g)


def _merge_body(x_ref, yc_ref, ya_ref, yr_ref, wc_ref, wa_ref, wr_ref, g0_ref, g1_ref, g2_ref,
                gb_ref, wo_ref, post_ref, o_ref, mg_ref, *, tn):
    j = pl.program_id(1)
    gb = gb_ref[...]
    merged = (jax.nn.sigmoid(g0_ref[...] + gb[0:1, :]) * _dot(yc_ref[...], wc_ref[...])
              + jax.nn.sigmoid(g1_ref[...] + gb[1:2, :]) * _dot(ya_ref[...], wa_ref[...])
              + jax.nn.sigmoid(g2_ref[...] + gb[2:3, :]) * _dot(yr_ref[...], wr_ref[...]))
    mg_ref[j] = merged.astype(BF16)

    @pl.when(j == pl.num_programs(1) - 1)
    def _():
        nt = mg_ref.shape[0]
        mo = _dot(mg_ref[0], wo_ref[0:tn, :])
        for t in range(1, nt):
            mo = mo + _dot(mg_ref[t], wo_ref[t * tn:(t + 1) * tn, :])
        o_ref[...] = x_ref[...] + _rms(mo, post_ref[...])


def _merge(x, yc, ya, yr, wc, wa, wr, proj, gate_b3, wo, post_g, *, gates_col, tm=256, tn=512):
    m, d = x.shape
    nt = d // tn

    def ytile(a):
        return pl.BlockSpec((tm, a.shape[1]), lambda i, j: (i, 0))

    def wtile(w):
        return pl.BlockSpec((w.shape[0], tn), lambda i, j: (0, j))

    def gtile(br):
        return pl.BlockSpec((tm, tn), lambda i, j: (i, (gates_col + br * d) // tn + j))

    return pl.pallas_call(
        functools.partial(_merge_body, tn=tn),
        grid=(m // tm, nt),
        in_specs=[
            pl.BlockSpec((tm, d), lambda i, j: (i, 0)),
            ytile(yc), ytile(ya), ytile(yr), wtile(wc), wtile(wa), wtile(wr),
            gtile(0), gtile(1), gtile(2),
            pl.BlockSpec((None, N_BRANCH, tn), lambda i, j: (j, 0, 0)),
            pl.BlockSpec((d, d), lambda i, j: (0, 0)),
            pl.BlockSpec((1, d), lambda i, j: (0, 0)),
        ],
        out_specs=pl.BlockSpec((tm, d), lambda i, j: (i, 0)),
        out_shape=jax.ShapeDtypeStruct((m, d), F32),
        scratch_shapes=[pltpu.VMEM((nt, tm, tn), BF16)],
        compiler_params=_params("parallel", "arbitrary"),
        name="merge",
    )(x, yc, ya, yr, wc, wa, wr, proj, proj, proj, gate_b3, wo, post_g)


def _rope_tables(seq, n_rot, theta, group):
    half = n_rot // 2
    inv = 1.0 / (theta ** (jnp.arange(half, dtype=F32) / half))
    ang = jnp.arange(seq, dtype=jnp.int32).astype(F32)[:, None] * inv[None, :]
    cos, sin = jnp.cos(ang), jnp.sin(ang)
    rest = group - n_rot
    cos_g = jnp.concatenate([cos, cos, jnp.ones((seq, rest), F32)], axis=-1)
    sin_g = jnp.concatenate([-sin, sin, jnp.zeros((seq, rest), F32)], axis=-1)
    reps = LANES // group
    return jnp.tile(cos_g, (1, reps)), jnp.tile(sin_g, (1, reps))


def _retention_constants(heads):
    c = RET_CHUNK
    log_g = jnp.log1p(-(2.0 ** (-5.0 - jnp.arange(heads, dtype=F32))))
    idx = jnp.arange(c, dtype=F32)
    diff = idx[:, None] - idx[None, :]
    decay_mask = jnp.exp(jnp.where(diff >= 0, log_g[:, None, None] * diff, -jnp.inf))
    q_decay = jnp.exp(log_g[:, None] * (idx + 1.0))[:, :, None]
    k_decay = jnp.exp(log_g[:, None] * (c - 1.0 - idx))[:, None, :]
    chunk_decay = jnp.broadcast_to(jnp.exp(log_g * c)[:, None, None], (heads, 1, LANES))
    return decay_mask, q_decay, k_decay, chunk_decay


def kernel(x, ffn1_pre_g, ffn1_w_gate, ffn1_w_up, ffn1_w_down, ffn1_post_g, mix_pre_g, w_in, conv_dw_w, conv_dw_b, conv_ln_g, conv_ln_b, ret_norm_g, w_br_conv, w_br_att, w_br_ret, gate_b, w_out, mix_post_g, ffn2_pre_g, ffn2_w_gate, ffn2_w_up, ffn2_w_down, ffn2_post_g):
    batch, seq, d = x.shape
    depth = w_in.shape[0]
    conv_ch = conv_dw_w.shape[2]
    att_w = w_br_att.shape[1]
    ret_v_w = w_br_ret.shape[1]
    att_heads = att_w // ATT_HEAD_DIM
    ret_dv = ret_v_w // RET_HEADS
    ret_dk = ret_dv // 2
    ret_qk_w = RET_HEADS * ret_dk
    sizes = [conv_ch, conv_ch, att_w, att_w, att_w, ret_qk_w, ret_qk_w, ret_v_w, ret_v_w, N_BRANCH * d]
    assert sum(sizes) == w_in.shape[2]
    cols = [0]
    for s in sizes[:-1]:
        cols.append(cols[-1] + s)
    (_, _, qa_col, ka_col, va_col, qr_col, kr_col, vr_col, gr_col, gates_col) = cols

    att_cos, att_sin = _rope_tables(seq, ATT_ROPE_DIMS, ATT_ROPE_THETA, ATT_HEAD_DIM)
    ret_cos, ret_sin = _rope_tables(seq, ret_dk, RET_ROT_THETA, ret_dk)
    dm, qd, kd, cd = _retention_constants(RET_HEADS)

    bf = lambda w: w.astype(BF16)
    row = lambda v: v.reshape(1, -1)
    merge_tn = 512

    xf = x.reshape(batch * seq, d)
    for l in range(depth):
        xf = _ffn(xf, row(ffn1_pre_g[l]), bf(ffn1_w_gate[l]), bf(ffn1_w_up[l]), bf(ffn1_w_down[l]),
                  row(ffn1_post_g[l]))
        proj = _inproj(xf, row(mix_pre_g[l]), bf(w_in[l]))
        w3 = conv_dw_w[l].reshape(CONV_WIDTH, conv_ch // LANES, LANES).transpose(1, 0, 2)
        y_conv = _conv(proj, w3, row(conv_dw_b[l]), row(conv_ln_g[l]), row(conv_ln_b[l]),
                       batch=batch, seq=seq, ch=conv_ch)
        y_att = _moba(proj, att_cos, att_sin, batch=batch, seq=seq, heads=att_heads,
                      q_col=qa_col, k_col=ka_col, v_col=va_col)
        y_ret = _retention(proj, ret_cos, ret_sin, dm, qd, kd, cd, row(ret_norm_g[l]),
                           batch=batch, seq=seq, heads=RET_HEADS, dk=ret_dk, dv=ret_dv,
                           q_col=qr_col, k_col=kr_col, v_col=vr_col, g_col=gr_col)
        gate_b3 = gate_b[l].reshape(N_BRANCH, d // merge_tn, merge_tn).transpose(1, 0, 2)
        xf = _merge(xf, y_conv, y_att, y_ret, bf(w_br_conv[l]), bf(w_br_att[l]), bf(w_br_ret[l]),
                    proj, gate_b3, bf(w_out[l]), row(mix_post_g[l]), gates_col=gates_col, tn=merge_tn)
        xf = _ffn(xf, row(ffn2_pre_g[l]), bf(ffn2_w_gate[l]), bf(ffn2_w_up[l]), bf(ffn2_w_down[l]),
                  row(ffn2_post_g[l]))
    return xf.reshape(batch, seq, d)
```

```python
import functools
import math

import jax
import jax.numpy as jnp
from jax import lax
from jax.experimental import pallas as pl
from jax.experimental.pallas import tpu as pltpu

F32 = jnp.float32
BF16 = jnp.bfloat16

EPS = 1e-6
LANES = 128
V7X_VMEM_LIMIT_BYTES = 56 * 1024 * 1024
MASK_VALUE = -1e30

CONV_WIDTH = 31
CONV_HALO = 32
ATT_HEAD_DIM = 128
ATT_ROPE_DIMS = ATT_HEAD_DIM // 4
ATT_ROPE_THETA = 500000.0
MOBA_BLOCK = 256
MOBA_TOPK = 3
RET_HEADS = 8
RET_CHUNK = 128
RET_ROT_THETA = 10000.0
N_BRANCH = 3


def _params(*semantics):
    return pltpu.CompilerParams(dimension_semantics=semantics,
                                vmem_limit_bytes=V7X_VMEM_LIMIT_BYTES)


def _layer_vec(layer, width):
    return pl.BlockSpec((None, 1, width), lambda *_: (layer, 0, 0))


def _rms(x, g):
    return x * lax.rsqrt(jnp.mean(x * x, axis=-1, keepdims=True) + EPS) * g


def _silu(x):
    return x * jax.nn.sigmoid(x)


def _dot(a, b):
    return jnp.dot(a, b, preferred_element_type=F32)


def _dot_nt(a, b, precision=None):
    return lax.dot_general(a, b, (((1,), (1,)), ((), ())), precision=precision,
                           preferred_element_type=F32)


def _rope(x, cos, sin_signed, half, group):
    n = x.shape[-1]
    lane = lax.broadcasted_iota(jnp.int32, x.shape, x.ndim - 1)
    first = (lane % group) < half
    partner = jnp.where(first, pltpu.roll(x, n - half, x.ndim - 1), pltpu.roll(x, half, x.ndim - 1))
    return x * cos + partner * sin_signed


def _ffn_body(x_ref, pre_ref, wg_ref, wu_ref, wd_ref, post_ref, o_ref, h_ref, acc_ref):
    j = pl.program_id(1)

    @pl.when(j == 0)
    def _():
        h_ref[...] = _rms(x_ref[...], pre_ref[...]).astype(BF16)
        acc_ref[...] = jnp.zeros_like(acc_ref)

    h = h_ref[...]
    g = _dot(h, wg_ref[...])
    u = _dot(h, wu_ref[...])
    a = (_silu(g) * u).astype(BF16)
    acc_ref[...] += _dot(a, wd_ref[...])

    @pl.when(j == pl.num_programs(1) - 1)
    def _():
        o_ref[...] = x_ref[...] + 0.5 * _rms(acc_ref[...], post_ref[...])


def _ffn(x, pre_g, wg, wu, wd, post_g, *, layer, tm=512, tf=512):
    m, d = x.shape
    f = wg.shape[2]
    return pl.pallas_call(
        _ffn_body,
        grid=(m // tm, f // tf),
        in_specs=[
            pl.BlockSpec((tm, d), lambda i, j: (i, 0)),
            _layer_vec(layer, d),
            pl.BlockSpec((None, d, tf), lambda i, j: (layer, 0, j)),
            pl.BlockSpec((None, d, tf), lambda i, j: (layer, 0, j)),
            pl.BlockSpec((None, tf, d), lambda i, j: (layer, j, 0)),
            _layer_vec(layer, d),
        ],
        out_specs=pl.BlockSpec((tm, d), lambda i, j: (i, 0)),
        out_shape=jax.ShapeDtypeStruct((m, d), F32),
        scratch_shapes=[pltpu.VMEM((tm, d), BF16), pltpu.VMEM((tm, d), F32)],
        compiler_params=_params("parallel", "arbitrary"),
        name="ffn",
    )(x, pre_g, wg, wu, wd, post_g)


def _inproj_body(x_ref, g_ref, w_ref, o_ref, h_ref):
    @pl.when(pl.program_id(1) == 0)
    def _():
        h_ref[...] = _rms(x_ref[...], g_ref[...]).astype(BF16)

    o_ref[...] = _dot(h_ref[...], w_ref[...].astype(BF16))


def _inproj(x, g, w, *, layer, tm=1024, tn=1024):
    m, d = x.shape
    n = w.shape[2]
    return pl.pallas_call(
        _inproj_body,
        grid=(m // tm, n // tn),
        in_specs=[
            pl.BlockSpec((tm, d), lambda i, j: (i, 0)),
            _layer_vec(layer, d),
            pl.BlockSpec((None, d, tn), lambda i, j: (layer, 0, j)),
        ],
        out_specs=pl.BlockSpec((tm, tn), lambda i, j: (i, j)),
        out_shape=jax.ShapeDtypeStruct((m, n), F32),
        scratch_shapes=[pltpu.VMEM((tm, d), BF16)],
        compiler_params=_params("parallel", "arbitrary"),
        name="inproj",
    )(x, g, w)


def _conv_body(a_ref, g_ref, ap_ref, gp_ref, w_ref, b_ref, lng_ref, lnb_ref, o_ref,
               u_ref, y_ref, *, ts, rows_per_acc):
    nck = u_ref.shape[0]
    u = a_ref[...] * jax.nn.sigmoid(g_ref[...])
    up = ap_ref[...] * jax.nn.sigmoid(gp_ref[...])
    up = jnp.where(pl.program_id(1) > 0, up, 0.0)
    for c in range(nck):
        u_ref[c, 0:CONV_HALO, :] = up[:, c * LANES:(c + 1) * LANES]
        u_ref[c, CONV_HALO:CONV_HALO + ts, :] = u[:, c * LANES:(c + 1) * LANES]

    first = CONV_HALO - (CONV_WIDTH - 1)

    def chunk(c, carry):
        for r in range(ts // rows_per_acc):
            r0 = r * rows_per_acc
            acc = jnp.zeros((rows_per_acc, LANES), F32)
            for w in range(CONV_WIDTH):
                acc = acc + u_ref[c, r0 + first + w:r0 + first + w + rows_per_acc, :] * w_ref[c, w:w + 1, :]
            y_ref[c, r0:r0 + rows_per_acc, :] = acc
        return carry

    lax.fori_loop(0, nck, chunk, 0)

    y = jnp.concatenate([y_ref[c] for c in range(nck)], axis=-1) + b_ref[...]
    mu = jnp.mean(y, axis=-1, keepdims=True)
    yc = y - mu
    yn = yc * lax.rsqrt(jnp.mean(yc * yc, axis=-1, keepdims=True) + EPS) * lng_ref[...] + lnb_ref[...]
    o_ref[...] = _silu(yn).astype(BF16)


def _conv(proj, w3, b, ln_g, ln_b, *, layer, batch, seq, ch, ts=256, rows_per_acc=64):
    m = proj.shape[0]
    nck = ch // LANES
    tiles = seq // ts
    halo_per_tile = ts // CONV_HALO

    def cur(col):
        return pl.BlockSpec((ts, ch), lambda bi, i: (bi * tiles + i, col))

    def prev(col):
        return pl.BlockSpec(
            (CONV_HALO, ch),
            lambda bi, i: (jnp.maximum((bi * tiles + i) * halo_per_tile - 1, 0), col))

    vec = _layer_vec(layer, ch)
    return pl.pallas_call(
        functools.partial(_conv_body, ts=ts, rows_per_acc=rows_per_acc),
        grid=(batch, tiles),
        in_specs=[cur(0), cur(1), prev(0), prev(1),
                  pl.BlockSpec((None, nck, CONV_WIDTH, LANES), lambda bi, i: (layer, 0, 0, 0)),
                  vec, vec, vec],
        out_specs=pl.BlockSpec((ts, ch), lambda bi, i: (bi * tiles + i, 0)),
        out_shape=jax.ShapeDtypeStruct((m, ch), BF16),
        scratch_shapes=[pltpu.VMEM((nck, CONV_HALO + ts, LANES), F32),
                        pltpu.VMEM((nck, ts, LANES), F32)],
        compiler_params=_params("parallel", "parallel"),
        name="conv",
    )(proj, proj, proj, proj, w3, b, ln_g, ln_b)


def _moba_body(q_ref, k_ref, v_ref, cos_ref, sin_ref, o_ref, kb_ref, vt_ref, km_ref, *, nb, n_sel, scale):
    blk = MOBA_BLOCK
    half = ATT_ROPE_DIMS // 2

    for j in range(nb):
        rows = slice(j * blk, (j + 1) * blk)
        kj = _rope(k_ref[rows, :], cos_ref[rows, :], sin_ref[rows, :], half, ATT_HEAD_DIM)
        kb_ref[rows, :] = kj.astype(BF16)
        km_ref[j:j + 1, :] = jnp.mean(kj, axis=0, keepdims=True)
        vt_ref[:, rows] = v_ref[rows, :].T.astype(BF16)

    kidx = lax.broadcasted_iota(jnp.int32, (blk, blk), 0)
    qidx = lax.broadcasted_iota(jnp.int32, (blk, blk), 1)
    causal = kidx <= qidx

    for i in range(nb):
        rows = slice(i * blk, (i + 1) * blk)
        nk = (i + 1) * blk
        q = _rope(q_ref[rows, :], cos_ref[rows, :], sin_ref[rows, :], half, ATT_HEAD_DIM)
        s = _dot_nt(kb_ref[0:nk, :], q.astype(BF16)) * scale
        parts = []
        if i > n_sel:
            gate = _dot_nt(km_ref[...], q, precision=lax.Precision.HIGHEST)
            brow = lax.broadcasted_iota(jnp.int32, (nb, blk), 0)
            rank = jnp.zeros((nb, blk), jnp.int32)
            for jp in range(i):
                gj = gate[jp:jp + 1, :]
                rank = rank + jnp.where((gj > gate) | ((gj == gate) & (jp < brow)), 1, 0)
            bias = jnp.where(rank < n_sel, 0.0, MASK_VALUE)
            for j in range(i):
                parts.append(s[j * blk:(j + 1) * blk, :] + bias[j:j + 1, :])
        elif i > 0:
            parts.append(s[0:i * blk, :])
        parts.append(jnp.where(causal, s[i * blk:nk, :], MASK_VALUE))
        s = jnp.concatenate(parts, axis=0) if len(parts) > 1 else parts[0]
        m = jnp.max(s, axis=0, keepdims=True)
        p = jnp.exp(s - m)
        l = jnp.sum(p, axis=0, keepdims=True)
        acc = _dot(vt_ref[:, 0:nk], p.astype(BF16))
        o_ref[rows, :] = (acc / l).T.astype(BF16)


def _moba(proj, cos_t, sin_t, *, batch, seq, heads, q_col, k_col, v_col):
    m = proj.shape[0]
    hd = ATT_HEAD_DIM
    nb = seq // MOBA_BLOCK
    n_sel = min(MOBA_TOPK, nb - 1)
    table = pl.BlockSpec((seq, hd), lambda b, h: (0, 0))

    def head(col):
        return pl.BlockSpec((seq, hd), lambda b, h: (b, col // hd + h))

    return pl.pallas_call(
        functools.partial(_moba_body, nb=nb, n_sel=n_sel, scale=hd ** -0.5),
        grid=(batch, heads),
        in_specs=[head(q_col), head(k_col), head(v_col), table, table],
        out_specs=pl.BlockSpec((seq, hd), lambda b, h: (b, h)),
        out_shape=jax.ShapeDtypeStruct((m, heads * hd), BF16),
        scratch_shapes=[
            pltpu.VMEM((seq, hd), BF16),
            pltpu.VMEM((hd, seq), BF16),
            pltpu.VMEM((nb, hd), F32),
        ],
        compiler_params=_params("parallel", "parallel"),
        name="moba",
    )(proj, proj, proj, cos_t, sin_t)


def _ret_body(q_ref, k_ref, v_ref, gr_ref, cos_ref, sin_ref, dm_ref, qd_ref, kd_ref, cd_ref,
              ng_ref, o_ref, st_ref, *, heads, dk, dv, chunks):
    c = RET_CHUNK
    per_group = LANES // dk

    @pl.when(pl.program_id(1) == 0)
    def _():
        st_ref[...] = jnp.zeros_like(st_ref)

    for cc in range(chunks):
        rows = slice(cc * c, (cc + 1) * c)
        cos = cos_ref[rows, :]
        sin = sin_ref[rows, :]
        for hg in range(heads // per_group):
            lanes = slice(hg * LANES, (hg + 1) * LANES)
            qr = _rope(q_ref[rows, lanes], cos, sin, dk // 2, dk)
            kr = _rope(k_ref[rows, lanes], cos, sin, dk // 2, dk) * (dk ** -0.5)
            krt = kr.T
            for hh in range(per_group):
                h = hg * per_group + hh
                hl = slice(hh * dk, (hh + 1) * dk)
                vl = slice(h * dv, (h + 1) * dv)
                qh = qr[:, hl]
                vb = v_ref[rows, vl].astype(BF16)
                inner = _dot_nt(qh.astype(BF16), kr[:, hl].astype(BF16)) * dm_ref[h]
                ro = _dot(inner.astype(BF16), vb)
                st = st_ref[h]
                ro = ro + _dot((qh * qd_ref[h]).astype(BF16), st.astype(BF16))
                kv = _dot((krt[hl, :] * kd_ref[h]).astype(BF16), vb)
                st_ref[h] = st * cd_ref[h] + kv
                ro = ro * lax.rsqrt(jnp.mean(ro * ro, axis=-1, keepdims=True) + EPS) * ng_ref[:, vl]
                o_ref[rows, vl] = (_silu(gr_ref[rows, vl]) * ro).astype(BF16)


def _retention(proj, cos_t, sin_t, dm, qd, kd, cd, norm_g, *, layer, batch, seq, heads, dk, dv,
               q_col, k_col, v_col, g_col, chunks=4):
    m = proj.shape[0]
    ts = chunks * RET_CHUNK
    tiles = seq // ts
    qk_w = heads * dk
    v_w = heads * dv

    def tok(width, col):
        return pl.BlockSpec((ts, width), lambda b, t: (b * tiles + t, col // width))

    def const(shape):
        return pl.BlockSpec(shape, lambda b, t: (0,) * len(shape))

    return pl.pallas_call(
        functools.partial(_ret_body, heads=heads, dk=dk, dv=dv, chunks=chunks),
        grid=(batch, tiles),
        in_specs=[
            tok(qk_w, q_col), tok(qk_w, k_col), tok(v_w, v_col), tok(v_w, g_col),
            pl.BlockSpec((ts, LANES), lambda b, t: (t, 0)),
            pl.BlockSpec((ts, LANES), lambda b, t: (t, 0)),
            const(dm.shape), const(qd.shape), const(kd.shape), const(cd.shape),
            _layer_vec(layer, v_w),
        ],
        out_specs=pl.BlockSpec((ts, v_w), lambda b, t: (b * tiles + t, 0)),
        out_shape=jax.ShapeDtypeStruct((m, v_w), BF16),
        scratch_shapes=[pltpu.VMEM((heads, dk, dv), F32)],
        compiler_params=_params("parallel", "arbitrary"),
        name="retention",
    )(proj, proj, proj, proj, cos_t, sin_t, dm, qd, kd, cd, norm_g)


def _merge_body(x_ref, yc_ref, ya_ref, yr_ref, wc_ref, wa_ref, wr_ref, g0_ref, g1_ref, g2_ref,
                gb_ref, wo_ref, post_ref, o_ref, mg_ref, *, tn):
    d = o_ref.shape[1]
    for t in range(d // tn):
        cols = slice(t * tn, (t + 1) * tn)
        merged = (jax.nn.sigmoid(g0_ref[:, cols] + gb_ref[0:1, cols]) * _dot(yc_ref[...], wc_ref[:, cols])
                  + jax.nn.sigmoid(g1_ref[:, cols] + gb_ref[1:2, cols]) * _dot(ya_ref[...], wa_ref[:, cols])
                  + jax.nn.sigmoid(g2_ref[:, cols] + gb_ref[2:3, cols]) * _dot(yr_ref[...], wr_ref[:, cols]))
        mg_ref[:, cols] = merged.astype(BF16)
    o_ref[...] = x_ref[...] + _rms(_dot(mg_ref[...], wo_ref[...]), post_ref[...])


def _merge(x, yc, ya, yr, wc, wa, wr, proj, gate_b, wo, post_g, *, layer, gates_col, tm=256, tn=512):
    m, d = x.shape

    def ytile(a):
        return pl.BlockSpec((tm, a.shape[1]), lambda i: (i, 0))

    def resident(w):
        return pl.BlockSpec((None,) + w.shape[1:], lambda i: (layer, 0, 0),
                            pipeline_mode=pl.Buffered(1))

    def gtile(br):
        return pl.BlockSpec((tm, d), lambda i: (i, gates_col // d + br))

    return pl.pallas_call(
        functools.partial(_merge_body, tn=tn),
        grid=(m // tm,),
        in_specs=[
            pl.BlockSpec((tm, d), lambda i: (i, 0)),
            ytile(yc), ytile(ya), ytile(yr), resident(wc), resident(wa), resident(wr),
            gtile(0), gtile(1), gtile(2),
            pl.BlockSpec((None, N_BRANCH, d), lambda i: (layer, 0, 0)),
            resident(wo),
            _layer_vec(layer, d),
        ],
        out_specs=pl.BlockSpec((tm, d), lambda i: (i, 0)),
        out_shape=jax.ShapeDtypeStruct((m, d), F32),
        scratch_shapes=[pltpu.VMEM((tm, d), BF16)],
        compiler_params=_params("parallel"),
        name="merge",
    )(x, yc, ya, yr, wc, wa, wr, proj, proj, proj, gate_b, wo, post_g)


def _rope_tables(seq, n_rot, theta, group):
    half = n_rot // 2
    inv = 1.0 / (theta ** (jnp.arange(half, dtype=F32) / half))
    ang = jnp.arange(seq, dtype=jnp.int32).astype(F32)[:, None] * inv[None, :]
    cos, sin = jnp.cos(ang), jnp.sin(ang)
    rest = group - n_rot
    cos_g = jnp.concatenate([cos, cos, jnp.ones((seq, rest), F32)], axis=-1)
    sin_g = jnp.concatenate([-sin, sin, jnp.zeros((seq, rest), F32)], axis=-1)
    reps = LANES // group
    return jnp.tile(cos_g, (1, reps)), jnp.tile(sin_g, (1, reps))


def _retention_constants(heads):
    c = RET_CHUNK
    log_g = jnp.log1p(-(2.0 ** (-5.0 - jnp.arange(heads, dtype=F32))))
    idx = jnp.arange(c, dtype=F32)
    diff = idx[:, None] - idx[None, :]
    decay_mask = jnp.exp(jnp.where(diff >= 0, log_g[:, None, None] * diff, -jnp.inf))
    q_decay = jnp.exp(log_g[:, None] * (idx + 1.0))[:, :, None]
    k_decay = jnp.exp(log_g[:, None] * (c - 1.0 - idx))[:, None, :]
    chunk_decay = jnp.broadcast_to(jnp.exp(log_g * c)[:, None, None], (heads, 1, LANES))
    return decay_mask, q_decay, k_decay, chunk_decay


def kernel(x, ffn1_pre_g, ffn1_w_gate, ffn1_w_up, ffn1_w_down, ffn1_post_g, mix_pre_g, w_in, conv_dw_w, conv_dw_b, conv_ln_g, conv_ln_b, ret_norm_g, w_br_conv, w_br_att, w_br_ret, gate_b, w_out, mix_post_g, ffn2_pre_g, ffn2_w_gate, ffn2_w_up, ffn2_w_down, ffn2_post_g):
    batch, seq, d = x.shape
    depth = w_in.shape[0]
    conv_ch = conv_dw_w.shape[2]
    att_w = w_br_att.shape[1]
    ret_v_w = w_br_ret.shape[1]
    att_heads = att_w // ATT_HEAD_DIM
    ret_dv = ret_v_w // RET_HEADS
    ret_dk = ret_dv // 2
    ret_qk_w = RET_HEADS * ret_dk
    sizes = [conv_ch, conv_ch, att_w, att_w, att_w, ret_qk_w, ret_qk_w, ret_v_w, ret_v_w, N_BRANCH * d]
    assert sum(sizes) == w_in.shape[2]
    cols = [0]
    for s in sizes[:-1]:
        cols.append(cols[-1] + s)
    (_, _, qa_col, ka_col, va_col, qr_col, kr_col, vr_col, gr_col, gates_col) = cols

    att_cos, att_sin = _rope_tables(seq, ATT_ROPE_DIMS, ATT_ROPE_THETA, ATT_HEAD_DIM)
    ret_cos, ret_sin = _rope_tables(seq, ret_dk, RET_ROT_THETA, ret_dk)
    dm, qd, kd, cd = _retention_constants(RET_HEADS)

    bf = lambda w: w.astype(BF16)
    rows = lambda v: v.reshape(depth, 1, -1)
    ffn1 = (rows(ffn1_pre_g), bf(ffn1_w_gate), bf(ffn1_w_up), bf(ffn1_w_down), rows(ffn1_post_g))
    ffn2 = (rows(ffn2_pre_g), bf(ffn2_w_gate), bf(ffn2_w_up), bf(ffn2_w_down), rows(ffn2_post_g))
    conv_w = conv_dw_w.reshape(depth, CONV_WIDTH, conv_ch // LANES, LANES).transpose(0, 2, 1, 3)
    conv_p = (conv_w, rows(conv_dw_b), rows(conv_ln_g), rows(conv_ln_b))
    merge_w = (bf(w_br_conv), bf(w_br_att), bf(w_br_ret))
    gate_b3 = gate_b.reshape(depth, N_BRANCH, d)
    wo = bf(w_out)
    mix_pre, mix_post, ret_g = rows(mix_pre_g), rows(mix_post_g), rows(ret_norm_g)

    xf = x.reshape(batch * seq, d)
    for l in range(depth):
        xf = _ffn(xf, *ffn1, layer=l)
        proj = _inproj(xf, mix_pre, w_in, layer=l)
        y_conv = _conv(proj, *conv_p, layer=l, batch=batch, seq=seq, ch=conv_ch)
        y_att = _moba(proj, att_cos, att_sin, batch=batch, seq=seq, heads=att_heads,
                      q_col=qa_col, k_col=ka_col, v_col=va_col)
        y_ret = _retention(proj, ret_cos, ret_sin, dm, qd, kd, cd, ret_g, layer=l,
                           batch=batch, seq=seq, heads=RET_HEADS, dk=ret_dk, dv=ret_dv,
                           q_col=qr_col, k_col=kr_col, v_col=vr_col, g_col=gr_col)
        xf = _merge(xf, y_conv, y_att, y_ret, *merge_w, proj, gate_b3, wo, mix_post,
                    layer=l, gates_col=gates_col)
        xf = _ffn(xf, *ffn2, layer=l)
    return xf.reshape(batch, seq, d)
```

```python
import functools
import math

import jax
import jax.numpy as jnp
from jax import lax
from jax.experimental import pallas as pl
from jax.experimental.pallas import tpu as pltpu

F32 = jnp.float32
BF16 = jnp.bfloat16

EPS = 1e-6
LANES = 128
V7X_VMEM_LIMIT_BYTES = 56 * 1024 * 1024
MASK_VALUE = -1e30

CONV_WIDTH = 31
CONV_HALO = 32
ATT_HEAD_DIM = 128
ATT_ROPE_DIMS = ATT_HEAD_DIM // 4
ATT_ROPE_THETA = 500000.0
MOBA_BLOCK = 256
MOBA_TOPK = 3
RET_HEADS = 8
RET_CHUNK = 128
RET_ROT_THETA = 10000.0
N_BRANCH = 3


def _params(*semantics):
    return pltpu.CompilerParams(dimension_semantics=semantics,
                                vmem_limit_bytes=V7X_VMEM_LIMIT_BYTES)


def _layer_vec(layer, width):
    return pl.BlockSpec((None, 1, width), lambda *_: (layer, 0, 0))


def _rms(x, g):
    return x * lax.rsqrt(jnp.mean(x * x, axis=-1, keepdims=True) + EPS) * g


def _silu(x):
    return x * jax.nn.sigmoid(x)


def _dot(a, b):
    return jnp.dot(a, b, preferred_element_type=F32)


def _dot_nt(a, b, precision=None):
    return lax.dot_general(a, b, (((1,), (1,)), ((), ())), precision=precision,
                           preferred_element_type=F32)


def _rope(x, cos, sin_signed, half, group):
    n = x.shape[-1]
    lane = lax.broadcasted_iota(jnp.int32, x.shape, x.ndim - 1)
    first = (lane % group) < half
    partner = jnp.where(first, pltpu.roll(x, n - half, x.ndim - 1), pltpu.roll(x, half, x.ndim - 1))
    return x * cos + partner * sin_signed


def _ffn_body(x_ref, pre_ref, wg_ref, wu_ref, wd_ref, post_ref, o_ref, h_ref, acc_ref):
    j = pl.program_id(1)

    @pl.when(j == 0)
    def _():
        h_ref[...] = _rms(x_ref[...], pre_ref[...]).astype(BF16)
        acc_ref[...] = jnp.zeros_like(acc_ref)

    h = h_ref[...]
    g = _dot(h, wg_ref[...])
    u = _dot(h, wu_ref[...])
    a = (_silu(g) * u).astype(BF16)
    acc_ref[...] += _dot(a, wd_ref[...])

    @pl.when(j == pl.num_programs(1) - 1)
    def _():
        o_ref[...] = x_ref[...] + 0.5 * _rms(acc_ref[...], post_ref[...])


def _ffn(x, pre_g, wg, wu, wd, post_g, *, layer, tm=512, tf=512):
    m, d = x.shape
    f = wg.shape[2]
    return pl.pallas_call(
        _ffn_body,
        grid=(m // tm, f // tf),
        in_specs=[
            pl.BlockSpec((tm, d), lambda i, j: (i, 0)),
            _layer_vec(layer, d),
            pl.BlockSpec((None, d, tf), lambda i, j: (layer, 0, j)),
            pl.BlockSpec((None, d, tf), lambda i, j: (layer, 0, j)),
            pl.BlockSpec((None, tf, d), lambda i, j: (layer, j, 0)),
            _layer_vec(layer, d),
        ],
        out_specs=pl.BlockSpec((tm, d), lambda i, j: (i, 0)),
        out_shape=jax.ShapeDtypeStruct((m, d), F32),
        scratch_shapes=[pltpu.VMEM((tm, d), BF16), pltpu.VMEM((tm, d), F32)],
        compiler_params=_params("parallel", "arbitrary"),
        name="ffn",
    )(x, pre_g, wg, wu, wd, post_g)


def _inproj_body(x_ref, g_ref, w_ref, o_ref, h_ref):
    @pl.when(pl.program_id(1) == 0)
    def _():
        h_ref[...] = _rms(x_ref[...], g_ref[...]).astype(BF16)

    o_ref[...] = _dot(h_ref[...], w_ref[...].astype(BF16))


def _inproj(x, g, w, *, layer, tm=2048, tn=512):
    m, d = x.shape
    n = w.shape[2]
    return pl.pallas_call(
        _inproj_body,
        grid=(m // tm, n // tn),
        in_specs=[
            pl.BlockSpec((tm, d), lambda i, j: (i, 0), pipeline_mode=pl.Buffered(1)),
            _layer_vec(layer, d),
            pl.BlockSpec((None, d, tn), lambda i, j: (layer, 0, j)),
        ],
        out_specs=pl.BlockSpec((tm, tn), lambda i, j: (i, j)),
        out_shape=jax.ShapeDtypeStruct((m, n), F32),
        scratch_shapes=[pltpu.VMEM((tm, d), BF16)],
        compiler_params=_params("parallel", "arbitrary"),
        name="inproj",
    )(x, g, w)


def _conv_body(a_ref, g_ref, ap_ref, gp_ref, w_ref, b_ref, lng_ref, lnb_ref, o_ref,
               u_ref, y_ref, *, ts, rows_per_acc):
    nck = u_ref.shape[0]
    u = a_ref[...] * jax.nn.sigmoid(g_ref[...])
    up = ap_ref[...] * jax.nn.sigmoid(gp_ref[...])
    up = jnp.where(pl.program_id(1) > 0, up, 0.0)
    for c in range(nck):
        u_ref[c, 0:CONV_HALO, :] = up[:, c * LANES:(c + 1) * LANES]
        u_ref[c, CONV_HALO:CONV_HALO + ts, :] = u[:, c * LANES:(c + 1) * LANES]

    first = CONV_HALO - (CONV_WIDTH - 1)

    def chunk(c, carry):
        for r in range(ts // rows_per_acc):
            r0 = r * rows_per_acc
            acc = jnp.zeros((rows_per_acc, LANES), F32)
            for w in range(CONV_WIDTH):
                acc = acc + u_ref[c, r0 + first + w:r0 + first + w + rows_per_acc, :] * w_ref[c, w:w + 1, :]
            y_ref[c, r0:r0 + rows_per_acc, :] = acc
        return carry

    lax.fori_loop(0, nck, chunk, 0)

    y = jnp.concatenate([y_ref[c] for c in range(nck)], axis=-1) + b_ref[...]
    mu = jnp.mean(y, axis=-1, keepdims=True)
    yc = y - mu
    yn = yc * lax.rsqrt(jnp.mean(yc * yc, axis=-1, keepdims=True) + EPS) * lng_ref[...] + lnb_ref[...]
    o_ref[...] = _silu(yn).astype(BF16)


def _conv(proj, w3, b, ln_g, ln_b, *, layer, batch, seq, ch, ts=256, rows_per_acc=64):
    m = proj.shape[0]
    nck = ch // LANES
    tiles = seq // ts
    halo_per_tile = ts // CONV_HALO

    def cur(col):
        return pl.BlockSpec((ts, ch), lambda bi, i: (bi * tiles + i, col))

    def prev(col):
        return pl.BlockSpec(
            (CONV_HALO, ch),
            lambda bi, i: (jnp.maximum((bi * tiles + i) * halo_per_tile - 1, 0), col))

    vec = _layer_vec(layer, ch)
    return pl.pallas_call(
        functools.partial(_conv_body, ts=ts, rows_per_acc=rows_per_acc),
        grid=(batch, tiles),
        in_specs=[cur(0), cur(1), prev(0), prev(1),
                  pl.BlockSpec((None, nck, CONV_WIDTH, LANES), lambda bi, i: (layer, 0, 0, 0)),
                  vec, vec, vec],
        out_specs=pl.BlockSpec((ts, ch), lambda bi, i: (bi * tiles + i, 0)),
        out_shape=jax.ShapeDtypeStruct((m, ch), BF16),
        scratch_shapes=[pltpu.VMEM((nck, CONV_HALO + ts, LANES), F32),
                        pltpu.VMEM((nck, ts, LANES), F32)],
        compiler_params=_params("parallel", "parallel"),
        name="conv",
    )(proj, proj, proj, proj, w3, b, ln_g, ln_b)


def _moba_body(q_ref, k_ref, v_ref, cos_ref, sin_ref, o_ref, kb_ref, vt_ref, km_ref, *, nb, n_sel, scale):
    blk = MOBA_BLOCK
    half = ATT_ROPE_DIMS // 2

    for j in range(nb):
        rows = slice(j * blk, (j + 1) * blk)
        kj = _rope(k_ref[rows, :], cos_ref[rows, :], sin_ref[rows, :], half, ATT_HEAD_DIM)
        kb_ref[rows, :] = kj.astype(BF16)
        km_ref[j:j + 1, :] = jnp.mean(kj, axis=0, keepdims=True)
        vt_ref[:, rows] = v_ref[rows, :].T.astype(BF16)

    kidx = lax.broadcasted_iota(jnp.int32, (blk, blk), 0)
    qidx = lax.broadcasted_iota(jnp.int32, (blk, blk), 1)
    causal = kidx <= qidx

    for i in range(nb):
        rows = slice(i * blk, (i + 1) * blk)
        nk = (i + 1) * blk
        q = _rope(q_ref[rows, :], cos_ref[rows, :], sin_ref[rows, :], half, ATT_HEAD_DIM)
        s = _dot_nt(kb_ref[0:nk, :], q.astype(BF16)) * scale
        parts = []
        if i > n_sel:
            gate = _dot_nt(km_ref[...], q, precision=lax.Precision.HIGHEST)
            brow = lax.broadcasted_iota(jnp.int32, (nb, blk), 0)
            rank = jnp.zeros((nb, blk), jnp.int32)
            for jp in range(i):
                gj = gate[jp:jp + 1, :]
                rank = rank + jnp.where((gj > gate) | ((gj == gate) & (jp < brow)), 1, 0)
            bias = jnp.where(rank < n_sel, 0.0, MASK_VALUE)
            for j in range(i):
                parts.append(s[j * blk:(j + 1) * blk, :] + bias[j:j + 1, :])
        elif i > 0:
            parts.append(s[0:i * blk, :])
        parts.append(jnp.where(causal, s[i * blk:nk, :], MASK_VALUE))
        s = jnp.concatenate(parts, axis=0) if len(parts) > 1 else parts[0]
        m = jnp.max(s, axis=0, keepdims=True)
        p = jnp.exp(s - m)
        l = jnp.sum(p, axis=0, keepdims=True)
        acc = _dot(vt_ref[:, 0:nk], p.astype(BF16))
        o_ref[rows, :] = (acc / l).T.astype(BF16)


def _moba(proj, cos_t, sin_t, *, batch, seq, heads, q_col, k_col, v_col):
    m = proj.shape[0]
    hd = ATT_HEAD_DIM
    nb = seq // MOBA_BLOCK
    n_sel = min(MOBA_TOPK, nb - 1)
    table = pl.BlockSpec((seq, hd), lambda b, h: (0, 0))

    def head(col):
        return pl.BlockSpec((seq, hd), lambda b, h: (b, col // hd + h))

    return pl.pallas_call(
        functools.partial(_moba_body, nb=nb, n_sel=n_sel, scale=hd ** -0.5),
        grid=(batch, heads),
        in_specs=[head(q_col), head(k_col), head(v_col), table, table],
        out_specs=pl.BlockSpec((seq, hd), lambda b, h: (b, h)),
        out_shape=jax.ShapeDtypeStruct((m, heads * hd), BF16),
        scratch_shapes=[
            pltpu.VMEM((seq, hd), BF16),
            pltpu.VMEM((hd, seq), BF16),
            pltpu.VMEM((nb, hd), F32),
        ],
        compiler_params=_params("parallel", "parallel"),
        name="moba",
    )(proj, proj, proj, cos_t, sin_t)


def _ret_body(q_ref, k_ref, v_ref, gr_ref, cos_ref, sin_ref, dm_ref, qd_ref, kd_ref, cd_ref,
              ng_ref, o_ref, st_ref, *, heads, dk, dv, chunks):
    c = RET_CHUNK
    per_group = LANES // dk

    @pl.when(pl.program_id(1) == 0)
    def _():
        st_ref[...] = jnp.zeros_like(st_ref)

    for cc in range(chunks):
        rows = slice(cc * c, (cc + 1) * c)
        cos = cos_ref[rows, :]
        sin = sin_ref[rows, :]
        for hg in range(heads // per_group):
            lanes = slice(hg * LANES, (hg + 1) * LANES)
            qr = _rope(q_ref[rows, lanes], cos, sin, dk // 2, dk)
            kr = _rope(k_ref[rows, lanes], cos, sin, dk // 2, dk) * (dk ** -0.5)
            krt = kr.T
            for hh in range(per_group):
                h = hg * per_group + hh
                hl = slice(hh * dk, (hh + 1) * dk)
                vl = slice(h * dv, (h + 1) * dv)
                qh = qr[:, hl]
                vb = v_ref[rows, vl].astype(BF16)
                inner = _dot_nt(qh.astype(BF16), kr[:, hl].astype(BF16)) * dm_ref[h]
                ro = _dot(inner.astype(BF16), vb)
                st = st_ref[h]
                ro = ro + _dot((qh * qd_ref[h]).astype(BF16), st.astype(BF16))
                kv = _dot((krt[hl, :] * kd_ref[h]).astype(BF16), vb)
                st_ref[h] = st * cd_ref[h] + kv
                ro = ro * lax.rsqrt(jnp.mean(ro * ro, axis=-1, keepdims=True) + EPS) * ng_ref[:, vl]
                o_ref[rows, vl] = (_silu(gr_ref[rows, vl]) * ro).astype(BF16)


def _retention(proj, cos_t, sin_t, dm, qd, kd, cd, norm_g, *, layer, batch, seq, heads, dk, dv,
               q_col, k_col, v_col, g_col, chunks=4):
    m = proj.shape[0]
    ts = chunks * RET_CHUNK
    tiles = seq // ts
    qk_w = heads * dk
    v_w = heads * dv

    def tok(width, col):
        return pl.BlockSpec((ts, width), lambda b, t: (b * tiles + t, col // width))

    def const(shape):
        return pl.BlockSpec(shape, lambda b, t: (0,) * len(shape))

    return pl.pallas_call(
        functools.partial(_ret_body, heads=heads, dk=dk, dv=dv, chunks=chunks),
        grid=(batch, tiles),
        in_specs=[
            tok(qk_w, q_col), tok(qk_w, k_col), tok(v_w, v_col), tok(v_w, g_col),
            pl.BlockSpec((ts, LANES), lambda b, t: (t, 0)),
            pl.BlockSpec((ts, LANES), lambda b, t: (t, 0)),
            const(dm.shape), const(qd.shape), const(kd.shape), const(cd.shape),
            _layer_vec(layer, v_w),
        ],
        out_specs=pl.BlockSpec((ts, v_w), lambda b, t: (b * tiles + t, 0)),
        out_shape=jax.ShapeDtypeStruct((m, v_w), BF16),
        scratch_shapes=[pltpu.VMEM((heads, dk, dv), F32)],
        compiler_params=_params("parallel", "arbitrary"),
        name="retention",
    )(proj, proj, proj, proj, cos_t, sin_t, dm, qd, kd, cd, norm_g)


def _merge_body(x_ref, yc_ref, ya_ref, yr_ref, wc_ref, wa_ref, wr_ref, g0_ref, g1_ref, g2_ref,
                gb_ref, wo_ref, post_ref, o_ref, mg_ref, *, tn):
    d = o_ref.shape[1]
    for t in range(d // tn):
        cols = slice(t * tn, (t + 1) * tn)
        merged = (jax.nn.sigmoid(g0_ref[:, cols] + gb_ref[0:1, cols]) * _dot(yc_ref[...], wc_ref[:, cols])
                  + jax.nn.sigmoid(g1_ref[:, cols] + gb_ref[1:2, cols]) * _dot(ya_ref[...], wa_ref[:, cols])
                  + jax.nn.sigmoid(g2_ref[:, cols] + gb_ref[2:3, cols]) * _dot(yr_ref[...], wr_ref[:, cols]))
        mg_ref[:, cols] = merged.astype(BF16)
    o_ref[...] = x_ref[...] + _rms(_dot(mg_ref[...], wo_ref[...]), post_ref[...])


def _merge(x, yc, ya, yr, wc, wa, wr, proj, gate_b, wo, post_g, *, layer, gates_col, tm=256, tn=512):
    m, d = x.shape

    def ytile(a):
        return pl.BlockSpec((tm, a.shape[1]), lambda i: (i, 0))

    def resident(w):
        return pl.BlockSpec((None,) + w.shape[1:], lambda i: (layer, 0, 0),
                            pipeline_mode=pl.Buffered(1))

    def gtile(br):
        return pl.BlockSpec((tm, d), lambda i: (i, gates_col // d + br))

    return pl.pallas_call(
        functools.partial(_merge_body, tn=tn),
        grid=(m // tm,),
        in_specs=[
            pl.BlockSpec((tm, d), lambda i: (i, 0)),
            ytile(yc), ytile(ya), ytile(yr), resident(wc), resident(wa), resident(wr),
            gtile(0), gtile(1), gtile(2),
            pl.BlockSpec((None, N_BRANCH, d), lambda i: (layer, 0, 0)),
            resident(wo),
            _layer_vec(layer, d),
        ],
        out_specs=pl.BlockSpec((tm, d), lambda i: (i, 0)),
        out_shape=jax.ShapeDtypeStruct((m, d), F32),
        scratch_shapes=[pltpu.VMEM((tm, d), BF16)],
        compiler_params=_params("parallel"),
        name="merge",
    )(x, yc, ya, yr, wc, wa, wr, proj, proj, proj, gate_b, wo, post_g)


def _rope_tables(seq, n_rot, theta, group):
    half = n_rot // 2
    inv = 1.0 / (theta ** (jnp.arange(half, dtype=F32) / half))
    ang = jnp.arange(seq, dtype=jnp.int32).astype(F32)[:, None] * inv[None, :]
    cos, sin = jnp.cos(ang), jnp.sin(ang)
    rest = group - n_rot
    cos_g = jnp.concatenate([cos, cos, jnp.ones((seq, rest), F32)], axis=-1)
    sin_g = jnp.concatenate([-sin, sin, jnp.zeros((seq, rest), F32)], axis=-1)
    reps = LANES // group
    return jnp.tile(cos_g, (1, reps)), jnp.tile(sin_g, (1, reps))


def _retention_constants(heads):
    c = RET_CHUNK
    log_g = jnp.log1p(-(2.0 ** (-5.0 - jnp.arange(heads, dtype=F32))))
    idx = jnp.arange(c, dtype=F32)
    diff = idx[:, None] - idx[None, :]
    decay_mask = jnp.exp(jnp.where(diff >= 0, log_g[:, None, None] * diff, -jnp.inf))
    q_decay = jnp.exp(log_g[:, None] * (idx + 1.0))[:, :, None]
    k_decay = jnp.exp(log_g[:, None] * (c - 1.0 - idx))[:, None, :]
    chunk_decay = jnp.broadcast_to(jnp.exp(log_g * c)[:, None, None], (heads, 1, LANES))
    return decay_mask, q_decay, k_decay, chunk_decay


def kernel(x, ffn1_pre_g, ffn1_w_gate, ffn1_w_up, ffn1_w_down, ffn1_post_g, mix_pre_g, w_in, conv_dw_w, conv_dw_b, conv_ln_g, conv_ln_b, ret_norm_g, w_br_conv, w_br_att, w_br_ret, gate_b, w_out, mix_post_g, ffn2_pre_g, ffn2_w_gate, ffn2_w_up, ffn2_w_down, ffn2_post_g):
    batch, seq, d = x.shape
    depth = w_in.shape[0]
    conv_ch = conv_dw_w.shape[2]
    att_w = w_br_att.shape[1]
    ret_v_w = w_br_ret.shape[1]
    att_heads = att_w // ATT_HEAD_DIM
    ret_dv = ret_v_w // RET_HEADS
    ret_dk = ret_dv // 2
    ret_qk_w = RET_HEADS * ret_dk
    sizes = [conv_ch, conv_ch, att_w, att_w, att_w, ret_qk_w, ret_qk_w, ret_v_w, ret_v_w, N_BRANCH * d]
    assert sum(sizes) == w_in.shape[2]
    cols = [0]
    for s in sizes[:-1]:
        cols.append(cols[-1] + s)
    (_, _, qa_col, ka_col, va_col, qr_col, kr_col, vr_col, gr_col, gates_col) = cols

    att_cos, att_sin = _rope_tables(seq, ATT_ROPE_DIMS, ATT_ROPE_THETA, ATT_HEAD_DIM)
    ret_cos, ret_sin = _rope_tables(seq, ret_dk, RET_ROT_THETA, ret_dk)
    dm, qd, kd, cd = _retention_constants(RET_HEADS)

    bf = lambda w: w.astype(BF16)
    rows = lambda v: v.reshape(depth, 1, -1)
    ffn1 = (rows(ffn1_pre_g), bf(ffn1_w_gate), bf(ffn1_w_up), bf(ffn1_w_down), rows(ffn1_post_g))
    ffn2 = (rows(ffn2_pre_g), bf(ffn2_w_gate), bf(ffn2_w_up), bf(ffn2_w_down), rows(ffn2_post_g))
    conv_w = conv_dw_w.reshape(depth, CONV_WIDTH, conv_ch // LANES, LANES).transpose(0, 2, 1, 3)
    conv_p = (conv_w, rows(conv_dw_b), rows(conv_ln_g), rows(conv_ln_b))
    merge_w = (bf(w_br_conv), bf(w_br_att), bf(w_br_ret))
    gate_b3 = gate_b.reshape(depth, N_BRANCH, d)
    wo = bf(w_out)
    mix_pre, mix_post, ret_g = rows(mix_pre_g), rows(mix_post_g), rows(ret_norm_g)

    xf = x.reshape(batch * seq, d)
    for l in range(depth):
        xf = _ffn(xf, *ffn1, layer=l)
        proj = _inproj(xf, mix_pre, w_in, layer=l)
        y_conv = _conv(proj, *conv_p, layer=l, batch=batch, seq=seq, ch=conv_ch)
        y_att = _moba(proj, att_cos, att_sin, batch=batch, seq=seq, heads=att_heads,
                      q_col=qa_col, k_col=ka_col, v_col=va_col)
        y_ret = _retention(proj, ret_cos, ret_sin, dm, qd, kd, cd, ret_g, layer=l,
                           batch=batch, seq=seq, heads=RET_HEADS, dk=ret_dk, dv=ret_dv,
                           q_col=qr_col, k_col=kr_col, v_col=vr_col, g_col=gr_col)
        xf = _merge(xf, y_conv, y_att, y_ret, *merge_w, proj, gate_b3, wo, mix_post,
                    layer=l, gates_col=gates_col)
        xf = _ffn(xf, *ffn2, layer=l)
    return xf.reshape(batch, seq, d)
```

```python
import functools
import math

import jax
import jax.numpy as jnp
from jax import lax
from jax.experimental import pallas as pl
from jax.experimental.pallas import tpu as pltpu

F32 = jnp.float32
BF16 = jnp.bfloat16

EPS = 1e-6
LANES = 128
V7X_VMEM_LIMIT_BYTES = 56 * 1024 * 1024
MASK_VALUE = -1e30
LOG2_E = math.log2(math.e)
BF16_SUBLANES = 16

CONV_WIDTH = 31
CONV_HALO = 32
ATT_HEAD_DIM = 128
ATT_ROPE_DIMS = ATT_HEAD_DIM // 4
ATT_ROPE_THETA = 500000.0
MOBA_BLOCK = 256
MOBA_TOPK = 3
RET_HEADS = 8
RET_CHUNK = 128
RET_ROT_THETA = 10000.0
N_BRANCH = 3


def _params(*semantics):
    return pltpu.CompilerParams(dimension_semantics=semantics,
                                vmem_limit_bytes=V7X_VMEM_LIMIT_BYTES)


def _layer_vec(layer, width):
    return pl.BlockSpec((None, 1, width), lambda *_: (layer, 0, 0))


def _rms(x, g):
    return x * lax.rsqrt(jnp.mean(x * x, axis=-1, keepdims=True) + EPS) * g


def _silu(x):
    return x * jax.nn.sigmoid(x)


def _dot(a, b):
    return jnp.dot(a, b, preferred_element_type=F32)


def _dot_nt(a, b, precision=None):
    return lax.dot_general(a, b, (((1,), (1,)), ((), ())), precision=precision,
                           preferred_element_type=F32)


def _rope(x, cos, sin_signed, half, group):
    n = x.shape[-1]
    lane = lax.broadcasted_iota(jnp.int32, x.shape, x.ndim - 1)
    first = (lane % group) < half
    partner = jnp.where(first, pltpu.roll(x, n - half, x.ndim - 1), pltpu.roll(x, half, x.ndim - 1))
    return x * cos + partner * sin_signed


def _ffn_body(x_ref, pre_ref, wg_ref, wu_ref, wd_ref, post_ref, o_ref, h_ref, acc_ref):
    j = pl.program_id(1)

    @pl.when(j == 0)
    def _():
        h_ref[...] = _rms(x_ref[...], pre_ref[...]).astype(BF16)
        acc_ref[...] = jnp.zeros_like(acc_ref)

    h = h_ref[...]
    g = _dot(h, wg_ref[...])
    u = _dot(h, wu_ref[...])
    a = (_silu(g) * u).astype(BF16)
    acc_ref[...] += _dot(a, wd_ref[...])

    @pl.when(j == pl.num_programs(1) - 1)
    def _():
        o_ref[...] = x_ref[...] + 0.5 * _rms(acc_ref[...], post_ref[...])


def _ffn(x, pre_g, wg, wu, wd, post_g, *, layer, tm=512, tf=512):
    m, d = x.shape
    f = wg.shape[2]
    return pl.pallas_call(
        _ffn_body,
        grid=(m // tm, f // tf),
        in_specs=[
            pl.BlockSpec((tm, d), lambda i, j: (i, 0)),
            _layer_vec(layer, d),
            pl.BlockSpec((None, d, tf), lambda i, j: (layer, 0, j)),
            pl.BlockSpec((None, d, tf), lambda i, j: (layer, 0, j)),
            pl.BlockSpec((None, tf, d), lambda i, j: (layer, j, 0)),
            _layer_vec(layer, d),
        ],
        out_specs=pl.BlockSpec((tm, d), lambda i, j: (i, 0)),
        out_shape=jax.ShapeDtypeStruct((m, d), F32),
        scratch_shapes=[pltpu.VMEM((tm, d), BF16), pltpu.VMEM((tm, d), F32)],
        compiler_params=_params("parallel", "arbitrary"),
        name="ffn",
    )(x, pre_g, wg, wu, wd, post_g)


def _inproj_body(x_ref, g_ref, w_ref, o_ref, h_ref):
    @pl.when(pl.program_id(1) == 0)
    def _():
        h_ref[...] = _rms(x_ref[...], g_ref[...]).astype(BF16)

    o_ref[...] = _dot(h_ref[...], w_ref[...].astype(BF16))


def _inproj(x, g, w, *, layer, tm=2048, tn=512):
    m, d = x.shape
    n = w.shape[2]
    return pl.pallas_call(
        _inproj_body,
        grid=(m // tm, n // tn),
        in_specs=[
            pl.BlockSpec((tm, d), lambda i, j: (i, 0), pipeline_mode=pl.Buffered(1)),
            _layer_vec(layer, d),
            pl.BlockSpec((None, d, tn), lambda i, j: (layer, 0, j)),
        ],
        out_specs=pl.BlockSpec((tm, tn), lambda i, j: (i, j)),
        out_shape=jax.ShapeDtypeStruct((m, n), F32),
        scratch_shapes=[pltpu.VMEM((tm, d), BF16)],
        compiler_params=_params("parallel", "arbitrary"),
        name="inproj",
    )(x, g, w)


def _conv_body(a_ref, g_ref, ap_ref, gp_ref, w_ref, b_ref, lng_ref, lnb_ref, o_ref,
               u_ref, y_ref, *, ts, rows_per_acc):
    nck = u_ref.shape[0]
    u = a_ref[...] * jax.nn.sigmoid(g_ref[...])
    up = ap_ref[...] * jax.nn.sigmoid(gp_ref[...])
    up = jnp.where(pl.program_id(1) > 0, up, 0.0)
    for c in range(nck):
        u_ref[c, 0:CONV_HALO, :] = up[:, c * LANES:(c + 1) * LANES]
        u_ref[c, CONV_HALO:CONV_HALO + ts, :] = u[:, c * LANES:(c + 1) * LANES]

    first = CONV_HALO - (CONV_WIDTH - 1)

    def chunk(c, carry):
        for r in range(ts // rows_per_acc):
            r0 = r * rows_per_acc
            acc = jnp.zeros((rows_per_acc, LANES), F32)
            for w in range(CONV_WIDTH):
                acc = acc + u_ref[c, r0 + first + w:r0 + first + w + rows_per_acc, :] * w_ref[c, w:w + 1, :]
            y_ref[c, r0:r0 + rows_per_acc, :] = acc
        return carry

    lax.fori_loop(0, nck, chunk, 0)

    y = jnp.concatenate([y_ref[c] for c in range(nck)], axis=-1) + b_ref[...]
    mu = jnp.mean(y, axis=-1, keepdims=True)
    yc = y - mu
    yn = yc * lax.rsqrt(jnp.mean(yc * yc, axis=-1, keepdims=True) + EPS) * lng_ref[...] + lnb_ref[...]
    o_ref[...] = _silu(yn).astype(BF16)


def _conv(proj, w3, b, ln_g, ln_b, *, layer, batch, seq, ch, ts=256, rows_per_acc=64):
    m = proj.shape[0]
    nck = ch // LANES
    tiles = seq // ts
    halo_per_tile = ts // CONV_HALO

    def cur(col):
        return pl.BlockSpec((ts, ch), lambda bi, i: (bi * tiles + i, col))

    def prev(col):
        return pl.BlockSpec(
            (CONV_HALO, ch),
            lambda bi, i: (jnp.maximum((bi * tiles + i) * halo_per_tile - 1, 0), col))

    vec = _layer_vec(layer, ch)
    return pl.pallas_call(
        functools.partial(_conv_body, ts=ts, rows_per_acc=rows_per_acc),
        grid=(batch, tiles),
        in_specs=[cur(0), cur(1), prev(0), prev(1),
                  pl.BlockSpec((None, nck, CONV_WIDTH, LANES), lambda bi, i: (layer, 0, 0, 0)),
                  vec, vec, vec],
        out_specs=pl.BlockSpec((ts, ch), lambda bi, i: (bi * tiles + i, 0)),
        out_shape=jax.ShapeDtypeStruct((m, ch), BF16),
        scratch_shapes=[pltpu.VMEM((nck, CONV_HALO + ts, LANES), F32),
                        pltpu.VMEM((nck, ts, LANES), F32)],
        compiler_params=_params("parallel", "parallel"),
        name="conv",
    )(proj, proj, proj, proj, w3, b, ln_g, ln_b)


def _moba_body(q_ref, k_ref, v_ref, cos_ref, sin_ref, o_ref, kb_ref, vt_ref, km_ref, s_ref, m_ref,
               *, nb, n_sel, scale):
    blk = MOBA_BLOCK
    hd = ATT_HEAD_DIM
    half = ATT_ROPE_DIMS // 2

    for j in range(nb):
        rows = slice(j * blk, (j + 1) * blk)
        kj = _rope(k_ref[rows, :], cos_ref[rows, :], sin_ref[rows, :], half, ATT_HEAD_DIM)
        kb_ref[rows, :] = kj.astype(BF16)
        km_ref[j:j + 1, :] = jnp.mean(kj, axis=0, keepdims=True)
        vt_ref[0:hd, rows] = v_ref[rows, :].T.astype(BF16)
    vt_ref[hd:, :] = jnp.ones((vt_ref.shape[0] - hd, vt_ref.shape[1]), BF16)

    kidx = lax.broadcasted_iota(jnp.int32, (blk, blk), 0)
    qidx = lax.broadcasted_iota(jnp.int32, (blk, blk), 1)
    causal = kidx <= qidx

    def score_stage(i):
        rows = slice(i * blk, (i + 1) * blk)
        nk = (i + 1) * blk
        q = _rope(q_ref[rows, :], cos_ref[rows, :], sin_ref[rows, :], half, ATT_HEAD_DIM)
        s = _dot_nt(kb_ref[0:nk, :], (q * (scale * LOG2_E)).astype(BF16))
        parts = []
        if i > n_sel:
            gate = _dot_nt(km_ref[...], q, precision=lax.Precision.HIGHEST)
            brow = lax.broadcasted_iota(jnp.int32, (nb, blk), 0)
            rank = jnp.zeros((nb, blk), jnp.int32)
            for jp in range(i):
                gj = gate[jp:jp + 1, :]
                rank = rank + jnp.where((gj > gate) | ((gj == gate) & (jp < brow)), 1, 0)
            bias = jnp.where(rank < n_sel, 0.0, MASK_VALUE)
            for j in range(i):
                parts.append(s[j * blk:(j + 1) * blk, :] + bias[j:j + 1, :])
        elif i > 0:
            parts.append(s[0:i * blk, :])
        parts.append(jnp.where(causal, s[i * blk:nk, :], MASK_VALUE))
        s = jnp.concatenate(parts, axis=0) if len(parts) > 1 else parts[0]
        s_ref[i % 2, 0:nk, :] = s
        m_ref[i % 2] = jnp.max(s, axis=0, keepdims=True)

    def value_stage(i):
        rows = slice(i * blk, (i + 1) * blk)
        nk = (i + 1) * blk
        p = jnp.exp2(s_ref[i % 2, 0:nk, :] - m_ref[i % 2]).astype(BF16)
        acc = _dot(vt_ref[:, 0:nk], p)
        o_ref[rows, :] = (acc[0:hd, :] / acc[hd:hd + 1, :]).T.astype(BF16)

    always = pl.program_id(0) >= 0
    for step in range(nb + 1):
        @pl.when(always)
        def _():
            if step < nb:
                score_stage(step)
            if step > 0:
                value_stage(step - 1)


def _moba(proj, cos_t, sin_t, *, batch, seq, heads, q_col, k_col, v_col):
    m = proj.shape[0]
    hd = ATT_HEAD_DIM
    nb = seq // MOBA_BLOCK
    n_sel = min(MOBA_TOPK, nb - 1)
    table = pl.BlockSpec((seq, hd), lambda b, h: (0, 0))

    def head(col):
        return pl.BlockSpec((seq, hd), lambda b, h: (b, col // hd + h))

    return pl.pallas_call(
        functools.partial(_moba_body, nb=nb, n_sel=n_sel, scale=hd ** -0.5),
        grid=(batch, heads),
        in_specs=[head(q_col), head(k_col), head(v_col), table, table],
        out_specs=pl.BlockSpec((seq, hd), lambda b, h: (b, h)),
        out_shape=jax.ShapeDtypeStruct((m, heads * hd), BF16),
        scratch_shapes=[
            pltpu.VMEM((seq, hd), BF16),
            pltpu.VMEM((hd + BF16_SUBLANES, seq), BF16),
            pltpu.VMEM((nb, hd), F32),
            pltpu.VMEM((2, seq, MOBA_BLOCK), F32),
            pltpu.VMEM((2, 1, MOBA_BLOCK), F32),
        ],
        compiler_params=_params("parallel", "parallel"),
        name="moba",
    )(proj, proj, proj, cos_t, sin_t)


def _ret_body(q_ref, k_ref, v_ref, gr_ref, cos_ref, sin_ref, dm_ref, qd_ref, kd_ref, cd_ref,
              ng_ref, o_ref, st_ref, *, heads, dk, dv, chunks):
    c = RET_CHUNK
    per_group = LANES // dk

    @pl.when(pl.program_id(1) == 0)
    def _():
        st_ref[...] = jnp.zeros_like(st_ref)

    for cc in range(chunks):
        rows = slice(cc * c, (cc + 1) * c)
        cos = cos_ref[rows, :]
        sin = sin_ref[rows, :]
        for hg in range(heads // per_group):
            lanes = slice(hg * LANES, (hg + 1) * LANES)
            qr = _rope(q_ref[rows, lanes], cos, sin, dk // 2, dk)
            kr = _rope(k_ref[rows, lanes], cos, sin, dk // 2, dk) * (dk ** -0.5)
            krt = kr.T
            for hh in range(per_group):
                h = hg * per_group + hh
                hl = slice(hh * dk, (hh + 1) * dk)
                vl = slice(h * dv, (h + 1) * dv)
                qh = qr[:, hl]
                vb = v_ref[rows, vl].astype(BF16)
                inner = _dot_nt(qh.astype(BF16), kr[:, hl].astype(BF16)) * dm_ref[h]
                ro = _dot(inner.astype(BF16), vb)
                st = st_ref[h]
                ro = ro + _dot((qh * qd_ref[h]).astype(BF16), st.astype(BF16))
                kv = _dot((krt[hl, :] * kd_ref[h]).astype(BF16), vb)
                st_ref[h] = st * cd_ref[h] + kv
                ro = ro * lax.rsqrt(jnp.mean(ro * ro, axis=-1, keepdims=True) + EPS) * ng_ref[:, vl]
                o_ref[rows, vl] = (_silu(gr_ref[rows, vl]) * ro).astype(BF16)


def _retention(proj, cos_t, sin_t, dm, qd, kd, cd, norm_g, *, layer, batch, seq, heads, dk, dv,
               q_col, k_col, v_col, g_col, chunks=4):
    m = proj.shape[0]
    ts = chunks * RET_CHUNK
    tiles = seq // ts
    qk_w = heads * dk
    v_w = heads * dv

    def tok(width, col):
        return pl.BlockSpec((ts, width), lambda b, t: (b * tiles + t, col // width))

    def const(shape):
        return pl.BlockSpec(shape, lambda b, t: (0,) * len(shape))

    return pl.pallas_call(
        functools.partial(_ret_body, heads=heads, dk=dk, dv=dv, chunks=chunks),
        grid=(batch, tiles),
        in_specs=[
            tok(qk_w, q_col), tok(qk_w, k_col), tok(v_w, v_col), tok(v_w, g_col),
            pl.BlockSpec((ts, LANES), lambda b, t: (t, 0)),
            pl.BlockSpec((ts, LANES), lambda b, t: (t, 0)),
            const(dm.shape), const(qd.shape), const(kd.shape), const(cd.shape),
            _layer_vec(layer, v_w),
        ],
        out_specs=pl.BlockSpec((ts, v_w), lambda b, t: (b * tiles + t, 0)),
        out_shape=jax.ShapeDtypeStruct((m, v_w), BF16),
        scratch_shapes=[pltpu.VMEM((heads, dk, dv), F32)],
        compiler_params=_params("parallel", "arbitrary"),
        name="retention",
    )(proj, proj, proj, proj, cos_t, sin_t, dm, qd, kd, cd, norm_g)


def _merge_body(x_ref, yc_ref, ya_ref, yr_ref, wc_ref, wa_ref, wr_ref, g0_ref, g1_ref, g2_ref,
                gb_ref, wo_ref, post_ref, o_ref, mg_ref, *, tn):
    d = o_ref.shape[1]
    for t in range(d // tn):
        cols = slice(t * tn, (t + 1) * tn)
        merged = (jax.nn.sigmoid(g0_ref[:, cols] + gb_ref[0:1, cols]) * _dot(yc_ref[...], wc_ref[:, cols])
                  + jax.nn.sigmoid(g1_ref[:, cols] + gb_ref[1:2, cols]) * _dot(ya_ref[...], wa_ref[:, cols])
                  + jax.nn.sigmoid(g2_ref[:, cols] + gb_ref[2:3, cols]) * _dot(yr_ref[...], wr_ref[:, cols]))
        mg_ref[:, cols] = merged.astype(BF16)
    o_ref[...] = x_ref[...] + _rms(_dot(mg_ref[...], wo_ref[...]), post_ref[...])


def _merge(x, yc, ya, yr, wc, wa, wr, proj, gate_b, wo, post_g, *, layer, gates_col, tm=256, tn=512):
    m, d = x.shape

    def ytile(a):
        return pl.BlockSpec((tm, a.shape[1]), lambda i: (i, 0))

    def resident(w):
        return pl.BlockSpec((None,) + w.shape[1:], lambda i: (layer, 0, 0),
                            pipeline_mode=pl.Buffered(1))

    def gtile(br):
        return pl.BlockSpec((tm, d), lambda i: (i, gates_col // d + br))

    return pl.pallas_call(
        functools.partial(_merge_body, tn=tn),
        grid=(m // tm,),
        in_specs=[
            pl.BlockSpec((tm, d), lambda i: (i, 0)),
            ytile(yc), ytile(ya), ytile(yr), resident(wc), resident(wa), resident(wr),
            gtile(0), gtile(1), gtile(2),
            pl.BlockSpec((None, N_BRANCH, d), lambda i: (layer, 0, 0)),
            resident(wo),
            _layer_vec(layer, d),
        ],
        out_specs=pl.BlockSpec((tm, d), lambda i: (i, 0)),
        out_shape=jax.ShapeDtypeStruct((m, d), F32),
        scratch_shapes=[pltpu.VMEM((tm, d), BF16)],
        compiler_params=_params("parallel"),
        name="merge",
    )(x, yc, ya, yr, wc, wa, wr, proj, proj, proj, gate_b, wo, post_g)


def _rope_tables(seq, n_rot, theta, group):
    half = n_rot // 2
    inv = 1.0 / (theta ** (jnp.arange(half, dtype=F32) / half))
    ang = jnp.arange(seq, dtype=jnp.int32).astype(F32)[:, None] * inv[None, :]
    cos, sin = jnp.cos(ang), jnp.sin(ang)
    rest = group - n_rot
    cos_g = jnp.concatenate([cos, cos, jnp.ones((seq, rest), F32)], axis=-1)
    sin_g = jnp.concatenate([-sin, sin, jnp.zeros((seq, rest), F32)], axis=-1)
    reps = LANES // group
    return jnp.tile(cos_g, (1, reps)), jnp.tile(sin_g, (1, reps))


def _retention_constants(heads):
    c = RET_CHUNK
    log_g = jnp.log1p(-(2.0 ** (-5.0 - jnp.arange(heads, dtype=F32))))
    idx = jnp.arange(c, dtype=F32)
    diff = idx[:, None] - idx[None, :]
    decay_mask = jnp.exp(jnp.where(diff >= 0, log_g[:, None, None] * diff, -jnp.inf))
    q_decay = jnp.exp(log_g[:, None] * (idx + 1.0))[:, :, None]
    k_decay = jnp.exp(log_g[:, None] * (c - 1.0 - idx))[:, None, :]
    chunk_decay = jnp.broadcast_to(jnp.exp(log_g * c)[:, None, None], (heads, 1, LANES))
    return decay_mask, q_decay, k_decay, chunk_decay


def kernel(x, ffn1_pre_g, ffn1_w_gate, ffn1_w_up, ffn1_w_down, ffn1_post_g, mix_pre_g, w_in, conv_dw_w, conv_dw_b, conv_ln_g, conv_ln_b, ret_norm_g, w_br_conv, w_br_att, w_br_ret, gate_b, w_out, mix_post_g, ffn2_pre_g, ffn2_w_gate, ffn2_w_up, ffn2_w_down, ffn2_post_g):
    batch, seq, d = x.shape
    depth = w_in.shape[0]
    conv_ch = conv_dw_w.shape[2]
    att_w = w_br_att.shape[1]
    ret_v_w = w_br_ret.shape[1]
    att_heads = att_w // ATT_HEAD_DIM
    ret_dv = ret_v_w // RET_HEADS
    ret_dk = ret_dv // 2
    ret_qk_w = RET_HEADS * ret_dk
    sizes = [conv_ch, conv_ch, att_w, att_w, att_w, ret_qk_w, ret_qk_w, ret_v_w, ret_v_w, N_BRANCH * d]
    assert sum(sizes) == w_in.shape[2]
    cols = [0]
    for s in sizes[:-1]:
        cols.append(cols[-1] + s)
    (_, _, qa_col, ka_col, va_col, qr_col, kr_col, vr_col, gr_col, gates_col) = cols

    att_cos, att_sin = _rope_tables(seq, ATT_ROPE_DIMS, ATT_ROPE_THETA, ATT_HEAD_DIM)
    ret_cos, ret_sin = _rope_tables(seq, ret_dk, RET_ROT_THETA, ret_dk)
    dm, qd, kd, cd = _retention_constants(RET_HEADS)

    bf = lambda w: w.astype(BF16)
    rows = lambda v: v.reshape(depth, 1, -1)
    ffn1 = (rows(ffn1_pre_g), bf(ffn1_w_gate), bf(ffn1_w_up), bf(ffn1_w_down), rows(ffn1_post_g))
    ffn2 = (rows(ffn2_pre_g), bf(ffn2_w_gate), bf(ffn2_w_up), bf(ffn2_w_down), rows(ffn2_post_g))
    conv_w = conv_dw_w.reshape(depth, CONV_WIDTH, conv_ch // LANES, LANES).transpose(0, 2, 1, 3)
    conv_p = (conv_w, rows(conv_dw_b), rows(conv_ln_g), rows(conv_ln_b))
    merge_w = (bf(w_br_conv), bf(w_br_att), bf(w_br_ret))
    gate_b3 = gate_b.reshape(depth, N_BRANCH, d)
    wo = bf(w_out)
    mix_pre, mix_post, ret_g = rows(mix_pre_g), rows(mix_post_g), rows(ret_norm_g)

    xf = x.reshape(batch * seq, d)
    for l in range(depth):
        xf = _ffn(xf, *ffn1, layer=l)
        proj = _inproj(xf, mix_pre, w_in, layer=l)
        y_conv = _conv(proj, *conv_p, layer=l, batch=batch, seq=seq, ch=conv_ch)
        y_att = _moba(proj, att_cos, att_sin, batch=batch, seq=seq, heads=att_heads,
                      q_col=qa_col, k_col=ka_col, v_col=va_col)
        y_ret = _retention(proj, ret_cos, ret_sin, dm, qd, kd, cd, ret_g, layer=l,
                           batch=batch, seq=seq, heads=RET_HEADS, dk=ret_dk, dv=ret_dv,
                           q_col=qr_col, k_col=kr_col, v_col=vr_col, g_col=gr_col)
        xf = _merge(xf, y_conv, y_att, y_ret, *merge_w, proj, gate_b3, wo, mix_post,
                    layer=l, gates_col=gates_col)
        xf = _ffn(xf, *ffn2, layer=l)
    return xf.reshape(batch, seq, d)
```

```python
import functools
import math

import jax
import jax.numpy as jnp
from jax import lax
from jax.experimental import pallas as pl
from jax.experimental.pallas import tpu as pltpu

F32 = jnp.float32
BF16 = jnp.bfloat16

EPS = 1e-6
LANES = 128
V7X_VMEM_LIMIT_BYTES = 56 * 1024 * 1024
MASK_VALUE = -1e30
LOG2_E = math.log2(math.e)
BF16_SUBLANES = 16

CONV_WIDTH = 31
CONV_HALO = 32
ATT_HEAD_DIM = 128
ATT_ROPE_DIMS = ATT_HEAD_DIM // 4
ATT_ROPE_THETA = 500000.0
MOBA_BLOCK = 256
MOBA_TOPK = 3
RET_HEADS = 8
RET_CHUNK = 128
RET_ROT_THETA = 10000.0
N_BRANCH = 3


def _params(*semantics):
    return pltpu.CompilerParams(dimension_semantics=semantics,
                                vmem_limit_bytes=V7X_VMEM_LIMIT_BYTES)


def _layer_vec(layer, width):
    return pl.BlockSpec((None, 1, width), lambda *_: (layer, 0, 0))


def _rms(x, g):
    return x * lax.rsqrt(jnp.mean(x * x, axis=-1, keepdims=True) + EPS) * g


def _silu(x):
    return x * jax.nn.sigmoid(x)


def _dot(a, b):
    return jnp.dot(a, b, preferred_element_type=F32)


def _dot_nt(a, b, precision=None):
    return lax.dot_general(a, b, (((1,), (1,)), ((), ())), precision=precision,
                           preferred_element_type=F32)


def _rope(x, cos, sin_signed, half, group):
    n = x.shape[-1]
    lane = lax.broadcasted_iota(jnp.int32, x.shape, x.ndim - 1)
    first = (lane % group) < half
    partner = jnp.where(first, pltpu.roll(x, n - half, x.ndim - 1), pltpu.roll(x, half, x.ndim - 1))
    return x * cos + partner * sin_signed


def _ffn_body(x_ref, pre_ref, wg_ref, wu_ref, wd_ref, post_ref, o_ref, h_ref):
    j = pl.program_id(1)

    @pl.when(j == 0)
    def _():
        h_ref[...] = _rms(x_ref[...], pre_ref[...]).astype(BF16)
        o_ref[...] = jnp.zeros_like(o_ref)

    h = h_ref[...]
    g = _dot(h, wg_ref[...].astype(BF16))
    u = _dot(h, wu_ref[...].astype(BF16))
    a = (_silu(g) * u).astype(BF16)
    o_ref[...] += _dot(a, wd_ref[...].astype(BF16))

    @pl.when(j == pl.num_programs(1) - 1)
    def _():
        o_ref[...] = x_ref[...] + 0.5 * _rms(o_ref[...], post_ref[...])


def _ffn(x, pre_g, wg, wu, wd, post_g, *, layer, tm=1024, tf=256):
    m, d = x.shape
    f = wg.shape[2]
    return pl.pallas_call(
        _ffn_body,
        grid=(m // tm, f // tf),
        in_specs=[
            pl.BlockSpec((tm, d), lambda i, j: (i, 0), pipeline_mode=pl.Buffered(1)),
            _layer_vec(layer, d),
            pl.BlockSpec((None, d, tf), lambda i, j: (layer, 0, j)),
            pl.BlockSpec((None, d, tf), lambda i, j: (layer, 0, j)),
            pl.BlockSpec((None, tf, d), lambda i, j: (layer, j, 0)),
            _layer_vec(layer, d),
        ],
        out_specs=pl.BlockSpec((tm, d), lambda i, j: (i, 0)),
        out_shape=jax.ShapeDtypeStruct((m, d), F32),
        scratch_shapes=[pltpu.VMEM((tm, d), BF16)],
        compiler_params=_params("parallel", "arbitrary"),
        name="ffn",
    )(x, pre_g, wg, wu, wd, post_g)


def _inproj_body(x_ref, g_ref, w_ref, o_ref, h_ref):
    @pl.when(pl.program_id(1) == 0)
    def _():
        h_ref[...] = _rms(x_ref[...], g_ref[...]).astype(BF16)

    o_ref[...] = _dot(h_ref[...], w_ref[...].astype(BF16))


def _inproj(x, g, w, *, layer, tm=2048, tn=512):
    m, d = x.shape
    n = w.shape[2]
    return pl.pallas_call(
        _inproj_body,
        grid=(m // tm, n // tn),
        in_specs=[
            pl.BlockSpec((tm, d), lambda i, j: (i, 0), pipeline_mode=pl.Buffered(1)),
            _layer_vec(layer, d),
            pl.BlockSpec((None, d, tn), lambda i, j: (layer, 0, j)),
        ],
        out_specs=pl.BlockSpec((tm, tn), lambda i, j: (i, j)),
        out_shape=jax.ShapeDtypeStruct((m, n), F32),
        scratch_shapes=[pltpu.VMEM((tm, d), BF16)],
        compiler_params=_params("parallel", "arbitrary"),
        name="inproj",
    )(x, g, w)


def _conv_body(a_ref, g_ref, ap_ref, gp_ref, w_ref, b_ref, lng_ref, lnb_ref, o_ref,
               u_ref, y_ref, *, ts, rows_per_acc):
    nck = u_ref.shape[0]
    u = a_ref[...] * jax.nn.sigmoid(g_ref[...])
    up = ap_ref[...] * jax.nn.sigmoid(gp_ref[...])
    up = jnp.where(pl.program_id(1) > 0, up, 0.0)
    for c in range(nck):
        u_ref[c, 0:CONV_HALO, :] = up[:, c * LANES:(c + 1) * LANES]
        u_ref[c, CONV_HALO:CONV_HALO + ts, :] = u[:, c * LANES:(c + 1) * LANES]

    first = CONV_HALO - (CONV_WIDTH - 1)

    def chunk(c, carry):
        for r in range(ts // rows_per_acc):
            r0 = r * rows_per_acc
            acc = jnp.zeros((rows_per_acc, LANES), F32)
            for w in range(CONV_WIDTH):
                acc = acc + u_ref[c, r0 + first + w:r0 + first + w + rows_per_acc, :] * w_ref[c, w:w + 1, :]
            y_ref[c, r0:r0 + rows_per_acc, :] = acc
        return carry

    lax.fori_loop(0, nck, chunk, 0)

    y = jnp.concatenate([y_ref[c] for c in range(nck)], axis=-1) + b_ref[...]
    mu = jnp.mean(y, axis=-1, keepdims=True)
    yc = y - mu
    yn = yc * lax.rsqrt(jnp.mean(yc * yc, axis=-1, keepdims=True) + EPS) * lng_ref[...] + lnb_ref[...]
    o_ref[...] = _silu(yn).astype(BF16)


def _conv(proj, w3, b, ln_g, ln_b, *, layer, batch, seq, ch, ts=256, rows_per_acc=64):
    m = proj.shape[0]
    nck = ch // LANES
    tiles = seq // ts
    halo_per_tile = ts // CONV_HALO

    def cur(col):
        return pl.BlockSpec((ts, ch), lambda bi, i: (bi * tiles + i, col))

    def prev(col):
        return pl.BlockSpec(
            (CONV_HALO, ch),
            lambda bi, i: (jnp.maximum((bi * tiles + i) * halo_per_tile - 1, 0), col))

    vec = _layer_vec(layer, ch)
    return pl.pallas_call(
        functools.partial(_conv_body, ts=ts, rows_per_acc=rows_per_acc),
        grid=(batch, tiles),
        in_specs=[cur(0), cur(1), prev(0), prev(1),
                  pl.BlockSpec((None, nck, CONV_WIDTH, LANES), lambda bi, i: (layer, 0, 0, 0)),
                  vec, vec, vec],
        out_specs=pl.BlockSpec((ts, ch), lambda bi, i: (bi * tiles + i, 0)),
        out_shape=jax.ShapeDtypeStruct((m, ch), BF16),
        scratch_shapes=[pltpu.VMEM((nck, CONV_HALO + ts, LANES), F32),
                        pltpu.VMEM((nck, ts, LANES), F32)],
        compiler_params=_params("parallel", "parallel"),
        name="conv",
    )(proj, proj, proj, proj, w3, b, ln_g, ln_b)


def _moba_body(q_ref, k_ref, v_ref, cos_ref, sin_ref, o_ref, kb_ref, vt_ref, km_ref, s_ref, m_ref,
               *, nb, n_sel, scale):
    blk = MOBA_BLOCK
    hd = ATT_HEAD_DIM
    half = ATT_ROPE_DIMS // 2

    for j in range(nb):
        rows = slice(j * blk, (j + 1) * blk)
        kj = _rope(k_ref[rows, :], cos_ref[rows, :], sin_ref[rows, :], half, ATT_HEAD_DIM)
        kb_ref[rows, :] = kj.astype(BF16)
        km_ref[j:j + 1, :] = jnp.mean(kj, axis=0, keepdims=True)
        vt_ref[0:hd, rows] = v_ref[rows, :].T.astype(BF16)
    vt_ref[hd:, :] = jnp.ones((vt_ref.shape[0] - hd, vt_ref.shape[1]), BF16)

    kidx = lax.broadcasted_iota(jnp.int32, (blk, blk), 0)
    qidx = lax.broadcasted_iota(jnp.int32, (blk, blk), 1)
    causal = kidx <= qidx

    def score_stage(i):
        rows = slice(i * blk, (i + 1) * blk)
        nk = (i + 1) * blk
        q = _rope(q_ref[rows, :], cos_ref[rows, :], sin_ref[rows, :], half, ATT_HEAD_DIM)
        s = _dot_nt(kb_ref[0:nk, :], (q * (scale * LOG2_E)).astype(BF16))
        parts = []
        if i > n_sel:
            gate = _dot_nt(km_ref[...], q, precision=lax.Precision.HIGHEST)
            brow = lax.broadcasted_iota(jnp.int32, (nb, blk), 0)
            rank = jnp.zeros((nb, blk), jnp.int32)
            for jp in range(i):
                gj = gate[jp:jp + 1, :]
                rank = rank + jnp.where((gj > gate) | ((gj == gate) & (jp < brow)), 1, 0)
            bias = jnp.where(rank < n_sel, 0.0, MASK_VALUE)
            for j in range(i):
                parts.append(s[j * blk:(j + 1) * blk, :] + bias[j:j + 1, :])
        elif i > 0:
            parts.append(s[0:i * blk, :])
        parts.append(jnp.where(causal, s[i * blk:nk, :], MASK_VALUE))
        s = jnp.concatenate(parts, axis=0) if len(parts) > 1 else parts[0]
        s_ref[i % 2, 0:nk, :] = s
        m_ref[i % 2] = jnp.max(s, axis=0, keepdims=True)

    def value_stage(i):
        rows = slice(i * blk, (i + 1) * blk)
        nk = (i + 1) * blk
        p = jnp.exp2(s_ref[i % 2, 0:nk, :] - m_ref[i % 2]).astype(BF16)
        acc = _dot(vt_ref[:, 0:nk], p)
        o_ref[rows, :] = (acc[0:hd, :] / acc[hd:hd + 1, :]).T.astype(BF16)

    always = pl.program_id(0) >= 0
    for step in range(nb + 1):
        @pl.when(always)
        def _():
            if step < nb:
                score_stage(step)
            if step > 0:
                value_stage(step - 1)


def _moba(proj, cos_t, sin_t, *, batch, seq, heads, q_col, k_col, v_col):
    m = proj.shape[0]
    hd = ATT_HEAD_DIM
    nb = seq // MOBA_BLOCK
    n_sel = min(MOBA_TOPK, nb - 1)
    table = pl.BlockSpec((seq, hd), lambda b, h: (0, 0))

    def head(col):
        return pl.BlockSpec((seq, hd), lambda b, h: (b, col // hd + h))

    return pl.pallas_call(
        functools.partial(_moba_body, nb=nb, n_sel=n_sel, scale=hd ** -0.5),
        grid=(batch, heads),
        in_specs=[head(q_col), head(k_col), head(v_col), table, table],
        out_specs=pl.BlockSpec((seq, hd), lambda b, h: (b, h)),
        out_shape=jax.ShapeDtypeStruct((m, heads * hd), BF16),
        scratch_shapes=[
            pltpu.VMEM((seq, hd), BF16),
            pltpu.VMEM((hd + BF16_SUBLANES, seq), BF16),
            pltpu.VMEM((nb, hd), F32),
            pltpu.VMEM((2, seq, MOBA_BLOCK), F32),
            pltpu.VMEM((2, 1, MOBA_BLOCK), F32),
        ],
        compiler_params=_params("parallel", "parallel"),
        name="moba",
    )(proj, proj, proj, cos_t, sin_t)


def _ret_body(q_ref, k_ref, v_ref, gr_ref, cos_ref, sin_ref, dm_ref, qd_ref, kd_ref, cd_ref,
              ng_ref, o_ref, st_ref, *, heads, dk, dv, chunks):
    c = RET_CHUNK
    per_group = LANES // dk

    @pl.when(pl.program_id(1) == 0)
    def _():
        st_ref[...] = jnp.zeros_like(st_ref)

    for cc in range(chunks):
        rows = slice(cc * c, (cc + 1) * c)
        cos = cos_ref[rows, :]
        sin = sin_ref[rows, :]
        for hg in range(heads // per_group):
            lanes = slice(hg * LANES, (hg + 1) * LANES)
            qr = _rope(q_ref[rows, lanes], cos, sin, dk // 2, dk)
            kr = _rope(k_ref[rows, lanes], cos, sin, dk // 2, dk) * (dk ** -0.5)
            krt = kr.T
            for hh in range(per_group):
                h = hg * per_group + hh
                hl = slice(hh * dk, (hh + 1) * dk)
                vl = slice(h * dv, (h + 1) * dv)
                qh = qr[:, hl]
                vb = v_ref[rows, vl].astype(BF16)
                inner = _dot_nt(qh.astype(BF16), kr[:, hl].astype(BF16)) * dm_ref[h]
                ro = _dot(inner.astype(BF16), vb)
                st = st_ref[h]
                ro = ro + _dot((qh * qd_ref[h]).astype(BF16), st.astype(BF16))
                kv = _dot((krt[hl, :] * kd_ref[h]).astype(BF16), vb)
                st_ref[h] = st * cd_ref[h] + kv
                ro = ro * lax.rsqrt(jnp.mean(ro * ro, axis=-1, keepdims=True) + EPS) * ng_ref[:, vl]
                o_ref[rows, vl] = (_silu(gr_ref[rows, vl]) * ro).astype(BF16)


def _retention(proj, cos_t, sin_t, dm, qd, kd, cd, norm_g, *, layer, batch, seq, heads, dk, dv,
               q_col, k_col, v_col, g_col, chunks=4):
    m = proj.shape[0]
    ts = chunks * RET_CHUNK
    tiles = seq // ts
    qk_w = heads * dk
    v_w = heads * dv

    def tok(width, col):
        return pl.BlockSpec((ts, width), lambda b, t: (b * tiles + t, col // width))

    def const(shape):
        return pl.BlockSpec(shape, lambda b, t: (0,) * len(shape))

    return pl.pallas_call(
        functools.partial(_ret_body, heads=heads, dk=dk, dv=dv, chunks=chunks),
        grid=(batch, tiles),
        in_specs=[
            tok(qk_w, q_col), tok(qk_w, k_col), tok(v_w, v_col), tok(v_w, g_col),
            pl.BlockSpec((ts, LANES), lambda b, t: (t, 0)),
            pl.BlockSpec((ts, LANES), lambda b, t: (t, 0)),
            const(dm.shape), const(qd.shape), const(kd.shape), const(cd.shape),
            _layer_vec(layer, v_w),
        ],
        out_specs=pl.BlockSpec((ts, v_w), lambda b, t: (b * tiles + t, 0)),
        out_shape=jax.ShapeDtypeStruct((m, v_w), BF16),
        scratch_shapes=[pltpu.VMEM((heads, dk, dv), F32)],
        compiler_params=_params("parallel", "arbitrary"),
        name="retention",
    )(proj, proj, proj, proj, cos_t, sin_t, dm, qd, kd, cd, norm_g)


def _merge_body(x_ref, yc_ref, ya_ref, yr_ref, wc_ref, wa_ref, wr_ref, g0_ref, g1_ref, g2_ref,
                gb_ref, wo_ref, post_ref, o_ref, mg_ref, *, tn):
    d = o_ref.shape[1]
    for t in range(d // tn):
        cols = slice(t * tn, (t + 1) * tn)
        merged = (jax.nn.sigmoid(g0_ref[:, cols] + gb_ref[0:1, cols]) * _dot(yc_ref[...], wc_ref[:, cols])
                  + jax.nn.sigmoid(g1_ref[:, cols] + gb_ref[1:2, cols]) * _dot(ya_ref[...], wa_ref[:, cols])
                  + jax.nn.sigmoid(g2_ref[:, cols] + gb_ref[2:3, cols]) * _dot(yr_ref[...], wr_ref[:, cols]))
        mg_ref[:, cols] = merged.astype(BF16)
    o_ref[...] = x_ref[...] + _rms(_dot(mg_ref[...], wo_ref[...]), post_ref[...])


def _merge(x, yc, ya, yr, wc, wa, wr, proj, gate_b, wo, post_g, *, layer, gates_col, tm=256, tn=512):
    m, d = x.shape

    def ytile(a):
        return pl.BlockSpec((tm, a.shape[1]), lambda i: (i, 0))

    def resident(w):
        return pl.BlockSpec((None,) + w.shape[1:], lambda i: (layer, 0, 0),
                            pipeline_mode=pl.Buffered(1))

    def gtile(br):
        return pl.BlockSpec((tm, d), lambda i: (i, gates_col // d + br))

    return pl.pallas_call(
        functools.partial(_merge_body, tn=tn),
        grid=(m // tm,),
        in_specs=[
            pl.BlockSpec((tm, d), lambda i: (i, 0)),
            ytile(yc), ytile(ya), ytile(yr), resident(wc), resident(wa), resident(wr),
            gtile(0), gtile(1), gtile(2),
            pl.BlockSpec((None, N_BRANCH, d), lambda i: (layer, 0, 0)),
            resident(wo),
            _layer_vec(layer, d),
        ],
        out_specs=pl.BlockSpec((tm, d), lambda i: (i, 0)),
        out_shape=jax.ShapeDtypeStruct((m, d), F32),
        scratch_shapes=[pltpu.VMEM((tm, d), BF16)],
        compiler_params=_params("parallel"),
        name="merge",
    )(x, yc, ya, yr, wc, wa, wr, proj, proj, proj, gate_b, wo, post_g)


def _rope_tables(seq, n_rot, theta, group):
    half = n_rot // 2
    inv = 1.0 / (theta ** (jnp.arange(half, dtype=F32) / half))
    ang = jnp.arange(seq, dtype=jnp.int32).astype(F32)[:, None] * inv[None, :]
    cos, sin = jnp.cos(ang), jnp.sin(ang)
    rest = group - n_rot
    cos_g = jnp.concatenate([cos, cos, jnp.ones((seq, rest), F32)], axis=-1)
    sin_g = jnp.concatenate([-sin, sin, jnp.zeros((seq, rest), F32)], axis=-1)
    reps = LANES // group
    return jnp.tile(cos_g, (1, reps)), jnp.tile(sin_g, (1, reps))


def _retention_constants(heads):
    c = RET_CHUNK
    log_g = jnp.log1p(-(2.0 ** (-5.0 - jnp.arange(heads, dtype=F32))))
    idx = jnp.arange(c, dtype=F32)
    diff = idx[:, None] - idx[None, :]
    decay_mask = jnp.exp(jnp.where(diff >= 0, log_g[:, None, None] * diff, -jnp.inf))
    q_decay = jnp.exp(log_g[:, None] * (idx + 1.0))[:, :, None]
    k_decay = jnp.exp(log_g[:, None] * (c - 1.0 - idx))[:, None, :]
    chunk_decay = jnp.broadcast_to(jnp.exp(log_g * c)[:, None, None], (heads, 1, LANES))
    return decay_mask, q_decay, k_decay, chunk_decay


def kernel(x, ffn1_pre_g, ffn1_w_gate, ffn1_w_up, ffn1_w_down, ffn1_post_g, mix_pre_g, w_in, conv_dw_w, conv_dw_b, conv_ln_g, conv_ln_b, ret_norm_g, w_br_conv, w_br_att, w_br_ret, gate_b, w_out, mix_post_g, ffn2_pre_g, ffn2_w_gate, ffn2_w_up, ffn2_w_down, ffn2_post_g):
    batch, seq, d = x.shape
    depth = w_in.shape[0]
    conv_ch = conv_dw_w.shape[2]
    att_w = w_br_att.shape[1]
    ret_v_w = w_br_ret.shape[1]
    att_heads = att_w // ATT_HEAD_DIM
    ret_dv = ret_v_w // RET_HEADS
    ret_dk = ret_dv // 2
    ret_qk_w = RET_HEADS * ret_dk
    sizes = [conv_ch, conv_ch, att_w, att_w, att_w, ret_qk_w, ret_qk_w, ret_v_w, ret_v_w, N_BRANCH * d]
    assert sum(sizes) == w_in.shape[2]
    cols = [0]
    for s in sizes[:-1]:
        cols.append(cols[-1] + s)
    (_, _, qa_col, ka_col, va_col, qr_col, kr_col, vr_col, gr_col, gates_col) = cols

    att_cos, att_sin = _rope_tables(seq, ATT_ROPE_DIMS, ATT_ROPE_THETA, ATT_HEAD_DIM)
    ret_cos, ret_sin = _rope_tables(seq, ret_dk, RET_ROT_THETA, ret_dk)
    dm, qd, kd, cd = _retention_constants(RET_HEADS)

    bf = lambda w: w.astype(BF16)
    rows = lambda v: v.reshape(depth, 1, -1)
    ffn1 = (rows(ffn1_pre_g), ffn1_w_gate, ffn1_w_up, ffn1_w_down, rows(ffn1_post_g))
    ffn2 = (rows(ffn2_pre_g), ffn2_w_gate, ffn2_w_up, ffn2_w_down, rows(ffn2_post_g))
    conv_w = conv_dw_w.reshape(depth, CONV_WIDTH, conv_ch // LANES, LANES).transpose(0, 2, 1, 3)
    conv_p = (conv_w, rows(conv_dw_b), rows(conv_ln_g), rows(conv_ln_b))
    merge_w = (bf(w_br_conv), bf(w_br_att), bf(w_br_ret))
    gate_b3 = gate_b.reshape(depth, N_BRANCH, d)
    wo = bf(w_out)
    mix_pre, mix_post, ret_g = rows(mix_pre_g), rows(mix_post_g), rows(ret_norm_g)

    xf = x.reshape(batch * seq, d)
    for l in range(depth):
        xf = _ffn(xf, *ffn1, layer=l)
        proj = _inproj(xf, mix_pre, w_in, layer=l)
        y_conv = _conv(proj, *conv_p, layer=l, batch=batch, seq=seq, ch=conv_ch)
        y_att = _moba(proj, att_cos, att_sin, batch=batch, seq=seq, heads=att_heads,
                      q_col=qa_col, k_col=ka_col, v_col=va_col)
        y_ret = _retention(proj, ret_cos, ret_sin, dm, qd, kd, cd, ret_g, layer=l,
                           batch=batch, seq=seq, heads=RET_HEADS, dk=ret_dk, dv=ret_dv,
                           q_col=qr_col, k_col=kr_col, v_col=vr_col, g_col=gr_col)
        xf = _merge(xf, y_conv, y_att, y_ret, *merge_w, proj, gate_b3, wo, mix_post,
                    layer=l, gates_col=gates_col)
        xf = _ffn(xf, *ffn2, layer=l)
    return xf.reshape(batch, seq, d)
```

```python
import functools
import math

import jax
import jax.numpy as jnp
from jax import lax
from jax.experimental import pallas as pl
from jax.experimental.pallas import tpu as pltpu

F32 = jnp.float32
BF16 = jnp.bfloat16

EPS = 1e-6
LANES = 128
V7X_VMEM_LIMIT_BYTES = 58 * 1024 * 1024
MASK_VALUE = -1e30
LOG2_E = math.log2(math.e)
BF16_SUBLANES = 16

CONV_WIDTH = 31
CONV_HALO = 32
ATT_HEAD_DIM = 128
ATT_ROPE_DIMS = ATT_HEAD_DIM // 4
ATT_ROPE_THETA = 500000.0
MOBA_BLOCK = 256
MOBA_TOPK = 3
RET_HEADS = 8
RET_CHUNK = 128
RET_ROT_THETA = 10000.0
N_BRANCH = 3


def _params(*semantics):
    return pltpu.CompilerParams(dimension_semantics=semantics,
                                vmem_limit_bytes=V7X_VMEM_LIMIT_BYTES)


def _layer_vec(layer, width):
    return pl.BlockSpec((None, 1, width), lambda *_: (layer, 0, 0))


def _rms(x, g):
    return x * lax.rsqrt(jnp.mean(x * x, axis=-1, keepdims=True) + EPS) * g


def _silu(x):
    return x * jax.nn.sigmoid(x)


def _dot(a, b):
    return jnp.dot(a, b, preferred_element_type=F32)


def _dot_nt(a, b, precision=None):
    return lax.dot_general(a, b, (((1,), (1,)), ((), ())), precision=precision,
                           preferred_element_type=F32)


def _rope(x, cos, sin_signed, half, group):
    n = x.shape[-1]
    lane = lax.broadcasted_iota(jnp.int32, x.shape, x.ndim - 1)
    first = (lane % group) < half
    partner = jnp.where(first, pltpu.roll(x, n - half, x.ndim - 1), pltpu.roll(x, half, x.ndim - 1))
    return x * cos + partner * sin_signed


def _ffn_body(x_ref, pre_ref, wg_ref, wu_ref, wd_ref, post_ref, o_ref, h_ref):
    j = pl.program_id(1)

    @pl.when(j == 0)
    def _():
        h_ref[...] = _rms(x_ref[...], pre_ref[...]).astype(BF16)
        o_ref[...] = jnp.zeros_like(o_ref)

    h = h_ref[...]
    g = _dot(h, wg_ref[...].astype(BF16))
    u = _dot(h, wu_ref[...].astype(BF16))
    a = (_silu(g) * u).astype(BF16)
    o_ref[...] += _dot(a, wd_ref[...].astype(BF16))

    @pl.when(j == pl.num_programs(1) - 1)
    def _():
        o_ref[...] = x_ref[...] + _rms(o_ref[...], 0.5 * post_ref[...])


def _ffn(x, pre_g, wg, wu, wd, post_g, *, layer, tm=1024, tf=256):
    m, d = x.shape
    f = wg.shape[2]
    return pl.pallas_call(
        _ffn_body,
        grid=(m // tm, f // tf),
        in_specs=[
            pl.BlockSpec((tm, d), lambda i, j: (i, 0)),
            _layer_vec(layer, d),
            pl.BlockSpec((None, d, tf), lambda i, j: (layer, 0, j)),
            pl.BlockSpec((None, d, tf), lambda i, j: (layer, 0, j)),
            pl.BlockSpec((None, tf, d), lambda i, j: (layer, j, 0)),
            _layer_vec(layer, d),
        ],
        out_specs=pl.BlockSpec((tm, d), lambda i, j: (i, 0)),
        out_shape=jax.ShapeDtypeStruct((m, d), F32),
        scratch_shapes=[pltpu.VMEM((tm, d), BF16)],
        compiler_params=_params("parallel", "arbitrary"),
        name="ffn",
    )(x, pre_g, wg, wu, wd, post_g)


def _inproj_body(x_ref, g_ref, w_ref, o_ref, h_ref):
    @pl.when(pl.program_id(1) == 0)
    def _():
        h_ref[...] = _rms(x_ref[...], g_ref[...]).astype(BF16)

    o_ref[...] = _dot(h_ref[...], w_ref[...].astype(BF16))


def _inproj(x, g, w, *, layer, tm, tn, x_buffers):
    m, d = x.shape
    n = w.shape[2]
    return pl.pallas_call(
        _inproj_body,
        grid=(m // tm, n // tn),
        in_specs=[
            pl.BlockSpec((tm, d), lambda i, j: (i, 0), pipeline_mode=pl.Buffered(x_buffers)),
            _layer_vec(layer, d),
            pl.BlockSpec((None, d, tn), lambda i, j: (layer, 0, j)),
        ],
        out_specs=pl.BlockSpec((tm, tn), lambda i, j: (i, j)),
        out_shape=jax.ShapeDtypeStruct((m, n), F32),
        scratch_shapes=[pltpu.VMEM((tm, d), BF16)],
        compiler_params=_params("parallel", "arbitrary"),
        name="inproj",
    )(x, g, w)


def _conv_body(a_ref, g_ref, ap_ref, gp_ref, w_ref, b_ref, lng_ref, lnb_ref, o_ref,
               u_ref, y_ref, *, ts, rows_per_acc):
    nck = u_ref.shape[0]
    u = a_ref[...] * jax.nn.sigmoid(g_ref[...])
    up = ap_ref[...] * jax.nn.sigmoid(gp_ref[...])
    up = jnp.where(pl.program_id(1) > 0, up, 0.0)
    for c in range(nck):
        u_ref[c, 0:CONV_HALO, :] = up[:, c * LANES:(c + 1) * LANES]
        u_ref[c, CONV_HALO:CONV_HALO + ts, :] = u[:, c * LANES:(c + 1) * LANES]

    first = CONV_HALO - (CONV_WIDTH - 1)

    def chunk(c, carry):
        for r in range(ts // rows_per_acc):
            r0 = r * rows_per_acc
            acc = jnp.zeros((rows_per_acc, LANES), F32)
            for w in range(CONV_WIDTH):
                acc = acc + u_ref[c, r0 + first + w:r0 + first + w + rows_per_acc, :] * w_ref[c, w:w + 1, :]
            y_ref[c, r0:r0 + rows_per_acc, :] = acc
        return carry

    lax.fori_loop(0, nck, chunk, 0)

    y = jnp.concatenate([y_ref[c] for c in range(nck)], axis=-1) + b_ref[...]
    mu = jnp.mean(y, axis=-1, keepdims=True)
    yc = y - mu
    yn = yc * lax.rsqrt(jnp.mean(yc * yc, axis=-1, keepdims=True) + EPS) * lng_ref[...] + lnb_ref[...]
    o_ref[...] = _silu(yn).astype(BF16)


def _conv(proj, w3, b, ln_g, ln_b, *, layer, batch, seq, ch, ts=512, rows_per_acc=128):
    m = proj.shape[0]
    nck = ch // LANES
    tiles = seq // ts
    halo_per_tile = ts // CONV_HALO

    def cur(col):
        return pl.BlockSpec((ts, ch), lambda bi, i: (bi * tiles + i, col))

    def prev(col):
        return pl.BlockSpec(
            (CONV_HALO, ch),
            lambda bi, i: (jnp.maximum((bi * tiles + i) * halo_per_tile - 1, 0), col))

    vec = _layer_vec(layer, ch)
    return pl.pallas_call(
        functools.partial(_conv_body, ts=ts, rows_per_acc=rows_per_acc),
        grid=(batch, tiles),
        in_specs=[cur(0), cur(1), prev(0), prev(1),
                  pl.BlockSpec((None, nck, CONV_WIDTH, LANES), lambda bi, i: (layer, 0, 0, 0)),
                  vec, vec, vec],
        out_specs=pl.BlockSpec((ts, ch), lambda bi, i: (bi * tiles + i, 0)),
        out_shape=jax.ShapeDtypeStruct((m, ch), BF16),
        scratch_shapes=[pltpu.VMEM((nck, CONV_HALO + ts, LANES), F32),
                        pltpu.VMEM((nck, ts, LANES), F32)],
        compiler_params=_params("parallel", "parallel"),
        name="conv",
    )(proj, proj, proj, proj, w3, b, ln_g, ln_b)


def _moba_body(q_ref, k_ref, v_ref, cos_ref, sin_ref, o_ref, kb_ref, vt_ref, km_ref, s_ref, m_ref,
               *, nb, n_sel, scale):
    blk = MOBA_BLOCK
    hd = ATT_HEAD_DIM
    half = ATT_ROPE_DIMS // 2

    for j in range(nb):
        rows = slice(j * blk, (j + 1) * blk)
        kj = _rope(k_ref[rows, :], cos_ref[rows, :], sin_ref[rows, :], half, ATT_HEAD_DIM)
        kb_ref[rows, :] = kj.astype(BF16)
        km_ref[j:j + 1, :] = jnp.mean(kj, axis=0, keepdims=True)
        vt_ref[0:hd, rows] = v_ref[rows, :].T.astype(BF16)
    vt_ref[hd:, :] = jnp.ones((vt_ref.shape[0] - hd, vt_ref.shape[1]), BF16)

    kidx = lax.broadcasted_iota(jnp.int32, (blk, blk), 0)
    qidx = lax.broadcasted_iota(jnp.int32, (blk, blk), 1)
    causal = kidx <= qidx

    def score_stage(i):
        rows = slice(i * blk, (i + 1) * blk)
        nk = (i + 1) * blk
        q = _rope(q_ref[rows, :], cos_ref[rows, :], sin_ref[rows, :], half, ATT_HEAD_DIM)
        s = _dot_nt(kb_ref[0:nk, :], (q * (scale * LOG2_E)).astype(BF16))
        parts = []
        if i > n_sel:
            gate = _dot_nt(km_ref[...], q, precision=lax.Precision.HIGHEST)
            brow = lax.broadcasted_iota(jnp.int32, (nb, blk), 0)
            rank = jnp.zeros((nb, blk), jnp.int32)
            for jp in range(i):
                gj = gate[jp:jp + 1, :]
                rank = rank + jnp.where((gj > gate) | ((gj == gate) & (jp < brow)), 1, 0)
            bias = jnp.where(rank < n_sel, 0.0, MASK_VALUE)
            for j in range(i):
                parts.append(s[j * blk:(j + 1) * blk, :] + bias[j:j + 1, :])
        elif i > 0:
            parts.append(s[0:i * blk, :])
        parts.append(jnp.where(causal, s[i * blk:nk, :], MASK_VALUE))
        s = jnp.concatenate(parts, axis=0) if len(parts) > 1 else parts[0]
        s_ref[i % 2, 0:nk, :] = s
        m_ref[i % 2] = jnp.max(s, axis=0, keepdims=True)

    def value_stage(i):
        rows = slice(i * blk, (i + 1) * blk)
        nk = (i + 1) * blk
        p = jnp.exp2(s_ref[i % 2, 0:nk, :] - m_ref[i % 2]).astype(BF16)
        acc = _dot(vt_ref[:, 0:nk], p)
        o_ref[rows, :] = (acc[0:hd, :] / acc[hd:hd + 1, :]).T.astype(BF16)

    always = pl.program_id(0) >= 0
    for step in range(nb + 1):
        @pl.when(always)
        def _():
            if step < nb:
                score_stage(step)
            if step > 0:
                value_stage(step - 1)


def _moba(proj, cos_t, sin_t, *, batch, seq, heads, q_col, k_col, v_col):
    m = proj.shape[0]
    hd = ATT_HEAD_DIM
    nb = seq // MOBA_BLOCK
    n_sel = min(MOBA_TOPK, nb - 1)
    table = pl.BlockSpec((seq, hd), lambda b, h: (0, 0))

    def head(col):
        return pl.BlockSpec((seq, hd), lambda b, h: (b, col // hd + h))

    return pl.pallas_call(
        functools.partial(_moba_body, nb=nb, n_sel=n_sel, scale=hd ** -0.5),
        grid=(batch, heads),
        in_specs=[head(q_col), head(k_col), head(v_col), table, table],
        out_specs=pl.BlockSpec((seq, hd), lambda b, h: (b, h)),
        out_shape=jax.ShapeDtypeStruct((m, heads * hd), BF16),
        scratch_shapes=[
            pltpu.VMEM((seq, hd), BF16),
            pltpu.VMEM((hd + BF16_SUBLANES, seq), BF16),
            pltpu.VMEM((nb, hd), F32),
            pltpu.VMEM((2, seq, MOBA_BLOCK), F32),
            pltpu.VMEM((2, 1, MOBA_BLOCK), F32),
        ],
        compiler_params=_params("parallel", "parallel"),
        name="moba",
    )(proj, proj, proj, cos_t, sin_t)


def _ret_body(q_ref, k_ref, v_ref, gr_ref, cos_ref, sin_ref, dm_ref, qd_ref, kd_ref, cd_ref,
              ng_ref, o_ref, st_ref, *, heads, dk, dv, chunks):
    c = RET_CHUNK
    per_group = LANES // dk

    @pl.when(pl.program_id(1) == 0)
    def _():
        st_ref[...] = jnp.zeros_like(st_ref)

    for cc in range(chunks):
        rows = slice(cc * c, (cc + 1) * c)
        cos = cos_ref[rows, :]
        sin = sin_ref[rows, :]
        for hg in range(heads // per_group):
            lanes = slice(hg * LANES, (hg + 1) * LANES)
            qr = _rope(q_ref[rows, lanes], cos, sin, dk // 2, dk)
            kr = _rope(k_ref[rows, lanes], cos, sin, dk // 2, dk) * (dk ** -0.5)
            krt = kr.T
            for hh in range(per_group):
                h = hg * per_group + hh
                hl = slice(hh * dk, (hh + 1) * dk)
                vl = slice(h * dv, (h + 1) * dv)
                qh = qr[:, hl]
                vb = v_ref[rows, vl].astype(BF16)
                inner = _dot_nt(qh.astype(BF16), kr[:, hl].astype(BF16)) * dm_ref[h]
                ro = _dot(inner.astype(BF16), vb)
                st = st_ref[h]
                ro = ro + _dot((qh * qd_ref[h]).astype(BF16), st.astype(BF16))
                kv = _dot((krt[hl, :] * kd_ref[h]).astype(BF16), vb)
                st_ref[h] = st * cd_ref[h] + kv
                ro = ro * lax.rsqrt(jnp.mean(ro * ro, axis=-1, keepdims=True) + EPS) * ng_ref[:, vl]
                o_ref[rows, vl] = (_silu(gr_ref[rows, vl]) * ro).astype(BF16)


def _retention(proj, cos_t, sin_t, dm, qd, kd, cd, norm_g, *, layer, batch, seq, heads, dk, dv,
               q_col, k_col, v_col, g_col, chunks=4):
    m = proj.shape[0]
    ts = chunks * RET_CHUNK
    tiles = seq // ts
    qk_w = heads * dk
    v_w = heads * dv

    def tok(width, col):
        return pl.BlockSpec((ts, width), lambda b, t: (b * tiles + t, col // width))

    def const(shape):
        return pl.BlockSpec(shape, lambda b, t: (0,) * len(shape))

    return pl.pallas_call(
        functools.partial(_ret_body, heads=heads, dk=dk, dv=dv, chunks=chunks),
        grid=(batch, tiles),
        in_specs=[
            tok(qk_w, q_col), tok(qk_w, k_col), tok(v_w, v_col), tok(v_w, g_col),
            pl.BlockSpec((ts, LANES), lambda b, t: (t, 0)),
            pl.BlockSpec((ts, LANES), lambda b, t: (t, 0)),
            const(dm.shape), const(qd.shape), const(kd.shape), const(cd.shape),
            _layer_vec(layer, v_w),
        ],
        out_specs=pl.BlockSpec((ts, v_w), lambda b, t: (b * tiles + t, 0)),
        out_shape=jax.ShapeDtypeStruct((m, v_w), BF16),
        scratch_shapes=[pltpu.VMEM((heads, dk, dv), F32)],
        compiler_params=_params("parallel", "arbitrary"),
        name="retention",
    )(proj, proj, proj, proj, cos_t, sin_t, dm, qd, kd, cd, norm_g)


def _merge_body(x_ref, yc_ref, ya_ref, yr_ref, wc_ref, wa_ref, wr_ref, g0_ref, g1_ref, g2_ref,
                gb_ref, wo_ref, post_ref, o_ref, mg_ref, *, tn):
    d = o_ref.shape[1]
    for t in range(d // tn):
        cols = slice(t * tn, (t + 1) * tn)
        merged = (jax.nn.sigmoid(g0_ref[:, cols] + gb_ref[0:1, cols]) * _dot(yc_ref[...], wc_ref[:, cols])
                  + jax.nn.sigmoid(g1_ref[:, cols] + gb_ref[1:2, cols]) * _dot(ya_ref[...], wa_ref[:, cols])
                  + jax.nn.sigmoid(g2_ref[:, cols] + gb_ref[2:3, cols]) * _dot(yr_ref[...], wr_ref[:, cols]))
        mg_ref[:, cols] = merged.astype(BF16)
    o_ref[...] = x_ref[...] + _rms(_dot(mg_ref[...], wo_ref[...]), post_ref[...])


def _merge(x, yc, ya, yr, wc, wa, wr, proj, gate_b, wo, post_g, *, layer, gates_col, tm=256, tn=512):
    m, d = x.shape

    def ytile(a):
        return pl.BlockSpec((tm, a.shape[1]), lambda i: (i, 0))

    def resident(w):
        return pl.BlockSpec((None,) + w.shape[1:], lambda i: (layer, 0, 0),
                            pipeline_mode=pl.Buffered(1))

    def gtile(br):
        return pl.BlockSpec((tm, d), lambda i: (i, gates_col // d + br))

    return pl.pallas_call(
        functools.partial(_merge_body, tn=tn),
        grid=(m // tm,),
        in_specs=[
            pl.BlockSpec((tm, d), lambda i: (i, 0)),
            ytile(yc), ytile(ya), ytile(yr), resident(wc), resident(wa), resident(wr),
            gtile(0), gtile(1), gtile(2),
            pl.BlockSpec((None, N_BRANCH, d), lambda i: (layer, 0, 0)),
            resident(wo),
            _layer_vec(layer, d),
        ],
        out_specs=pl.BlockSpec((tm, d), lambda i: (i, 0)),
        out_shape=jax.ShapeDtypeStruct((m, d), F32),
        scratch_shapes=[pltpu.VMEM((tm, d), BF16)],
        compiler_params=_params("parallel"),
        name="merge",
    )(x, yc, ya, yr, wc, wa, wr, proj, proj, proj, gate_b, wo, post_g)


def _rope_tables(seq, n_rot, theta, group):
    half = n_rot // 2
    inv = 1.0 / (theta ** (jnp.arange(half, dtype=F32) / half))
    ang = jnp.arange(seq, dtype=jnp.int32).astype(F32)[:, None] * inv[None, :]
    cos, sin = jnp.cos(ang), jnp.sin(ang)
    rest = group - n_rot
    cos_g = jnp.concatenate([cos, cos, jnp.ones((seq, rest), F32)], axis=-1)
    sin_g = jnp.concatenate([-sin, sin, jnp.zeros((seq, rest), F32)], axis=-1)
    reps = LANES // group
    return jnp.tile(cos_g, (1, reps)), jnp.tile(sin_g, (1, reps))


def _retention_constants(heads):
    c = RET_CHUNK
    log_g = jnp.log1p(-(2.0 ** (-5.0 - jnp.arange(heads, dtype=F32))))
    idx = jnp.arange(c, dtype=F32)
    diff = idx[:, None] - idx[None, :]
    decay_mask = jnp.exp(jnp.where(diff >= 0, log_g[:, None, None] * diff, -jnp.inf))
    q_decay = jnp.exp(log_g[:, None] * (idx + 1.0))[:, :, None]
    k_decay = jnp.exp(log_g[:, None] * (c - 1.0 - idx))[:, None, :]
    chunk_decay = jnp.broadcast_to(jnp.exp(log_g * c)[:, None, None], (heads, 1, LANES))
    return decay_mask, q_decay, k_decay, chunk_decay


def kernel(x, ffn1_pre_g, ffn1_w_gate, ffn1_w_up, ffn1_w_down, ffn1_post_g, mix_pre_g, w_in, conv_dw_w, conv_dw_b, conv_ln_g, conv_ln_b, ret_norm_g, w_br_conv, w_br_att, w_br_ret, gate_b, w_out, mix_post_g, ffn2_pre_g, ffn2_w_gate, ffn2_w_up, ffn2_w_down, ffn2_post_g):
    batch, seq, d = x.shape
    depth = w_in.shape[0]
    conv_ch = conv_dw_w.shape[2]
    att_w = w_br_att.shape[1]
    ret_v_w = w_br_ret.shape[1]
    att_heads = att_w // ATT_HEAD_DIM
    ret_dv = ret_v_w // RET_HEADS
    ret_dk = ret_dv // 2
    ret_qk_w = RET_HEADS * ret_dk
    sizes = [conv_ch, conv_ch, att_w, att_w, att_w, ret_qk_w, ret_qk_w, ret_v_w, ret_v_w, N_BRANCH * d]
    assert sum(sizes) == w_in.shape[2]
    cols = [0]
    for s in sizes[:-1]:
        cols.append(cols[-1] + s)
    (_, _, qa_col, ka_col, va_col, qr_col, kr_col, vr_col, gr_col, gates_col) = cols

    att_cos, att_sin = _rope_tables(seq, ATT_ROPE_DIMS, ATT_ROPE_THETA, ATT_HEAD_DIM)
    ret_cos, ret_sin = _rope_tables(seq, ret_dk, RET_ROT_THETA, ret_dk)
    dm, qd, kd, cd = _retention_constants(RET_HEADS)

    bf = lambda w: w.astype(BF16)
    rows = lambda v: v.reshape(depth, 1, -1)
    ffn1 = (rows(ffn1_pre_g), ffn1_w_gate, ffn1_w_up, ffn1_w_down, rows(ffn1_post_g))
    ffn2 = (rows(ffn2_pre_g), ffn2_w_gate, ffn2_w_up, ffn2_w_down, rows(ffn2_post_g))
    conv_w = conv_dw_w.reshape(depth, CONV_WIDTH, conv_ch // LANES, LANES).transpose(0, 2, 1, 3)
    conv_p = (conv_w, rows(conv_dw_b), rows(conv_ln_g), rows(conv_ln_b))
    merge_w = (bf(w_br_conv), bf(w_br_att), bf(w_br_ret))
    gate_b3 = gate_b.reshape(depth, N_BRANCH, d)
    wo = bf(w_out)
    mix_pre, mix_post, ret_g = rows(mix_pre_g), rows(mix_post_g), rows(ret_norm_g)

    xf = x.reshape(batch * seq, d)
    for l in range(depth):
        xf = _ffn(xf, *ffn1, layer=l)
        proj = _inproj(xf, mix_pre, w_in, layer=l, tm=(2048, 1024)[l], tn=512, x_buffers=(1, 2)[l])
        y_conv = _conv(proj, *conv_p, layer=l, batch=batch, seq=seq, ch=conv_ch)
        y_att = _moba(proj, att_cos, att_sin, batch=batch, seq=seq, heads=att_heads,
                      q_col=qa_col, k_col=ka_col, v_col=va_col)
        y_ret = _retention(proj, ret_cos, ret_sin, dm, qd, kd, cd, ret_g, layer=l,
                           batch=batch, seq=seq, heads=RET_HEADS, dk=ret_dk, dv=ret_dv,
                           q_col=qr_col, k_col=kr_col, v_col=vr_col, g_col=gr_col)
        xf = _merge(xf, y_conv, y_att, y_ret, *merge_w, proj, gate_b3, wo, mix_post,
                    layer=l, gates_col=gates_col)
        xf = _ffn(xf, *ffn2, layer=l)
    return xf.reshape(batch, seq, d)
```

```python
import functools
import math

import jax
import jax.numpy as jnp
from jax import lax
from jax.experimental import pallas as pl
from jax.experimental.pallas import tpu as pltpu

F32 = jnp.float32
BF16 = jnp.bfloat16

EPS = 1e-6
LANES = 128
V7X_VMEM_LIMIT_BYTES = 58 * 1024 * 1024
MASK_VALUE = -1e30
LOG2_E = math.log2(math.e)
BF16_SUBLANES = 16

CONV_WIDTH = 31
CONV_HALO = 32
ATT_HEAD_DIM = 128
ATT_ROPE_DIMS = ATT_HEAD_DIM // 4
ATT_ROPE_THETA = 500000.0
MOBA_BLOCK = 256
MOBA_TOPK = 3
RET_HEADS = 8
RET_CHUNK = 128
RET_ROT_THETA = 10000.0
N_BRANCH = 3


def _params(*semantics):
    return pltpu.CompilerParams(dimension_semantics=semantics,
                                vmem_limit_bytes=V7X_VMEM_LIMIT_BYTES)


def _layer_vec(layer, width):
    return pl.BlockSpec((None, 1, width), lambda *_: (layer, 0, 0))


def _rms(x, g):
    return x * lax.rsqrt(jnp.mean(x * x, axis=-1, keepdims=True) + EPS) * g


def _silu(x):
    return x * jax.nn.sigmoid(x)


def _dot(a, b):
    return jnp.dot(a, b, preferred_element_type=F32)


def _dot_nt(a, b, precision=None):
    return lax.dot_general(a, b, (((1,), (1,)), ((), ())), precision=precision,
                           preferred_element_type=F32)


def _rope(x, cos, sin_signed, half, group):
    n = x.shape[-1]
    lane = lax.broadcasted_iota(jnp.int32, x.shape, x.ndim - 1)
    first = (lane % group) < half
    partner = jnp.where(first, pltpu.roll(x, n - half, x.ndim - 1), pltpu.roll(x, half, x.ndim - 1))
    return x * cos + partner * sin_signed


def _ffn_body(x_ref, pre_ref, wg_ref, wu_ref, wd_ref, post_ref, o_ref, h_ref):
    j = pl.program_id(1)

    @pl.when(j == 0)
    def _():
        h_ref[...] = _rms(x_ref[...], pre_ref[...]).astype(BF16)
        o_ref[...] = jnp.zeros_like(o_ref)

    h = h_ref[...]
    g = _dot(h, wg_ref[...].astype(BF16))
    u = _dot(h, wu_ref[...].astype(BF16))
    a = (_silu(g) * u).astype(BF16)
    o_ref[...] += _dot(a, wd_ref[...].astype(BF16))

    @pl.when(j == pl.num_programs(1) - 1)
    def _():
        o_ref[...] = x_ref[...] + _rms(o_ref[...], 0.5 * post_ref[...])


def _ffn(x, pre_g, wg, wu, wd, post_g, *, layer, tm=1024, tf=256):
    m, d = x.shape
    f = wg.shape[2]
    return pl.pallas_call(
        _ffn_body,
        grid=(m // tm, f // tf),
        in_specs=[
            pl.BlockSpec((tm, d), lambda i, j: (i, 0)),
            _layer_vec(layer, d),
            pl.BlockSpec((None, d, tf), lambda i, j: (layer, 0, j)),
            pl.BlockSpec((None, d, tf), lambda i, j: (layer, 0, j)),
            pl.BlockSpec((None, tf, d), lambda i, j: (layer, j, 0)),
            _layer_vec(layer, d),
        ],
        out_specs=pl.BlockSpec((tm, d), lambda i, j: (i, 0)),
        out_shape=jax.ShapeDtypeStruct((m, d), F32),
        scratch_shapes=[pltpu.VMEM((tm, d), BF16)],
        compiler_params=_params("parallel", "arbitrary"),
        name="ffn",
    )(x, pre_g, wg, wu, wd, post_g)


def _inproj_body(x_ref, g_ref, w_ref, o_ref, h_ref):
    @pl.when(pl.program_id(1) == 0)
    def _():
        h_ref[...] = _rms(x_ref[...], g_ref[...]).astype(BF16)

    o_ref[...] = _dot(h_ref[...], w_ref[...].astype(BF16))


def _inproj(x, g, w, *, layer, tm=2048, tn=512):
    m, d = x.shape
    n = w.shape[2]
    return pl.pallas_call(
        _inproj_body,
        grid=(m // tm, n // tn),
        in_specs=[
            pl.BlockSpec((tm, d), lambda i, j: (i, 0), pipeline_mode=pl.Buffered(1)),
            _layer_vec(layer, d),
            pl.BlockSpec((None, d, tn), lambda i, j: (layer, 0, j)),
        ],
        out_specs=pl.BlockSpec((tm, tn), lambda i, j: (i, j)),
        out_shape=jax.ShapeDtypeStruct((m, n), F32),
        scratch_shapes=[pltpu.VMEM((tm, d), BF16)],
        compiler_params=_params("parallel", "arbitrary"),
        name="inproj",
    )(x, g, w)


def _conv_body(a_ref, g_ref, ap_ref, gp_ref, w_ref, b_ref, lng_ref, lnb_ref, o_ref,
               u_ref, y_ref, *, ts, rows_per_acc):
    nck = u_ref.shape[0]
    u = a_ref[...] * jax.nn.sigmoid(g_ref[...])
    up = ap_ref[...] * jax.nn.sigmoid(gp_ref[...])
    up = jnp.where(pl.program_id(1) > 0, up, 0.0)
    for c in range(nck):
        u_ref[c, 0:CONV_HALO, :] = up[:, c * LANES:(c + 1) * LANES]
        u_ref[c, CONV_HALO:CONV_HALO + ts, :] = u[:, c * LANES:(c + 1) * LANES]

    first = CONV_HALO - (CONV_WIDTH - 1)

    def chunk(c, carry):
        for r in range(ts // rows_per_acc):
            r0 = r * rows_per_acc
            acc = jnp.zeros((rows_per_acc, LANES), F32)
            for w in range(CONV_WIDTH):
                acc = acc + u_ref[c, r0 + first + w:r0 + first + w + rows_per_acc, :] * w_ref[c, w:w + 1, :]
            y_ref[c, r0:r0 + rows_per_acc, :] = acc
        return carry

    lax.fori_loop(0, nck, chunk, 0)

    y = jnp.concatenate([y_ref[c] for c in range(nck)], axis=-1) + b_ref[...]
    mu = jnp.mean(y, axis=-1, keepdims=True)
    yc = y - mu
    yn = yc * lax.rsqrt(jnp.mean(yc * yc, axis=-1, keepdims=True) + EPS) * lng_ref[...] + lnb_ref[...]
    o_ref[...] = _silu(yn).astype(BF16)


def _conv(proj, w3, b, ln_g, ln_b, *, layer, batch, seq, ch, ts=512, rows_per_acc=128):
    m = proj.shape[0]
    nck = ch // LANES
    tiles = seq // ts
    halo_per_tile = ts // CONV_HALO

    def cur(col):
        return pl.BlockSpec((ts, ch), lambda bi, i: (bi * tiles + i, col))

    def prev(col):
        return pl.BlockSpec(
            (CONV_HALO, ch),
            lambda bi, i: (jnp.maximum((bi * tiles + i) * halo_per_tile - 1, 0), col))

    vec = _layer_vec(layer, ch)
    return pl.pallas_call(
        functools.partial(_conv_body, ts=ts, rows_per_acc=rows_per_acc),
        grid=(batch, tiles),
        in_specs=[cur(0), cur(1), prev(0), prev(1),
                  pl.BlockSpec((None, nck, CONV_WIDTH, LANES), lambda bi, i: (layer, 0, 0, 0)),
                  vec, vec, vec],
        out_specs=pl.BlockSpec((ts, ch), lambda bi, i: (bi * tiles + i, 0)),
        out_shape=jax.ShapeDtypeStruct((m, ch), BF16),
        scratch_shapes=[pltpu.VMEM((nck, CONV_HALO + ts, LANES), F32),
                        pltpu.VMEM((nck, ts, LANES), F32)],
        compiler_params=_params("parallel", "parallel"),
        name="conv",
    )(proj, proj, proj, proj, w3, b, ln_g, ln_b)


def _moba_body(q_ref, k_ref, v_ref, cos_ref, sin_ref, o_ref, kb_ref, vt_ref, km_ref, s_ref, m_ref,
               *, nb, n_sel, scale):
    blk = MOBA_BLOCK
    hd = ATT_HEAD_DIM
    half = ATT_ROPE_DIMS // 2

    def prepare_block(j):
        rows = slice(j * blk, (j + 1) * blk)
        kj = _rope(k_ref[rows, :], cos_ref[rows, :], sin_ref[rows, :], half, ATT_HEAD_DIM)
        kb_ref[rows, :] = kj.astype(BF16)
        km_ref[j:j + 1, :] = jnp.mean(kj, axis=0, keepdims=True)
        vt_ref[0:hd, rows] = v_ref[rows, :].T.astype(BF16)
        vt_ref[hd:, rows] = jnp.ones((vt_ref.shape[0] - hd, blk), BF16)

    kidx = lax.broadcasted_iota(jnp.int32, (blk, blk), 0)
    qidx = lax.broadcasted_iota(jnp.int32, (blk, blk), 1)
    causal = kidx <= qidx

    def score_stage(i):
        rows = slice(i * blk, (i + 1) * blk)
        nk = (i + 1) * blk
        q = _rope(q_ref[rows, :], cos_ref[rows, :], sin_ref[rows, :], half, ATT_HEAD_DIM)
        s = _dot_nt(kb_ref[0:nk, :], (q * (scale * LOG2_E)).astype(BF16))
        parts = []
        if i > n_sel:
            gate = _dot_nt(km_ref[...], q, precision=lax.Precision.HIGHEST)
            brow = lax.broadcasted_iota(jnp.int32, (nb, blk), 0)
            rank = jnp.zeros((nb, blk), jnp.int32)
            for jp in range(i):
                gj = gate[jp:jp + 1, :]
                rank = rank + jnp.where((gj > gate) | ((gj == gate) & (jp < brow)), 1, 0)
            bias = jnp.where(rank < n_sel, 0.0, MASK_VALUE)
            for j in range(i):
                parts.append(s[j * blk:(j + 1) * blk, :] + bias[j:j + 1, :])
        elif i > 0:
            parts.append(s[0:i * blk, :])
        parts.append(jnp.where(causal, s[i * blk:nk, :], MASK_VALUE))
        s = jnp.concatenate(parts, axis=0) if len(parts) > 1 else parts[0]
        s_ref[i % 2, 0:nk, :] = s
        m_ref[i % 2] = jnp.max(s, axis=0, keepdims=True)

    def value_stage(i):
        rows = slice(i * blk, (i + 1) * blk)
        nk = (i + 1) * blk
        p = jnp.exp2(s_ref[i % 2, 0:nk, :] - m_ref[i % 2]).astype(BF16)
        acc = _dot(vt_ref[:, 0:nk], p)
        o_ref[rows, :] = (acc[0:hd, :] / acc[hd:hd + 1, :]).T.astype(BF16)

    always = pl.program_id(0) >= 0
    for step in range(nb + 1):
        @pl.when(always)
        def _():
            if step < nb:
                prepare_block(step)
                score_stage(step)
            if step > 0:
                value_stage(step - 1)


def _moba(proj, cos_t, sin_t, *, batch, seq, heads, q_col, k_col, v_col):
    m = proj.shape[0]
    hd = ATT_HEAD_DIM
    nb = seq // MOBA_BLOCK
    n_sel = min(MOBA_TOPK, nb - 1)
    table = pl.BlockSpec((seq, hd), lambda b, h: (0, 0))

    def head(col):
        return pl.BlockSpec((seq, hd), lambda b, h: (b, col // hd + h))

    return pl.pallas_call(
        functools.partial(_moba_body, nb=nb, n_sel=n_sel, scale=hd ** -0.5),
        grid=(batch, heads),
        in_specs=[head(q_col), head(k_col), head(v_col), table, table],
        out_specs=pl.BlockSpec((seq, hd), lambda b, h: (b, h)),
        out_shape=jax.ShapeDtypeStruct((m, heads * hd), BF16),
        scratch_shapes=[
            pltpu.VMEM((seq, hd), BF16),
            pltpu.VMEM((hd + BF16_SUBLANES, seq), BF16),
            pltpu.VMEM((nb, hd), F32),
            pltpu.VMEM((2, seq, MOBA_BLOCK), F32),
            pltpu.VMEM((2, 1, MOBA_BLOCK), F32),
        ],
        compiler_params=_params("parallel", "parallel"),
        name="moba",
    )(proj, proj, proj, cos_t, sin_t)


def _ret_body(q_ref, k_ref, v_ref, gr_ref, cos_ref, sin_ref, dm_ref, qd_ref, kd_ref, cd_ref,
              ng_ref, o_ref, st_ref, *, heads, dk, dv, chunks):
    c = RET_CHUNK
    per_group = LANES // dk

    @pl.when(pl.program_id(1) == 0)
    def _():
        st_ref[...] = jnp.zeros_like(st_ref)

    for cc in range(chunks):
        rows = slice(cc * c, (cc + 1) * c)
        cos = cos_ref[rows, :]
        sin = sin_ref[rows, :]
        for hg in range(heads // per_group):
            lanes = slice(hg * LANES, (hg + 1) * LANES)
            qr = _rope(q_ref[rows, lanes], cos, sin, dk // 2, dk)
            kr = _rope(k_ref[rows, lanes], cos, sin, dk // 2, dk) * (dk ** -0.5)
            krt = kr.T
            for hh in range(per_group):
                h = hg * per_group + hh
                hl = slice(hh * dk, (hh + 1) * dk)
                vl = slice(h * dv, (h + 1) * dv)
                qh = qr[:, hl]
                vb = v_ref[rows, vl].astype(BF16)
                inner = _dot_nt(qh.astype(BF16), kr[:, hl].astype(BF16)) * dm_ref[h]
                ro = _dot(inner.astype(BF16), vb)
                st = st_ref[h]
                ro = ro + _dot((qh * qd_ref[h]).astype(BF16), st.astype(BF16))
                kv = _dot((krt[hl, :] * kd_ref[h]).astype(BF16), vb)
                st_ref[h] = st * cd_ref[h] + kv
                ro = ro * lax.rsqrt(jnp.mean(ro * ro, axis=-1, keepdims=True) + EPS) * ng_ref[:, vl]
                o_ref[rows, vl] = (_silu(gr_ref[rows, vl]) * ro).astype(BF16)


def _retention(proj, cos_t, sin_t, dm, qd, kd, cd, norm_g, *, layer, batch, seq, heads, dk, dv,
               q_col, k_col, v_col, g_col, chunks=4):
    m = proj.shape[0]
    ts = chunks * RET_CHUNK
    tiles = seq // ts
    qk_w = heads * dk
    v_w = heads * dv

    def tok(width, col):
        return pl.BlockSpec((ts, width), lambda b, t: (b * tiles + t, col // width))

    def const(shape):
        return pl.BlockSpec(shape, lambda b, t: (0,) * len(shape))

    return pl.pallas_call(
        functools.partial(_ret_body, heads=heads, dk=dk, dv=dv, chunks=chunks),
        grid=(batch, tiles),
        in_specs=[
            tok(qk_w, q_col), tok(qk_w, k_col), tok(v_w, v_col), tok(v_w, g_col),
            pl.BlockSpec((ts, LANES), lambda b, t: (t, 0)),
            pl.BlockSpec((ts, LANES), lambda b, t: (t, 0)),
            const(dm.shape), const(qd.shape), const(kd.shape), const(cd.shape),
            _layer_vec(layer, v_w),
        ],
        out_specs=pl.BlockSpec((ts, v_w), lambda b, t: (b * tiles + t, 0)),
        out_shape=jax.ShapeDtypeStruct((m, v_w), BF16),
        scratch_shapes=[pltpu.VMEM((heads, dk, dv), F32)],
        compiler_params=_params("parallel", "arbitrary"),
        name="retention",
    )(proj, proj, proj, proj, cos_t, sin_t, dm, qd, kd, cd, norm_g)


def _merge_body(x_ref, yc_ref, ya_ref, yr_ref, wc_ref, wa_ref, wr_ref, g0_ref, g1_ref, g2_ref,
                gb_ref, wo_ref, post_ref, o_ref, mg_ref, *, tn):
    d = o_ref.shape[1]
    for t in range(d // tn):
        cols = slice(t * tn, (t + 1) * tn)
        merged = (jax.nn.sigmoid(g0_ref[:, cols] + gb_ref[0:1, cols]) * _dot(yc_ref[...], wc_ref[:, cols])
                  + jax.nn.sigmoid(g1_ref[:, cols] + gb_ref[1:2, cols]) * _dot(ya_ref[...], wa_ref[:, cols])
                  + jax.nn.sigmoid(g2_ref[:, cols] + gb_ref[2:3, cols]) * _dot(yr_ref[...], wr_ref[:, cols]))
        mg_ref[:, cols] = merged.astype(BF16)
    o_ref[...] = x_ref[...] + _rms(_dot(mg_ref[...], wo_ref[...]), post_ref[...])


def _merge(x, yc, ya, yr, wc, wa, wr, proj, gate_b, wo, post_g, *, layer, gates_col, tm=256, tn=512):
    m, d = x.shape

    def ytile(a):
        return pl.BlockSpec((tm, a.shape[1]), lambda i: (i, 0))

    def resident(w):
        return pl.BlockSpec((None,) + w.shape[1:], lambda i: (layer, 0, 0),
                            pipeline_mode=pl.Buffered(1))

    def gtile(br):
        return pl.BlockSpec((tm, d), lambda i: (i, gates_col // d + br))

    return pl.pallas_call(
        functools.partial(_merge_body, tn=tn),
        grid=(m // tm,),
        in_specs=[
            pl.BlockSpec((tm, d), lambda i: (i, 0)),
            ytile(yc), ytile(ya), ytile(yr), resident(wc), resident(wa), resident(wr),
            gtile(0), gtile(1), gtile(2),
            pl.BlockSpec((None, N_BRANCH, d), lambda i: (layer, 0, 0)),
            resident(wo),
            _layer_vec(layer, d),
        ],
        out_specs=pl.BlockSpec((tm, d), lambda i: (i, 0)),
        out_shape=jax.ShapeDtypeStruct((m, d), F32),
        scratch_shapes=[pltpu.VMEM((tm, d), BF16)],
        compiler_params=_params("parallel"),
        name="merge",
    )(x, yc, ya, yr, wc, wa, wr, proj, proj, proj, gate_b, wo, post_g)


def _rope_tables(seq, n_rot, theta, group):
    half = n_rot // 2
    inv = 1.0 / (theta ** (jnp.arange(half, dtype=F32) / half))
    ang = jnp.arange(seq, dtype=jnp.int32).astype(F32)[:, None] * inv[None, :]
    cos, sin = jnp.cos(ang), jnp.sin(ang)
    rest = group - n_rot
    cos_g = jnp.concatenate([cos, cos, jnp.ones((seq, rest), F32)], axis=-1)
    sin_g = jnp.concatenate([-sin, sin, jnp.zeros((seq, rest), F32)], axis=-1)
    reps = LANES // group
    return jnp.tile(cos_g, (1, reps)), jnp.tile(sin_g, (1, reps))


def _retention_constants(heads):
    c = RET_CHUNK
    log_g = jnp.log1p(-(2.0 ** (-5.0 - jnp.arange(heads, dtype=F32))))
    idx = jnp.arange(c, dtype=F32)
    diff = idx[:, None] - idx[None, :]
    decay_mask = jnp.exp(jnp.where(diff >= 0, log_g[:, None, None] * diff, -jnp.inf))
    q_decay = jnp.exp(log_g[:, None] * (idx + 1.0))[:, :, None]
    k_decay = jnp.exp(log_g[:, None] * (c - 1.0 - idx))[:, None, :]
    chunk_decay = jnp.broadcast_to(jnp.exp(log_g * c)[:, None, None], (heads, 1, LANES))
    return decay_mask, q_decay, k_decay, chunk_decay


def kernel(x, ffn1_pre_g, ffn1_w_gate, ffn1_w_up, ffn1_w_down, ffn1_post_g, mix_pre_g, w_in, conv_dw_w, conv_dw_b, conv_ln_g, conv_ln_b, ret_norm_g, w_br_conv, w_br_att, w_br_ret, gate_b, w_out, mix_post_g, ffn2_pre_g, ffn2_w_gate, ffn2_w_up, ffn2_w_down, ffn2_post_g):
    batch, seq, d = x.shape
    depth = w_in.shape[0]
    conv_ch = conv_dw_w.shape[2]
    att_w = w_br_att.shape[1]
    ret_v_w = w_br_ret.shape[1]
    att_heads = att_w // ATT_HEAD_DIM
    ret_dv = ret_v_w // RET_HEADS
    ret_dk = ret_dv // 2
    ret_qk_w = RET_HEADS * ret_dk
    sizes = [conv_ch, conv_ch, att_w, att_w, att_w, ret_qk_w, ret_qk_w, ret_v_w, ret_v_w, N_BRANCH * d]
    assert sum(sizes) == w_in.shape[2]
    cols = [0]
    for s in sizes[:-1]:
        cols.append(cols[-1] + s)
    (_, _, qa_col, ka_col, va_col, qr_col, kr_col, vr_col, gr_col, gates_col) = cols

    att_cos, att_sin = _rope_tables(seq, ATT_ROPE_DIMS, ATT_ROPE_THETA, ATT_HEAD_DIM)
    ret_cos, ret_sin = _rope_tables(seq, ret_dk, RET_ROT_THETA, ret_dk)
    dm, qd, kd, cd = _retention_constants(RET_HEADS)

    bf = lambda w: w.astype(BF16)
    rows = lambda v: v.reshape(depth, 1, -1)
    ffn1 = (rows(ffn1_pre_g), ffn1_w_gate, ffn1_w_up, ffn1_w_down, rows(ffn1_post_g))
    ffn2 = (rows(ffn2_pre_g), ffn2_w_gate, ffn2_w_up, ffn2_w_down, rows(ffn2_post_g))
    conv_w = conv_dw_w.reshape(depth, CONV_WIDTH, conv_ch // LANES, LANES).transpose(0, 2, 1, 3)
    conv_p = (conv_w, rows(conv_dw_b), rows(conv_ln_g), rows(conv_ln_b))
    merge_w = (bf(w_br_conv), bf(w_br_att), bf(w_br_ret))
    gate_b3 = gate_b.reshape(depth, N_BRANCH, d)
    wo = bf(w_out)
    mix_pre, mix_post, ret_g = rows(mix_pre_g), rows(mix_post_g), rows(ret_norm_g)

    xf = x.reshape(batch * seq, d)
    for l in range(depth):
        xf = _ffn(xf, *ffn1, layer=l)
        proj = _inproj(xf, mix_pre, w_in, layer=l)
        y_conv = _conv(proj, *conv_p, layer=l, batch=batch, seq=seq, ch=conv_ch)
        y_att = _moba(proj, att_cos, att_sin, batch=batch, seq=seq, heads=att_heads,
                      q_col=qa_col, k_col=ka_col, v_col=va_col)
        y_ret = _retention(proj, ret_cos, ret_sin, dm, qd, kd, cd, ret_g, layer=l,
                           batch=batch, seq=seq, heads=RET_HEADS, dk=ret_dk, dv=ret_dv,
                           q_col=qr_col, k_col=kr_col, v_col=vr_col, g_col=gr_col)
        xf = _merge(xf, y_conv, y_att, y_ret, *merge_w, proj, gate_b3, wo, mix_post,
                    layer=l, gates_col=gates_col)
        xf = _ffn(xf, *ffn2, layer=l)
    return xf.reshape(batch, seq, d)
```

```python
import functools
import math

import jax
import jax.numpy as jnp
from jax import lax
from jax.experimental import pallas as pl
from jax.experimental.pallas import tpu as pltpu

F32 = jnp.float32
BF16 = jnp.bfloat16

EPS = 1e-6
LANES = 128
V7X_VMEM_LIMIT_BYTES = 58 * 1024 * 1024
MASK_VALUE = -1e30
LOG2_E = math.log2(math.e)
BF16_SUBLANES = 16

CONV_WIDTH = 31
CONV_HALO = 32
ATT_HEAD_DIM = 128
ATT_ROPE_DIMS = ATT_HEAD_DIM // 4
ATT_ROPE_THETA = 500000.0
MOBA_BLOCK = 256
MOBA_TOPK = 3
MOBA_SOFTMAX_ROWS = 128
RET_HEADS = 8
RET_CHUNK = 128
RET_ROT_THETA = 10000.0
N_BRANCH = 3


def _params(*semantics):
    return pltpu.CompilerParams(dimension_semantics=semantics,
                                vmem_limit_bytes=V7X_VMEM_LIMIT_BYTES)


def _layer_vec(layer, width):
    return pl.BlockSpec((None, 1, width), lambda *_: (layer, 0, 0))


def _rms(x, g):
    return x * lax.rsqrt(jnp.mean(x * x, axis=-1, keepdims=True) + EPS) * g


def _silu(x):
    return x * jax.nn.sigmoid(x)


def _dot(a, b):
    return jnp.dot(a, b, preferred_element_type=F32)


def _dot_nt(a, b, precision=None):
    return lax.dot_general(a, b, (((1,), (1,)), ((), ())), precision=precision,
                           preferred_element_type=F32)


def _rope(x, cos, sin_signed, half, group):
    n = x.shape[-1]
    lane = lax.broadcasted_iota(jnp.int32, x.shape, x.ndim - 1)
    first = (lane % group) < half
    partner = jnp.where(first, pltpu.roll(x, n - half, x.ndim - 1), pltpu.roll(x, half, x.ndim - 1))
    return x * cos + partner * sin_signed


def _ffn_body(x_ref, pre_ref, wg_ref, wu_ref, wd_ref, post_ref, o_ref, h_ref):
    j = pl.program_id(1)

    @pl.when(j == 0)
    def _():
        h_ref[...] = _rms(x_ref[...], pre_ref[...]).astype(BF16)
        o_ref[...] = jnp.zeros_like(o_ref)

    h = h_ref[...]
    g = _dot(h, wg_ref[...].astype(BF16))
    u = _dot(h, wu_ref[...].astype(BF16))
    a = (_silu(g) * u).astype(BF16)
    o_ref[...] += _dot(a, wd_ref[...].astype(BF16))

    @pl.when(j == pl.num_programs(1) - 1)
    def _():
        o_ref[...] = x_ref[...] + _rms(o_ref[...], 0.5 * post_ref[...])


def _ffn(x, pre_g, wg, wu, wd, post_g, *, layer, tm=1024, tf=256):
    m, d = x.shape
    f = wg.shape[2]
    return pl.pallas_call(
        _ffn_body,
        grid=(m // tm, f // tf),
        in_specs=[
            pl.BlockSpec((tm, d), lambda i, j: (i, 0)),
            _layer_vec(layer, d),
            pl.BlockSpec((None, d, tf), lambda i, j: (layer, 0, j)),
            pl.BlockSpec((None, d, tf), lambda i, j: (layer, 0, j)),
            pl.BlockSpec((None, tf, d), lambda i, j: (layer, j, 0)),
            _layer_vec(layer, d),
        ],
        out_specs=pl.BlockSpec((tm, d), lambda i, j: (i, 0)),
        out_shape=jax.ShapeDtypeStruct((m, d), F32),
        scratch_shapes=[pltpu.VMEM((tm, d), BF16)],
        compiler_params=_params("parallel", "arbitrary"),
        name="ffn",
    )(x, pre_g, wg, wu, wd, post_g)


def _inproj_body(x_ref, g_ref, w_ref, o_ref, h_ref):
    @pl.when(pl.program_id(1) == 0)
    def _():
        h_ref[...] = _rms(x_ref[...], g_ref[...]).astype(BF16)

    o_ref[...] = _dot(h_ref[...], w_ref[...].astype(BF16))


def _inproj(x, g, w, *, layer, tm=2048, tn=512):
    m, d = x.shape
    n = w.shape[2]
    return pl.pallas_call(
        _inproj_body,
        grid=(m // tm, n // tn),
        in_specs=[
            pl.BlockSpec((tm, d), lambda i, j: (i, 0), pipeline_mode=pl.Buffered(1)),
            _layer_vec(layer, d),
            pl.BlockSpec((None, d, tn), lambda i, j: (layer, 0, j)),
        ],
        out_specs=pl.BlockSpec((tm, tn), lambda i, j: (i, j)),
        out_shape=jax.ShapeDtypeStruct((m, n), F32),
        scratch_shapes=[pltpu.VMEM((tm, d), BF16)],
        compiler_params=_params("parallel", "arbitrary"),
        name="inproj",
    )(x, g, w)


def _conv_body(a_ref, g_ref, ap_ref, gp_ref, w_ref, b_ref, lng_ref, lnb_ref, o_ref,
               u_ref, y_ref, *, ts, rows_per_acc):
    nck = u_ref.shape[0]
    u = a_ref[...] * jax.nn.sigmoid(g_ref[...])
    up = ap_ref[...] * jax.nn.sigmoid(gp_ref[...])
    up = jnp.where(pl.program_id(1) > 0, up, 0.0)
    for c in range(nck):
        u_ref[c, 0:CONV_HALO, :] = up[:, c * LANES:(c + 1) * LANES]
        u_ref[c, CONV_HALO:CONV_HALO + ts, :] = u[:, c * LANES:(c + 1) * LANES]

    first = CONV_HALO - (CONV_WIDTH - 1)

    def chunk(c, carry):
        for r in range(ts // rows_per_acc):
            r0 = r * rows_per_acc
            acc = jnp.zeros((rows_per_acc, LANES), F32)
            for w in range(CONV_WIDTH):
                acc = acc + u_ref[c, r0 + first + w:r0 + first + w + rows_per_acc, :] * w_ref[c, w:w + 1, :]
            y_ref[c, r0:r0 + rows_per_acc, :] = acc
        return carry

    lax.fori_loop(0, nck, chunk, 0)

    y = jnp.concatenate([y_ref[c] for c in range(nck)], axis=-1) + b_ref[...]
    mu = jnp.mean(y, axis=-1, keepdims=True)
    yc = y - mu
    yn = yc * lax.rsqrt(jnp.mean(yc * yc, axis=-1, keepdims=True) + EPS) * lng_ref[...] + lnb_ref[...]
    o_ref[...] = _silu(yn).astype(BF16)


def _conv(proj, w3, b, ln_g, ln_b, *, layer, batch, seq, ch, ts=512, rows_per_acc=128):
    m = proj.shape[0]
    nck = ch // LANES
    tiles = seq // ts
    halo_per_tile = ts // CONV_HALO

    def cur(col):
        return pl.BlockSpec((ts, ch), lambda bi, i: (bi * tiles + i, col))

    def prev(col):
        return pl.BlockSpec(
            (CONV_HALO, ch),
            lambda bi, i: (jnp.maximum((bi * tiles + i) * halo_per_tile - 1, 0), col))

    vec = _layer_vec(layer, ch)
    return pl.pallas_call(
        functools.partial(_conv_body, ts=ts, rows_per_acc=rows_per_acc),
        grid=(batch, tiles),
        in_specs=[cur(0), cur(1), prev(0), prev(1),
                  pl.BlockSpec((None, nck, CONV_WIDTH, LANES), lambda bi, i: (layer, 0, 0, 0)),
                  vec, vec, vec],
        out_specs=pl.BlockSpec((ts, ch), lambda bi, i: (bi * tiles + i, 0)),
        out_shape=jax.ShapeDtypeStruct((m, ch), BF16),
        scratch_shapes=[pltpu.VMEM((nck, CONV_HALO + ts, LANES), F32),
                        pltpu.VMEM((nck, ts, LANES), F32)],
        compiler_params=_params("parallel", "parallel"),
        name="conv",
    )(proj, proj, proj, proj, w3, b, ln_g, ln_b)


def _moba_body(q_ref, k_ref, v_ref, cos_ref, sin_ref, o_ref, kb_ref, vt_ref, km_ref, s_ref, m_ref, p_ref,
               *, nb, n_sel, scale):
    blk = MOBA_BLOCK
    hd = ATT_HEAD_DIM
    half = ATT_ROPE_DIMS // 2

    def prepare_block(j):
        rows = slice(j * blk, (j + 1) * blk)
        kj = _rope(k_ref[rows, :], cos_ref[rows, :], sin_ref[rows, :], half, ATT_HEAD_DIM)
        kb_ref[rows, :] = kj.astype(BF16)
        km_ref[j:j + 1, :] = jnp.mean(kj, axis=0, keepdims=True)
        vt_ref[0:hd, rows] = v_ref[rows, :].T.astype(BF16)
        vt_ref[hd:, rows] = jnp.ones((vt_ref.shape[0] - hd, blk), BF16)

    kc = MOBA_SOFTMAX_ROWS
    kidx = lax.broadcasted_iota(jnp.int32, (kc, blk), 0)
    qidx = lax.broadcasted_iota(jnp.int32, (kc, blk), 1)

    offs = [blk * i * (i + 1) // 2 for i in range(nb + 1)]
    bias_of = {}

    def scores(i):
        rows = slice(i * blk, (i + 1) * blk)
        nk = (i + 1) * blk
        q = _rope(q_ref[rows, :], cos_ref[rows, :], sin_ref[rows, :], half, ATT_HEAD_DIM)
        s_ref[offs[i]:offs[i + 1], :] = _dot_nt(kb_ref[0:nk, :], (q * (scale * LOG2_E)).astype(BF16))
        if i > n_sel:
            gate = _dot_nt(km_ref[...], q, precision=lax.Precision.HIGHEST)
            brow = lax.broadcasted_iota(jnp.int32, (nb, blk), 0)
            rank = jnp.zeros((nb, blk), jnp.int32)
            for jp in range(i):
                gj = gate[jp:jp + 1, :]
                rank = rank + jnp.where((gj > gate) | ((gj == gate) & (jp < brow)), 1, 0)
            bias_of[i] = jnp.where(rank < n_sel, 0.0, MASK_VALUE)

    def mask_and_max(i):
        m = None
        for c in range((i + 1) * blk // kc):
            keys = slice(offs[i] + c * kc, offs[i] + (c + 1) * kc)
            j = (c * kc) // blk
            sc = s_ref[keys, :]
            if j == i:
                sc = jnp.where(kidx + (c * kc - i * blk) <= qidx, sc, MASK_VALUE)
                s_ref[keys, :] = sc
            elif i in bias_of:
                sc = sc + bias_of[i][j:j + 1, :]
                s_ref[keys, :] = sc
            cm = jnp.max(sc, axis=0, keepdims=True)
            m = cm if m is None else jnp.maximum(m, cm)
        m_ref[i] = m

    def weights(i):
        m = m_ref[i]
        for c in range((i + 1) * blk // kc):
            keys = slice(offs[i] + c * kc, offs[i] + (c + 1) * kc)
            p_ref[keys, :] = jnp.exp2(s_ref[keys, :] - m).astype(BF16)

    def values(i):
        rows = slice(i * blk, (i + 1) * blk)
        acc = _dot(vt_ref[:, 0:(i + 1) * blk], p_ref[offs[i]:offs[i + 1], :])
        o_ref[rows, :] = (acc[0:hd, :] / acc[hd:hd + 1, :]).T.astype(BF16)

    for phase in (prepare_block, scores, mask_and_max, weights, values):
        for i in range(nb):
            phase(i)


def _moba(proj, cos_t, sin_t, *, batch, seq, heads, q_col, k_col, v_col):
    m = proj.shape[0]
    hd = ATT_HEAD_DIM
    nb = seq // MOBA_BLOCK
    n_sel = min(MOBA_TOPK, nb - 1)
    score_rows = MOBA_BLOCK * nb * (nb + 1) // 2
    table = pl.BlockSpec((seq, hd), lambda b, h: (0, 0))

    def head(col):
        return pl.BlockSpec((seq, hd), lambda b, h: (b, col // hd + h))

    return pl.pallas_call(
        functools.partial(_moba_body, nb=nb, n_sel=n_sel, scale=hd ** -0.5),
        grid=(batch, heads),
        in_specs=[head(q_col), head(k_col), head(v_col), table, table],
        out_specs=pl.BlockSpec((seq, hd), lambda b, h: (b, h)),
        out_shape=jax.ShapeDtypeStruct((m, heads * hd), BF16),
        scratch_shapes=[
            pltpu.VMEM((seq, hd), BF16),
            pltpu.VMEM((hd + BF16_SUBLANES, seq), BF16),
            pltpu.VMEM((nb, hd), F32),
            pltpu.VMEM((score_rows, MOBA_BLOCK), F32),
            pltpu.VMEM((nb, 1, MOBA_BLOCK), F32),
            pltpu.VMEM((score_rows, MOBA_BLOCK), BF16),
        ],
        compiler_params=_params("parallel", "parallel"),
        name="moba",
    )(proj, proj, proj, cos_t, sin_t)


def _ret_body(q_ref, k_ref, v_ref, gr_ref, cos_ref, sin_ref, dm_ref, qd_ref, kd_ref, cd_ref,
              ng_ref, o_ref, st_ref, *, heads, dk, dv, chunks, groups_per_phase):
    c = RET_CHUNK
    per_group = LANES // dk

    @pl.when(pl.program_id(1) == 0)
    def _():
        st_ref[...] = jnp.zeros_like(st_ref)

    lane_head = lax.broadcasted_iota(jnp.int32, (c, LANES), 1) // dk
    rows_of = lambda cc: slice(cc * c, (cc + 1) * c)
    n_groups = heads // per_group
    for g0 in range(0, n_groups, groups_per_phase):
        groups = range(g0, g0 + groups_per_phase)
        units = [(hg, cc, hh) for hg in groups for cc in range(chunks) for hh in range(per_group)]
        head = lambda u: u[0] * per_group + u[2]
        qr, kb, krt = {}, {}, {}
        for hg in groups:
            lanes = slice(hg * LANES, (hg + 1) * LANES)
            for cc in range(chunks):
                cos, sin = cos_ref[rows_of(cc), :], sin_ref[rows_of(cc), :]
                qr[hg, cc] = _rope(q_ref[rows_of(cc), lanes], cos, sin, dk // 2, dk)
                kr = _rope(k_ref[rows_of(cc), lanes], cos, sin, dk // 2, dk) * (dk ** -0.5)
                kb[hg, cc] = kr.astype(BF16)
                krt[hg, cc] = kr.T
        qh = {u: jnp.where(lane_head == u[2], qr[u[0], u[1]], 0.0) for u in units}
        vb = {u: v_ref[rows_of(u[1]), head(u) * dv:(head(u) + 1) * dv].astype(BF16) for u in units}
        inner = {u: _dot_nt(qh[u].astype(BF16), kb[u[0], u[1]]) * dm_ref[head(u)] for u in units}
        ro = {u: _dot(inner[u].astype(BF16), vb[u]) for u in units}
        kv = {u: _dot((krt[u[0], u[1]][u[2] * dk:(u[2] + 1) * dk, :] * kd_ref[head(u)]).astype(BF16), vb[u])
              for u in units}
        for hg in groups:
            st = st_ref[hg]
            for cc in range(chunks):
                stb = st.astype(BF16)
                new_rows = []
                for hh in range(per_group):
                    u = (hg, cc, hh)
                    h = head(u)
                    vl = slice(h * dv, (h + 1) * dv)
                    r = ro[u] + _dot((qh[u] * qd_ref[h]).astype(BF16), stb)
                    r = r * lax.rsqrt(jnp.mean(r * r, axis=-1, keepdims=True) + EPS) * ng_ref[:, vl]
                    o_ref[rows_of(cc), vl] = (_silu(gr_ref[rows_of(cc), vl]) * r).astype(BF16)
                    new_rows.append(st[hh * dk:(hh + 1) * dk, :] * cd_ref[h] + kv[u])
                st = jnp.concatenate(new_rows, axis=0)
            st_ref[hg] = st


def _retention(proj, cos_t, sin_t, dm, qd, kd, cd, norm_g, *, layer, batch, seq, heads, dk, dv,
               q_col, k_col, v_col, g_col, chunks=4, groups_per_phase=4):
    m = proj.shape[0]
    ts = chunks * RET_CHUNK
    tiles = seq // ts
    qk_w = heads * dk
    v_w = heads * dv

    def tok(width, col):
        return pl.BlockSpec((ts, width), lambda b, t: (b * tiles + t, col // width))

    def const(shape):
        return pl.BlockSpec(shape, lambda b, t: (0,) * len(shape))

    return pl.pallas_call(
        functools.partial(_ret_body, heads=heads, dk=dk, dv=dv, chunks=chunks, groups_per_phase=groups_per_phase),
        grid=(batch, tiles),
        in_specs=[
            tok(qk_w, q_col), tok(qk_w, k_col), tok(v_w, v_col), tok(v_w, g_col),
            pl.BlockSpec((ts, LANES), lambda b, t: (t, 0)),
            pl.BlockSpec((ts, LANES), lambda b, t: (t, 0)),
            const(dm.shape), const(qd.shape), const(kd.shape), const(cd.shape),
            _layer_vec(layer, v_w),
        ],
        out_specs=pl.BlockSpec((ts, v_w), lambda b, t: (b * tiles + t, 0)),
        out_shape=jax.ShapeDtypeStruct((m, v_w), BF16),
        scratch_shapes=[pltpu.VMEM((heads * dk // LANES, LANES, dv), F32)],
        compiler_params=_params("parallel", "arbitrary"),
        name="retention",
    )(proj, proj, proj, proj, cos_t, sin_t, dm, qd, kd, cd, norm_g)


def _merge_body(x_ref, yc_ref, ya_ref, yr_ref, wc_ref, wa_ref, wr_ref, g0_ref, g1_ref, g2_ref,
                gb_ref, wo_ref, post_ref, o_ref, mg_ref, *, tn):
    d = o_ref.shape[1]
    for t in range(d // tn):
        cols = slice(t * tn, (t + 1) * tn)
        merged = (jax.nn.sigmoid(g0_ref[:, cols] + gb_ref[0:1, cols]) * _dot(yc_ref[...], wc_ref[:, cols])
                  + jax.nn.sigmoid(g1_ref[:, cols] + gb_ref[1:2, cols]) * _dot(ya_ref[...], wa_ref[:, cols])
                  + jax.nn.sigmoid(g2_ref[:, cols] + gb_ref[2:3, cols]) * _dot(yr_ref[...], wr_ref[:, cols]))
        mg_ref[:, cols] = merged.astype(BF16)
    o_ref[...] = x_ref[...] + _rms(_dot(mg_ref[...], wo_ref[...]), post_ref[...])


def _merge(x, yc, ya, yr, wc, wa, wr, proj, gate_b, wo, post_g, *, layer, gates_col, tm=256, tn=512):
    m, d = x.shape

    def ytile(a):
        return pl.BlockSpec((tm, a.shape[1]), lambda i: (i, 0))

    def resident(w):
        return pl.BlockSpec((None,) + w.shape[1:], lambda i: (layer, 0, 0),
                            pipeline_mode=pl.Buffered(1))

    def gtile(br):
        return pl.BlockSpec((tm, d), lambda i: (i, gates_col // d + br))

    return pl.pallas_call(
        functools.partial(_merge_body, tn=tn),
        grid=(m // tm,),
        in_specs=[
            pl.BlockSpec((tm, d), lambda i: (i, 0)),
            ytile(yc), ytile(ya), ytile(yr), resident(wc), resident(wa), resident(wr),
            gtile(0), gtile(1), gtile(2),
            pl.BlockSpec((None, N_BRANCH, d), lambda i: (layer, 0, 0)),
            resident(wo),
            _layer_vec(layer, d),
        ],
        out_specs=pl.BlockSpec((tm, d), lambda i: (i, 0)),
        out_shape=jax.ShapeDtypeStruct((m, d), F32),
        scratch_shapes=[pltpu.VMEM((tm, d), BF16)],
        compiler_params=_params("parallel"),
        name="merge",
    )(x, yc, ya, yr, wc, wa, wr, proj, proj, proj, gate_b, wo, post_g)


def _rope_tables(seq, n_rot, theta, group):
    half = n_rot // 2
    inv = 1.0 / (theta ** (jnp.arange(half, dtype=F32) / half))
    ang = jnp.arange(seq, dtype=jnp.int32).astype(F32)[:, None] * inv[None, :]
    cos, sin = jnp.cos(ang), jnp.sin(ang)
    rest = group - n_rot
    cos_g = jnp.concatenate([cos, cos, jnp.ones((seq, rest), F32)], axis=-1)
    sin_g = jnp.concatenate([-sin, sin, jnp.zeros((seq, rest), F32)], axis=-1)
    reps = LANES // group
    return jnp.tile(cos_g, (1, reps)), jnp.tile(sin_g, (1, reps))


def _retention_constants(heads):
    c = RET_CHUNK
    log_g = jnp.log1p(-(2.0 ** (-5.0 - jnp.arange(heads, dtype=F32))))
    idx = jnp.arange(c, dtype=F32)
    diff = idx[:, None] - idx[None, :]
    decay_mask = jnp.exp(jnp.where(diff >= 0, log_g[:, None, None] * diff, -jnp.inf))
    q_decay = jnp.broadcast_to(jnp.exp(log_g[:, None] * (idx + 1.0))[:, :, None], (heads, c, LANES))
    k_decay = jnp.exp(log_g[:, None] * (c - 1.0 - idx))[:, None, :]
    chunk_decay = jnp.broadcast_to(jnp.exp(log_g * c)[:, None, None], (heads, 1, LANES))
    return decay_mask, q_decay, k_decay, chunk_decay


def kernel(x, ffn1_pre_g, ffn1_w_gate, ffn1_w_up, ffn1_w_down, ffn1_post_g, mix_pre_g, w_in, conv_dw_w, conv_dw_b, conv_ln_g, conv_ln_b, ret_norm_g, w_br_conv, w_br_att, w_br_ret, gate_b, w_out, mix_post_g, ffn2_pre_g, ffn2_w_gate, ffn2_w_up, ffn2_w_down, ffn2_post_g):
    batch, seq, d = x.shape
    depth = w_in.shape[0]
    conv_ch = conv_dw_w.shape[2]
    att_w = w_br_att.shape[1]
    ret_v_w = w_br_ret.shape[1]
    att_heads = att_w // ATT_HEAD_DIM
    ret_dv = ret_v_w // RET_HEADS
    ret_dk = ret_dv // 2
    ret_qk_w = RET_HEADS * ret_dk
    sizes = [conv_ch, conv_ch, att_w, att_w, att_w, ret_qk_w, ret_qk_w, ret_v_w, ret_v_w, N_BRANCH * d]
    assert sum(sizes) == w_in.shape[2]
    cols = [0]
    for s in sizes[:-1]:
        cols.append(cols[-1] + s)
    (_, _, qa_col, ka_col, va_col, qr_col, kr_col, vr_col, gr_col, gates_col) = cols

    att_cos, att_sin = _rope_tables(seq, ATT_ROPE_DIMS, ATT_ROPE_THETA, ATT_HEAD_DIM)
    ret_cos, ret_sin = _rope_tables(seq, ret_dk, RET_ROT_THETA, ret_dk)
    dm, qd, kd, cd = _retention_constants(RET_HEADS)

    bf = lambda w: w.astype(BF16)
    rows = lambda v: v.reshape(depth, 1, -1)
    ffn1 = (rows(ffn1_pre_g), ffn1_w_gate, ffn1_w_up, ffn1_w_down, rows(ffn1_post_g))
    ffn2 = (rows(ffn2_pre_g), ffn2_w_gate, ffn2_w_up, ffn2_w_down, rows(ffn2_post_g))
    conv_w = conv_dw_w.reshape(depth, CONV_WIDTH, conv_ch // LANES, LANES).transpose(0, 2, 1, 3)
    conv_p = (conv_w, rows(conv_dw_b), rows(conv_ln_g), rows(conv_ln_b))
    merge_w = (bf(w_br_conv), bf(w_br_att), bf(w_br_ret))
    gate_b3 = gate_b.reshape(depth, N_BRANCH, d)
    wo = bf(w_out)
    mix_pre, mix_post, ret_g = rows(mix_pre_g), rows(mix_post_g), rows(ret_norm_g)

    xf = x.reshape(batch * seq, d)
    for l in range(depth):
        xf = _ffn(xf, *ffn1, layer=l)
        proj = _inproj(xf, mix_pre, w_in, layer=l)
        y_conv = _conv(proj, *conv_p, layer=l, batch=batch, seq=seq, ch=conv_ch)
        y_att = _moba(proj, att_cos, att_sin, batch=batch, seq=seq, heads=att_heads,
                      q_col=qa_col, k_col=ka_col, v_col=va_col)
        y_ret = _retention(proj, ret_cos, ret_sin, dm, qd, kd, cd, ret_g, layer=l,
                           batch=batch, seq=seq, heads=RET_HEADS, dk=ret_dk, dv=ret_dv,
                           q_col=qr_col, k_col=kr_col, v_col=vr_col, g_col=gr_col)
        xf = _merge(xf, y_conv, y_att, y_ret, *merge_w, proj, gate_b3, wo, mix_post,
                    layer=l, gates_col=gates_col)
        xf = _ffn(xf, *ffn2, layer=l)
    return xf.reshape(batch, seq, d)
```

```python
import functools
import math

import jax
import jax.numpy as jnp
from jax import lax
from jax.experimental import pallas as pl
from jax.experimental.pallas import tpu as pltpu

F32 = jnp.float32
BF16 = jnp.bfloat16

EPS = 1e-6
LANES = 128
V7X_VMEM_LIMIT_BYTES = 58 * 1024 * 1024
MASK_VALUE = -1e30
LOG2_E = math.log2(math.e)
BF16_SUBLANES = 16
NORM_ROWS = 16

CONV_WIDTH = 31
CONV_HALO = 32
CONV_NORM_ROWS = 32
ATT_HEAD_DIM = 128
ATT_ROPE_DIMS = ATT_HEAD_DIM // 4
ATT_ROPE_THETA = 500000.0
MOBA_BLOCK = 256
MOBA_TOPK = 3
MOBA_SOFTMAX_ROWS = 128
RET_HEADS = 8
RET_CHUNK = 128
RET_ROT_THETA = 10000.0
N_BRANCH = 3


def _params(*semantics):
    return pltpu.CompilerParams(dimension_semantics=semantics,
                                vmem_limit_bytes=V7X_VMEM_LIMIT_BYTES)


def _layer_vec(layer, width):
    return pl.BlockSpec((None, 1, width), lambda *_: (layer, 0, 0))


def _rms(x, g):
    return x * lax.rsqrt(jnp.mean(x * x, axis=-1, keepdims=True) + EPS) * g


def _row_chunks(n_rows):
    return [slice(r, r + NORM_ROWS) for r in range(0, n_rows, NORM_ROWS)]


def _silu(x):
    return x * jax.nn.sigmoid(x)


def _dot(a, b):
    return jnp.dot(a, b, preferred_element_type=F32)


def _dot_nt(a, b, precision=None):
    return lax.dot_general(a, b, (((1,), (1,)), ((), ())), precision=precision,
                           preferred_element_type=F32)


def _rope(x, cos, sin_signed, half, group):
    n = x.shape[-1]
    lane = lax.broadcasted_iota(jnp.int32, x.shape, x.ndim - 1)
    first = (lane % group) < half
    partner = jnp.where(first, pltpu.roll(x, n - half, x.ndim - 1), pltpu.roll(x, half, x.ndim - 1))
    return x * cos + partner * sin_signed


def _ffn_body(x_ref, pre_ref, wg_ref, wu_ref, wd_ref, post_ref, o_ref, h_ref):
    j = pl.program_id(1)

    def hidden_tile(first):
        if first:
            for rows in _row_chunks(x_ref.shape[0]):
                h_ref[rows, :] = _rms(x_ref[rows, :], pre_ref[...]).astype(BF16)
        h = h_ref[...]
        g = _dot(h, wg_ref[...].astype(BF16))
        u = _dot(h, wu_ref[...].astype(BF16))
        a = (_silu(g) * u).astype(BF16)
        down = _dot(a, wd_ref[...].astype(BF16))
        if first:
            o_ref[...] = down
        else:
            o_ref[...] += down

    pl.when(j == 0)(lambda: hidden_tile(True))
    pl.when(j > 0)(lambda: hidden_tile(False))

    @pl.when(j == pl.num_programs(1) - 1)
    def _():
        half_post = 0.5 * post_ref[...]
        for rows in _row_chunks(x_ref.shape[0]):
            o_ref[rows, :] = x_ref[rows, :] + _rms(o_ref[rows, :], half_post)


def _ffn(x, pre_g, wg, wu, wd, post_g, *, layer, tm=1024, tf=256):
    m, d = x.shape
    f = wg.shape[2]
    return pl.pallas_call(
        _ffn_body,
        grid=(m // tm, f // tf),
        in_specs=[
            pl.BlockSpec((tm, d), lambda i, j: (i, 0)),
            _layer_vec(layer, d),
            pl.BlockSpec((None, d, tf), lambda i, j: (layer, 0, j)),
            pl.BlockSpec((None, d, tf), lambda i, j: (layer, 0, j)),
            pl.BlockSpec((None, tf, d), lambda i, j: (layer, j, 0)),
            _layer_vec(layer, d),
        ],
        out_specs=pl.BlockSpec((tm, d), lambda i, j: (i, 0)),
        out_shape=jax.ShapeDtypeStruct((m, d), F32),
        scratch_shapes=[pltpu.VMEM((tm, d), BF16)],
        compiler_params=_params("parallel", "arbitrary"),
        name="ffn",
    )(x, pre_g, wg, wu, wd, post_g)


def _inproj_body(x_ref, g_ref, w_ref, o_ref, h_ref):
    def column_tile(first):
        if first:
            for rows in _row_chunks(x_ref.shape[0]):
                h_ref[rows, :] = _rms(x_ref[rows, :], g_ref[...]).astype(BF16)
        o_ref[...] = _dot(h_ref[...], w_ref[...].astype(BF16))

    j = pl.program_id(1)
    pl.when(j == 0)(lambda: column_tile(True))
    pl.when(j > 0)(lambda: column_tile(False))


def _inproj(x, g, w, *, layer, tm=2048, tn=512):
    m, d = x.shape
    n = w.shape[2]
    return pl.pallas_call(
        _inproj_body,
        grid=(m // tm, n // tn),
        in_specs=[
            pl.BlockSpec((tm, d), lambda i, j: (i, 0), pipeline_mode=pl.Buffered(1)),
            _layer_vec(layer, d),
            pl.BlockSpec((None, d, tn), lambda i, j: (layer, 0, j)),
        ],
        out_specs=pl.BlockSpec((tm, tn), lambda i, j: (i, j)),
        out_shape=jax.ShapeDtypeStruct((m, n), F32),
        scratch_shapes=[pltpu.VMEM((tm, d), BF16)],
        compiler_params=_params("parallel", "arbitrary"),
        name="inproj",
    )(x, g, w)


def _conv_body(a_ref, g_ref, ap_ref, gp_ref, w_ref, b_ref, lng_ref, lnb_ref, o_ref,
               u_ref, y_ref, *, ts, rows_per_acc):
    nck = u_ref.shape[0]
    row_chunks = [slice(r, r + CONV_NORM_ROWS) for r in range(0, ts, CONV_NORM_ROWS)]

    def store_glu(u, dst_start):
        for c in range(nck):
            u_ref[c, dst_start:dst_start + u.shape[0], :] = u[:, c * LANES:(c + 1) * LANES]

    up = ap_ref[...] * jax.nn.sigmoid(gp_ref[...])
    store_glu(jnp.where(pl.program_id(1) > 0, up, 0.0), 0)
    for rows in row_chunks:
        store_glu(a_ref[rows, :] * jax.nn.sigmoid(g_ref[rows, :]), CONV_HALO + rows.start)

    first = CONV_HALO - (CONV_WIDTH - 1)

    def chunk(c, carry):
        for r in range(ts // rows_per_acc):
            r0 = r * rows_per_acc
            acc = jnp.zeros((rows_per_acc, LANES), F32)
            for w in range(CONV_WIDTH):
                acc = acc + u_ref[c, r0 + first + w:r0 + first + w + rows_per_acc, :] * w_ref[c, w:w + 1, :]
            y_ref[c, r0:r0 + rows_per_acc, :] = acc
        return carry

    lax.fori_loop(0, nck, chunk, 0)

    for rows in row_chunks:
        y = jnp.concatenate([y_ref[c, rows, :] for c in range(nck)], axis=-1) + b_ref[...]
        mu = jnp.mean(y, axis=-1, keepdims=True)
        yc = y - mu
        yn = yc * lax.rsqrt(jnp.mean(yc * yc, axis=-1, keepdims=True) + EPS) * lng_ref[...] + lnb_ref[...]
        o_ref[rows, :] = _silu(yn).astype(BF16)


def _conv(proj, w3, b, ln_g, ln_b, *, layer, batch, seq, ch, ts=512, rows_per_acc=128):
    m = proj.shape[0]
    nck = ch // LANES
    tiles = seq // ts
    halo_per_tile = ts // CONV_HALO

    def cur(col):
        return pl.BlockSpec((ts, ch), lambda bi, i: (bi * tiles + i, col))

    def prev(col):
        return pl.BlockSpec(
            (CONV_HALO, ch),
            lambda bi, i: (jnp.maximum((bi * tiles + i) * halo_per_tile - 1, 0), col))

    vec = _layer_vec(layer, ch)
    return pl.pallas_call(
        functools.partial(_conv_body, ts=ts, rows_per_acc=rows_per_acc),
        grid=(batch, tiles),
        in_specs=[cur(0), cur(1), prev(0), prev(1),
                  pl.BlockSpec((None, nck, CONV_WIDTH, LANES), lambda bi, i: (layer, 0, 0, 0)),
                  vec, vec, vec],
        out_specs=pl.BlockSpec((ts, ch), lambda bi, i: (bi * tiles + i, 0)),
        out_shape=jax.ShapeDtypeStruct((m, ch), BF16),
        scratch_shapes=[pltpu.VMEM((nck, CONV_HALO + ts, LANES), F32),
                        pltpu.VMEM((nck, ts, LANES), F32)],
        compiler_params=_params("parallel", "parallel"),
        name="conv",
    )(proj, proj, proj, proj, w3, b, ln_g, ln_b)


def _moba_body(q_ref, k_ref, v_ref, cos_ref, sin_ref, o_ref, kb_ref, vt_ref, km_ref, s_ref, m_ref, p_ref,
               *, nb, n_sel, scale):
    blk = MOBA_BLOCK
    hd = ATT_HEAD_DIM
    half = ATT_ROPE_DIMS // 2

    def prepare_block(j):
        rows = slice(j * blk, (j + 1) * blk)
        kj = _rope(k_ref[rows, :], cos_ref[rows, :], sin_ref[rows, :], half, ATT_HEAD_DIM)
        kb_ref[rows, :] = kj.astype(BF16)
        km_ref[j:j + 1, :] = jnp.mean(kj, axis=0, keepdims=True)
        vt_ref[0:hd, rows] = v_ref[rows, :].T.astype(BF16)
        vt_ref[hd:, rows] = jnp.ones((vt_ref.shape[0] - hd, blk), BF16)

    kc = MOBA_SOFTMAX_ROWS
    kidx = lax.broadcasted_iota(jnp.int32, (kc, blk), 0)
    qidx = lax.broadcasted_iota(jnp.int32, (kc, blk), 1)

    offs = [blk * i * (i + 1) // 2 for i in range(nb + 1)]
    bias_of = {}

    def scores(i):
        rows = slice(i * blk, (i + 1) * blk)
        nk = (i + 1) * blk
        q = _rope(q_ref[rows, :], cos_ref[rows, :], sin_ref[rows, :], half, ATT_HEAD_DIM)
        s_ref[offs[i]:offs[i + 1], :] = _dot_nt(kb_ref[0:nk, :], (q * (scale * LOG2_E)).astype(BF16))
        if i > n_sel:
            gate = _dot_nt(km_ref[...], q, precision=lax.Precision.HIGHEST)
            brow = lax.broadcasted_iota(jnp.int32, (nb, blk), 0)
            rank = jnp.zeros((nb, blk), jnp.int32)
            for jp in range(i):
                gj = gate[jp:jp + 1, :]
                rank = rank + jnp.where((gj > gate) | ((gj == gate) & (jp < brow)), 1, 0)
            bias_of[i] = jnp.where(rank < n_sel, 0.0, MASK_VALUE)

    def mask_and_max(i):
        m = None
        for c in range((i + 1) * blk // kc):
            keys = slice(offs[i] + c * kc, offs[i] + (c + 1) * kc)
            j = (c * kc) // blk
            sc = s_ref[keys, :]
            if j == i:
                sc = jnp.where(kidx + (c * kc - i * blk) <= qidx, sc, MASK_VALUE)
                s_ref[keys, :] = sc
            elif i in bias_of:
                sc = sc + bias_of[i][j:j + 1, :]
                s_ref[keys, :] = sc
            cm = jnp.max(sc, axis=0, keepdims=True)
            m = cm if m is None else jnp.maximum(m, cm)
        m_ref[i] = m

    def weights(i):
        m = m_ref[i]
        for c in range((i + 1) * blk // kc):
            keys = slice(offs[i] + c * kc, offs[i] + (c + 1) * kc)
            p_ref[keys, :] = jnp.exp2(s_ref[keys, :] - m).astype(BF16)

    def values(i):
        rows = slice(i * blk, (i + 1) * blk)
        acc = _dot(vt_ref[:, 0:(i + 1) * blk], p_ref[offs[i]:offs[i + 1], :])
        o_ref[rows, :] = (acc[0:hd, :] / acc[hd:hd + 1, :]).T.astype(BF16)

    for phase in (prepare_block, scores, mask_and_max, weights, values):
        for i in range(nb):
            phase(i)


def _moba(proj, cos_t, sin_t, *, batch, seq, heads, q_col, k_col, v_col):
    m = proj.shape[0]
    hd = ATT_HEAD_DIM
    nb = seq // MOBA_BLOCK
    n_sel = min(MOBA_TOPK, nb - 1)
    score_rows = MOBA_BLOCK * nb * (nb + 1) // 2
    table = pl.BlockSpec((seq, hd), lambda b, h: (0, 0))

    def head(col):
        return pl.BlockSpec((seq, hd), lambda b, h: (b, col // hd + h))

    return pl.pallas_call(
        functools.partial(_moba_body, nb=nb, n_sel=n_sel, scale=hd ** -0.5),
        grid=(batch, heads),
        in_specs=[head(q_col), head(k_col), head(v_col), table, table],
        out_specs=pl.BlockSpec((seq, hd), lambda b, h: (b, h)),
        out_shape=jax.ShapeDtypeStruct((m, heads * hd), BF16),
        scratch_shapes=[
            pltpu.VMEM((seq, hd), BF16),
            pltpu.VMEM((hd + BF16_SUBLANES, seq), BF16),
            pltpu.VMEM((nb, hd), F32),
            pltpu.VMEM((score_rows, MOBA_BLOCK), F32),
            pltpu.VMEM((nb, 1, MOBA_BLOCK), F32),
            pltpu.VMEM((score_rows, MOBA_BLOCK), BF16),
        ],
        compiler_params=_params("parallel", "parallel"),
        name="moba",
    )(proj, proj, proj, cos_t, sin_t)


def _ret_body(q_ref, k_ref, v_ref, gr_ref, cos_ref, sin_ref, dm_ref, qd_ref, kd_ref, cd_ref,
              ng_ref, o_ref, st_ref, *, heads, dk, dv, chunks, groups_per_phase):
    c = RET_CHUNK
    per_group = LANES // dk

    @pl.when(pl.program_id(1) == 0)
    def _():
        st_ref[...] = jnp.zeros_like(st_ref)

    lane_head = lax.broadcasted_iota(jnp.int32, (c, LANES), 1) // dk
    rows_of = lambda cc: slice(cc * c, (cc + 1) * c)
    n_groups = heads // per_group
    for g0 in range(0, n_groups, groups_per_phase):
        groups = range(g0, g0 + groups_per_phase)
        units = [(hg, cc, hh) for hg in groups for cc in range(chunks) for hh in range(per_group)]
        head = lambda u: u[0] * per_group + u[2]
        qr, kb, krt = {}, {}, {}
        for hg in groups:
            lanes = slice(hg * LANES, (hg + 1) * LANES)
            for cc in range(chunks):
                cos, sin = cos_ref[rows_of(cc), :], sin_ref[rows_of(cc), :]
                qr[hg, cc] = _rope(q_ref[rows_of(cc), lanes], cos, sin, dk // 2, dk)
                kr = _rope(k_ref[rows_of(cc), lanes], cos, sin, dk // 2, dk) * (dk ** -0.5)
                kb[hg, cc] = kr.astype(BF16)
                krt[hg, cc] = kr.T
        qh = {u: jnp.where(lane_head == u[2], qr[u[0], u[1]], 0.0) for u in units}
        vb = {u: v_ref[rows_of(u[1]), head(u) * dv:(head(u) + 1) * dv].astype(BF16) for u in units}
        inner = {u: _dot_nt(qh[u].astype(BF16), kb[u[0], u[1]]) * dm_ref[head(u)] for u in units}
        ro = {u: _dot(inner[u].astype(BF16), vb[u]) for u in units}
        kv = {u: _dot((krt[u[0], u[1]][u[2] * dk:(u[2] + 1) * dk, :] * kd_ref[head(u)]).astype(BF16), vb[u])
              for u in units}
        for hg in groups:
            st = st_ref[hg]
            for cc in range(chunks):
                stb = st.astype(BF16)
                new_rows = []
                for hh in range(per_group):
                    u = (hg, cc, hh)
                    h = head(u)
                    vl = slice(h * dv, (h + 1) * dv)
                    r = ro[u] + _dot((qh[u] * qd_ref[h]).astype(BF16), stb)
                    r = r * lax.rsqrt(jnp.mean(r * r, axis=-1, keepdims=True) + EPS) * ng_ref[:, vl]
                    o_ref[rows_of(cc), vl] = (_silu(gr_ref[rows_of(cc), vl]) * r).astype(BF16)
                    new_rows.append(st[hh * dk:(hh + 1) * dk, :] * cd_ref[h] + kv[u])
                st = jnp.concatenate(new_rows, axis=0)
            st_ref[hg] = st


def _retention(proj, cos_t, sin_t, dm, qd, kd, cd, norm_g, *, layer, batch, seq, heads, dk, dv,
               q_col, k_col, v_col, g_col, chunks=4, groups_per_phase=4):
    m = proj.shape[0]
    ts = chunks * RET_CHUNK
    tiles = seq // ts
    qk_w = heads * dk
    v_w = heads * dv

    def tok(width, col):
        return pl.BlockSpec((ts, width), lambda b, t: (b * tiles + t, col // width))

    def const(shape):
        return pl.BlockSpec(shape, lambda b, t: (0,) * len(shape))

    return pl.pallas_call(
        functools.partial(_ret_body, heads=heads, dk=dk, dv=dv, chunks=chunks, groups_per_phase=groups_per_phase),
        grid=(batch, tiles),
        in_specs=[
            tok(qk_w, q_col), tok(qk_w, k_col), tok(v_w, v_col), tok(v_w, g_col),
            pl.BlockSpec((ts, LANES), lambda b, t: (t, 0)),
            pl.BlockSpec((ts, LANES), lambda b, t: (t, 0)),
            const(dm.shape), const(qd.shape), const(kd.shape), const(cd.shape),
            _layer_vec(layer, v_w),
        ],
        out_specs=pl.BlockSpec((ts, v_w), lambda b, t: (b * tiles + t, 0)),
        out_shape=jax.ShapeDtypeStruct((m, v_w), BF16),
        scratch_shapes=[pltpu.VMEM((heads * dk // LANES, LANES, dv), F32)],
        compiler_params=_params("parallel", "arbitrary"),
        name="retention",
    )(proj, proj, proj, proj, cos_t, sin_t, dm, qd, kd, cd, norm_g)


def _merge_body(x_ref, yc_ref, ya_ref, yr_ref, wc_ref, wa_ref, wr_ref, g0_ref, g1_ref, g2_ref,
                gb_ref, wo_ref, post_ref, o_ref, mg_ref, *, tn):
    d = o_ref.shape[1]
    for t in range(d // tn):
        cols = slice(t * tn, (t + 1) * tn)
        merged = (jax.nn.sigmoid(g0_ref[:, cols] + gb_ref[0:1, cols]) * _dot(yc_ref[...], wc_ref[:, cols])
                  + jax.nn.sigmoid(g1_ref[:, cols] + gb_ref[1:2, cols]) * _dot(ya_ref[...], wa_ref[:, cols])
                  + jax.nn.sigmoid(g2_ref[:, cols] + gb_ref[2:3, cols]) * _dot(yr_ref[...], wr_ref[:, cols]))
        mg_ref[:, cols] = merged.astype(BF16)
    o_ref[...] = x_ref[...] + _rms(_dot(mg_ref[...], wo_ref[...]), post_ref[...])


def _merge(x, yc, ya, yr, wc, wa, wr, proj, gate_b, wo, post_g, *, layer, gates_col, tm=256, tn=512):
    m, d = x.shape

    def ytile(a):
        return pl.BlockSpec((tm, a.shape[1]), lambda i: (i, 0))

    def resident(w):
        return pl.BlockSpec((None,) + w.shape[1:], lambda i: (layer, 0, 0),
                            pipeline_mode=pl.Buffered(1))

    def gtile(br):
        return pl.BlockSpec((tm, d), lambda i: (i, gates_col // d + br))

    return pl.pallas_call(
        functools.partial(_merge_body, tn=tn),
        grid=(m // tm,),
        in_specs=[
            pl.BlockSpec((tm, d), lambda i: (i, 0)),
            ytile(yc), ytile(ya), ytile(yr), resident(wc), resident(wa), resident(wr),
            gtile(0), gtile(1), gtile(2),
            pl.BlockSpec((None, N_BRANCH, d), lambda i: (layer, 0, 0)),
            resident(wo),
            _layer_vec(layer, d),
        ],
        out_specs=pl.BlockSpec((tm, d), lambda i: (i, 0)),
        out_shape=jax.ShapeDtypeStruct((m, d), F32),
        scratch_shapes=[pltpu.VMEM((tm, d), BF16)],
        compiler_params=_params("parallel"),
        name="merge",
    )(x, yc, ya, yr, wc, wa, wr, proj, proj, proj, gate_b, wo, post_g)


def _rope_tables(seq, n_rot, theta, group):
    half = n_rot // 2
    inv = 1.0 / (theta ** (jnp.arange(half, dtype=F32) / half))
    ang = jnp.arange(seq, dtype=jnp.int32).astype(F32)[:, None] * inv[None, :]
    cos, sin = jnp.cos(ang), jnp.sin(ang)
    rest = group - n_rot
    cos_g = jnp.concatenate([cos, cos, jnp.ones((seq, rest), F32)], axis=-1)
    sin_g = jnp.concatenate([-sin, sin, jnp.zeros((seq, rest), F32)], axis=-1)
    reps = LANES // group
    return jnp.tile(cos_g, (1, reps)), jnp.tile(sin_g, (1, reps))


def _retention_constants(heads):
    c = RET_CHUNK
    log_g = jnp.log1p(-(2.0 ** (-5.0 - jnp.arange(heads, dtype=F32))))
    idx = jnp.arange(c, dtype=F32)
    diff = idx[:, None] - idx[None, :]
    decay_mask = jnp.exp(jnp.where(diff >= 0, log_g[:, None, None] * diff, -jnp.inf))
    q_decay = jnp.broadcast_to(jnp.exp(log_g[:, None] * (idx + 1.0))[:, :, None], (heads, c, LANES))
    k_decay = jnp.exp(log_g[:, None] * (c - 1.0 - idx))[:, None, :]
    chunk_decay = jnp.broadcast_to(jnp.exp(log_g * c)[:, None, None], (heads, 1, LANES))
    return decay_mask, q_decay, k_decay, chunk_decay


def kernel(x, ffn1_pre_g, ffn1_w_gate, ffn1_w_up, ffn1_w_down, ffn1_post_g, mix_pre_g, w_in, conv_dw_w, conv_dw_b, conv_ln_g, conv_ln_b, ret_norm_g, w_br_conv, w_br_att, w_br_ret, gate_b, w_out, mix_post_g, ffn2_pre_g, ffn2_w_gate, ffn2_w_up, ffn2_w_down, ffn2_post_g):
    batch, seq, d = x.shape
    depth = w_in.shape[0]
    conv_ch = conv_dw_w.shape[2]
    att_w = w_br_att.shape[1]
    ret_v_w = w_br_ret.shape[1]
    att_heads = att_w // ATT_HEAD_DIM
    ret_dv = ret_v_w // RET_HEADS
    ret_dk = ret_dv // 2
    ret_qk_w = RET_HEADS * ret_dk
    sizes = [conv_ch, conv_ch, att_w, att_w, att_w, ret_qk_w, ret_qk_w, ret_v_w, ret_v_w, N_BRANCH * d]
    assert sum(sizes) == w_in.shape[2]
    cols = [0]
    for s in sizes[:-1]:
        cols.append(cols[-1] + s)
    (_, _, qa_col, ka_col, va_col, qr_col, kr_col, vr_col, gr_col, gates_col) = cols

    att_cos, att_sin = _rope_tables(seq, ATT_ROPE_DIMS, ATT_ROPE_THETA, ATT_HEAD_DIM)
    ret_cos, ret_sin = _rope_tables(seq, ret_dk, RET_ROT_THETA, ret_dk)
    dm, qd, kd, cd = _retention_constants(RET_HEADS)

    bf = lambda w: w.astype(BF16)
    rows = lambda v: v.reshape(depth, 1, -1)
    ffn1 = (rows(ffn1_pre_g), ffn1_w_gate, ffn1_w_up, ffn1_w_down, rows(ffn1_post_g))
    ffn2 = (rows(ffn2_pre_g), ffn2_w_gate, ffn2_w_up, ffn2_w_down, rows(ffn2_post_g))
    conv_w = conv_dw_w.reshape(depth, CONV_WIDTH, conv_ch // LANES, LANES).transpose(0, 2, 1, 3)
    conv_p = (conv_w, rows(conv_dw_b), rows(conv_ln_g), rows(conv_ln_b))
    merge_w = (bf(w_br_conv), bf(w_br_att), bf(w_br_ret))
    gate_b3 = gate_b.reshape(depth, N_BRANCH, d)
    wo = bf(w_out)
    mix_pre, mix_post, ret_g = rows(mix_pre_g), rows(mix_post_g), rows(ret_norm_g)

    xf = x.reshape(batch * seq, d)
    for l in range(depth):
        xf = _ffn(xf, *ffn1, layer=l)
        proj = _inproj(xf, mix_pre, w_in, layer=l)
        y_conv = _conv(proj, *conv_p, layer=l, batch=batch, seq=seq, ch=conv_ch)
        y_att = _moba(proj, att_cos, att_sin, batch=batch, seq=seq, heads=att_heads,
                      q_col=qa_col, k_col=ka_col, v_col=va_col)
        y_ret = _retention(proj, ret_cos, ret_sin, dm, qd, kd, cd, ret_g, layer=l,
                           batch=batch, seq=seq, heads=RET_HEADS, dk=ret_dk, dv=ret_dv,
                           q_col=qr_col, k_col=kr_col, v_col=vr_col, g_col=gr_col)
        xf = _merge(xf, y_conv, y_att, y_ret, *merge_w, proj, gate_b3, wo, mix_post,
                    layer=l, gates_col=gates_col)
        xf = _ffn(xf, *ffn2, layer=l)
    return xf.reshape(batch, seq, d)
```

```python
import functools
import math

import jax
import jax.numpy as jnp
from jax import lax
from jax.experimental import pallas as pl
from jax.experimental.pallas import tpu as pltpu

F32 = jnp.float32
BF16 = jnp.bfloat16

EPS = 1e-6
LANES = 128
V7X_VMEM_LIMIT_BYTES = 58 * 1024 * 1024
MASK_VALUE = -1e30
LOG2_E = math.log2(math.e)
BF16_SUBLANES = 16
NORM_ROWS = 16

CONV_WIDTH = 31
CONV_HALO = 32
CONV_NORM_ROWS = 32
ATT_HEAD_DIM = 128
ATT_ROPE_DIMS = ATT_HEAD_DIM // 4
ATT_ROPE_THETA = 500000.0
MOBA_BLOCK = 256
MOBA_TOPK = 3
MOBA_SOFTMAX_ROWS = 128
RET_HEADS = 8
RET_CHUNK = 128
RET_ROT_THETA = 10000.0
N_BRANCH = 3

FFN_ROW_TILE = 1024
FFN_HIDDEN_TILE = 256
INPROJ_ROW_TILE = 2048
INPROJ_COL_TILE = 512
CONV_ROW_TILE = 512
CONV_ACC_ROWS = 128
MOBA_HEADS_PER_STEP = 2
RET_CHUNKS_PER_TILE = 4
MERGE_ROW_TILE = 256
MERGE_COL_CHUNK = 512


def _params(*semantics):
    return pltpu.CompilerParams(dimension_semantics=semantics,
                                vmem_limit_bytes=V7X_VMEM_LIMIT_BYTES)


def _layer_vec(layer, width):
    return pl.BlockSpec((None, 1, width), lambda *_: (layer, 0, 0))


def _rms(x, g):
    return x * lax.rsqrt(jnp.mean(x * x, axis=-1, keepdims=True) + EPS) * g


def _row_chunks(n_rows):
    return [slice(r, r + NORM_ROWS) for r in range(0, n_rows, NORM_ROWS)]


def _silu(x):
    return x * jax.nn.sigmoid(x)


def _dot(a, b):
    return jnp.dot(a, b, preferred_element_type=F32)


def _dot_nt(a, b, precision=None):
    return lax.dot_general(a, b, (((1,), (1,)), ((), ())), precision=precision,
                           preferred_element_type=F32)


def _rope(x, cos, sin_signed, half, group):
    n = x.shape[-1]
    lane = lax.broadcasted_iota(jnp.int32, x.shape, x.ndim - 1)
    first = (lane % group) < half
    partner = jnp.where(first, pltpu.roll(x, n - half, x.ndim - 1), pltpu.roll(x, half, x.ndim - 1))
    return x * cos + partner * sin_signed


def _ffn_body(x_ref, pre_ref, wg_ref, wu_ref, wd_ref, post_ref, o_ref, h_ref):
    j = pl.program_id(1)

    def hidden_tile(first):
        if first:
            for rows in _row_chunks(x_ref.shape[0]):
                h_ref[rows, :] = _rms(x_ref[rows, :], pre_ref[...]).astype(BF16)
        h = h_ref[...]
        g = _dot(h, wg_ref[...].astype(BF16))
        u = _dot(h, wu_ref[...].astype(BF16))
        a = (_silu(g) * u).astype(BF16)
        down = _dot(a, wd_ref[...].astype(BF16))
        if first:
            o_ref[...] = down
        else:
            o_ref[...] += down

    pl.when(j == 0)(lambda: hidden_tile(True))
    pl.when(j > 0)(lambda: hidden_tile(False))

    @pl.when(j == pl.num_programs(1) - 1)
    def _():
        half_post = 0.5 * post_ref[...]
        for rows in _row_chunks(x_ref.shape[0]):
            o_ref[rows, :] = x_ref[rows, :] + _rms(o_ref[rows, :], half_post)


def _ffn(x, pre_g, wg, wu, wd, post_g, *, layer, tm=FFN_ROW_TILE, tf=FFN_HIDDEN_TILE):
    m, d = x.shape
    f = wg.shape[2]
    return pl.pallas_call(
        _ffn_body,
        grid=(m // tm, f // tf),
        in_specs=[
            pl.BlockSpec((tm, d), lambda i, j: (i, 0)),
            _layer_vec(layer, d),
            pl.BlockSpec((None, d, tf), lambda i, j: (layer, 0, j)),
            pl.BlockSpec((None, d, tf), lambda i, j: (layer, 0, j)),
            pl.BlockSpec((None, tf, d), lambda i, j: (layer, j, 0)),
            _layer_vec(layer, d),
        ],
        out_specs=pl.BlockSpec((tm, d), lambda i, j: (i, 0)),
        out_shape=jax.ShapeDtypeStruct((m, d), F32),
        scratch_shapes=[pltpu.VMEM((tm, d), BF16)],
        compiler_params=_params("parallel", "arbitrary"),
        name="ffn",
    )(x, pre_g, wg, wu, wd, post_g)


def _inproj_body(x_ref, g_ref, w_ref, o_ref, h_ref):
    def column_tile(first):
        if first:
            for rows in _row_chunks(x_ref.shape[0]):
                h_ref[rows, :] = _rms(x_ref[rows, :], g_ref[...]).astype(BF16)
        o_ref[...] = _dot(h_ref[...], w_ref[...].astype(BF16))

    j = pl.program_id(1)
    pl.when(j == 0)(lambda: column_tile(True))
    pl.when(j > 0)(lambda: column_tile(False))


def _inproj(x, g, w, *, layer, tm=INPROJ_ROW_TILE, tn=INPROJ_COL_TILE):
    m, d = x.shape
    n = w.shape[2]
    return pl.pallas_call(
        _inproj_body,
        grid=(m // tm, n // tn),
        in_specs=[
            pl.BlockSpec((tm, d), lambda i, j: (i, 0), pipeline_mode=pl.Buffered(1)),
            _layer_vec(layer, d),
            pl.BlockSpec((None, d, tn), lambda i, j: (layer, 0, j)),
        ],
        out_specs=pl.BlockSpec((tm, tn), lambda i, j: (i, j)),
        out_shape=jax.ShapeDtypeStruct((m, n), F32),
        scratch_shapes=[pltpu.VMEM((tm, d), BF16)],
        compiler_params=_params("parallel", "arbitrary"),
        name="inproj",
    )(x, g, w)


def _conv_body(a_ref, g_ref, ap_ref, gp_ref, w_ref, b_ref, lng_ref, lnb_ref, o_ref,
               u_ref, y_ref, *, ts, rows_per_acc):
    nck = u_ref.shape[0]
    row_chunks = [slice(r, r + CONV_NORM_ROWS) for r in range(0, ts, CONV_NORM_ROWS)]

    def store_glu(u, dst_start):
        for c in range(nck):
            u_ref[c, dst_start:dst_start + u.shape[0], :] = u[:, c * LANES:(c + 1) * LANES]

    up = ap_ref[...] * jax.nn.sigmoid(gp_ref[...])
    store_glu(jnp.where(pl.program_id(1) > 0, up, 0.0), 0)
    for rows in row_chunks:
        store_glu(a_ref[rows, :] * jax.nn.sigmoid(g_ref[rows, :]), CONV_HALO + rows.start)

    first = CONV_HALO - (CONV_WIDTH - 1)

    def chunk(c, carry):
        for r in range(ts // rows_per_acc):
            r0 = r * rows_per_acc
            acc = jnp.zeros((rows_per_acc, LANES), F32)
            for w in range(CONV_WIDTH):
                acc = acc + u_ref[c, r0 + first + w:r0 + first + w + rows_per_acc, :] * w_ref[c, w:w + 1, :]
            y_ref[c, r0:r0 + rows_per_acc, :] = acc
        return carry

    lax.fori_loop(0, nck, chunk, 0)

    for rows in row_chunks:
        y = jnp.concatenate([y_ref[c, rows, :] for c in range(nck)], axis=-1) + b_ref[...]
        mu = jnp.mean(y, axis=-1, keepdims=True)
        yc = y - mu
        yn = yc * lax.rsqrt(jnp.mean(yc * yc, axis=-1, keepdims=True) + EPS) * lng_ref[...] + lnb_ref[...]
        o_ref[rows, :] = _silu(yn).astype(BF16)


def _conv(proj, w3, b, ln_g, ln_b, *, layer, batch, seq, ch, ts=CONV_ROW_TILE, rows_per_acc=CONV_ACC_ROWS):
    m = proj.shape[0]
    nck = ch // LANES
    tiles = seq // ts
    halo_per_tile = ts // CONV_HALO

    def cur(col):
        return pl.BlockSpec((ts, ch), lambda bi, i: (bi * tiles + i, col))

    def prev(col):
        return pl.BlockSpec(
            (CONV_HALO, ch),
            lambda bi, i: (jnp.maximum((bi * tiles + i) * halo_per_tile - 1, 0), col))

    vec = _layer_vec(layer, ch)
    return pl.pallas_call(
        functools.partial(_conv_body, ts=ts, rows_per_acc=rows_per_acc),
        grid=(batch, tiles),
        in_specs=[cur(0), cur(1), prev(0), prev(1),
                  pl.BlockSpec((None, nck, CONV_WIDTH, LANES), lambda bi, i: (layer, 0, 0, 0)),
                  vec, vec, vec],
        out_specs=pl.BlockSpec((ts, ch), lambda bi, i: (bi * tiles + i, 0)),
        out_shape=jax.ShapeDtypeStruct((m, ch), BF16),
        scratch_shapes=[pltpu.VMEM((nck, CONV_HALO + ts, LANES), F32),
                        pltpu.VMEM((nck, ts, LANES), F32)],
        compiler_params=_params("parallel", "parallel"),
        name="conv",
    )(proj, proj, proj, proj, w3, b, ln_g, ln_b)


def _moba_body(q_ref, k_ref, v_ref, cos_ref, sin_ref, o_ref, kb_ref, vt_ref, km_ref, s_ref, m_ref, p_ref,
               *, nb, n_sel, scale, heads_per_step):
    blk = MOBA_BLOCK
    hd = ATT_HEAD_DIM
    half = ATT_ROPE_DIMS // 2

    def head_lanes(g):
        return slice(g * hd, (g + 1) * hd)

    def prepare_block(g, j):
        rows = slice(j * blk, (j + 1) * blk)
        kj = _rope(k_ref[rows, head_lanes(g)], cos_ref[rows, :], sin_ref[rows, :], half, ATT_HEAD_DIM)
        kb_ref[g, rows, :] = kj.astype(BF16)
        km_ref[g, j:j + 1, :] = jnp.mean(kj, axis=0, keepdims=True)
        vt_ref[g, 0:hd, rows] = v_ref[rows, head_lanes(g)].T.astype(BF16)
        vt_ref[g, hd:, rows] = jnp.ones((vt_ref.shape[1] - hd, blk), BF16)

    kc = MOBA_SOFTMAX_ROWS
    kidx = lax.broadcasted_iota(jnp.int32, (kc, blk), 0)
    qidx = lax.broadcasted_iota(jnp.int32, (kc, blk), 1)

    offs = [blk * i * (i + 1) // 2 for i in range(nb + 1)]
    bias_of = {}

    def scores(g, i):
        rows = slice(i * blk, (i + 1) * blk)
        nk = (i + 1) * blk
        q = _rope(q_ref[rows, head_lanes(g)], cos_ref[rows, :], sin_ref[rows, :], half, ATT_HEAD_DIM)
        s_ref[g, offs[i]:offs[i + 1], :] = _dot_nt(kb_ref[g, 0:nk, :], (q * (scale * LOG2_E)).astype(BF16))
        if i > n_sel:
            gate = _dot_nt(km_ref[g], q, precision=lax.Precision.HIGHEST)
            brow = lax.broadcasted_iota(jnp.int32, (nb, blk), 0)
            rank = jnp.zeros((nb, blk), jnp.int32)
            for jp in range(i):
                gj = gate[jp:jp + 1, :]
                rank = rank + jnp.where((gj > gate) | ((gj == gate) & (jp < brow)), 1, 0)
            bias_of[g, i] = jnp.where(rank < n_sel, 0.0, MASK_VALUE)

    def mask_and_max(g, i):
        m = None
        for c in range((i + 1) * blk // kc):
            keys = slice(offs[i] + c * kc, offs[i] + (c + 1) * kc)
            j = (c * kc) // blk
            sc = s_ref[g, keys, :]
            if j == i:
                sc = jnp.where(kidx + (c * kc - i * blk) <= qidx, sc, MASK_VALUE)
                s_ref[g, keys, :] = sc
            elif (g, i) in bias_of:
                sc = sc + bias_of[g, i][j:j + 1, :]
                s_ref[g, keys, :] = sc
            cm = jnp.max(sc, axis=0, keepdims=True)
            m = cm if m is None else jnp.maximum(m, cm)
        m_ref[g, i] = m

    def weights(g, i):
        m = m_ref[g, i]
        for c in range((i + 1) * blk // kc):
            keys = slice(offs[i] + c * kc, offs[i] + (c + 1) * kc)
            p_ref[g, keys, :] = jnp.exp2(s_ref[g, keys, :] - m).astype(BF16)

    def values(g, i):
        rows = slice(i * blk, (i + 1) * blk)
        acc = _dot(vt_ref[g, :, 0:(i + 1) * blk], p_ref[g, offs[i]:offs[i + 1], :])
        o_ref[rows, head_lanes(g)] = (acc[0:hd, :] / acc[hd:hd + 1, :]).T.astype(BF16)

    for phase in (prepare_block, scores, mask_and_max, weights, values):
        for g in range(heads_per_step):
            for i in range(nb):
                phase(g, i)


def _moba(proj, cos_t, sin_t, *, batch, seq, heads, q_col, k_col, v_col,
          heads_per_step=MOBA_HEADS_PER_STEP):
    m = proj.shape[0]
    hd = ATT_HEAD_DIM
    hw = heads_per_step * hd
    nb = seq // MOBA_BLOCK
    n_sel = min(MOBA_TOPK, nb - 1)
    score_rows = MOBA_BLOCK * nb * (nb + 1) // 2
    table = pl.BlockSpec((seq, hd), lambda b, h: (0, 0))

    def head_group(col):
        return pl.BlockSpec((seq, hw), lambda b, h: (b, col // hw + h))

    g = heads_per_step
    return pl.pallas_call(
        functools.partial(_moba_body, nb=nb, n_sel=n_sel, scale=hd ** -0.5, heads_per_step=g),
        grid=(batch, heads // g),
        in_specs=[head_group(q_col), head_group(k_col), head_group(v_col), table, table],
        out_specs=pl.BlockSpec((seq, hw), lambda b, h: (b, h)),
        out_shape=jax.ShapeDtypeStruct((m, heads * hd), BF16),
        scratch_shapes=[
            pltpu.VMEM((g, seq, hd), BF16),
            pltpu.VMEM((g, hd + BF16_SUBLANES, seq), BF16),
            pltpu.VMEM((g, nb, hd), F32),
            pltpu.VMEM((g, score_rows, MOBA_BLOCK), F32),
            pltpu.VMEM((g, nb, 1, MOBA_BLOCK), F32),
            pltpu.VMEM((g, score_rows, MOBA_BLOCK), BF16),
        ],
        compiler_params=_params("parallel", "parallel"),
        name="moba",
    )(proj, proj, proj, cos_t, sin_t)


def _ret_body(q_ref, k_ref, v_ref, gr_ref, cos_ref, sin_ref, dm_ref, qd_ref, kd_ref, cd_ref,
              ng_ref, o_ref, st_ref, *, heads, dk, dv, chunks):
    c = RET_CHUNK
    per_group = LANES // dk

    @pl.when(pl.program_id(1) == 0)
    def _():
        st_ref[...] = jnp.zeros_like(st_ref)

    lane_head = lax.broadcasted_iota(jnp.int32, (c, LANES), 1) // dk
    rows_of = lambda cc: slice(cc * c, (cc + 1) * c)
    groups = range(heads // per_group)
    units = [(hg, cc, hh) for hg in groups for cc in range(chunks) for hh in range(per_group)]
    head = lambda u: u[0] * per_group + u[2]
    qr, kb, krt = {}, {}, {}
    for hg in groups:
        lanes = slice(hg * LANES, (hg + 1) * LANES)
        for cc in range(chunks):
            cos, sin = cos_ref[rows_of(cc), :], sin_ref[rows_of(cc), :]
            qr[hg, cc] = _rope(q_ref[rows_of(cc), lanes], cos, sin, dk // 2, dk)
            kr = _rope(k_ref[rows_of(cc), lanes], cos, sin, dk // 2, dk) * (dk ** -0.5)
            kb[hg, cc] = kr.astype(BF16)
            krt[hg, cc] = kr.T
    qh = {u: jnp.where(lane_head == u[2], qr[u[0], u[1]], 0.0) for u in units}
    vb = {u: v_ref[rows_of(u[1]), head(u) * dv:(head(u) + 1) * dv].astype(BF16) for u in units}
    inner = {u: _dot_nt(qh[u].astype(BF16), kb[u[0], u[1]]) * dm_ref[head(u)] for u in units}
    ro = {u: _dot(inner[u].astype(BF16), vb[u]) for u in units}
    kv = {u: _dot((krt[u[0], u[1]][u[2] * dk:(u[2] + 1) * dk, :] * kd_ref[head(u)]).astype(BF16), vb[u])
          for u in units}
    for hg in groups:
        st = st_ref[hg]
        for cc in range(chunks):
            stb = st.astype(BF16)
            new_rows = []
            for hh in range(per_group):
                u = (hg, cc, hh)
                h = head(u)
                vl = slice(h * dv, (h + 1) * dv)
                r = ro[u] + _dot((qh[u] * qd_ref[h]).astype(BF16), stb)
                r = r * lax.rsqrt(jnp.mean(r * r, axis=-1, keepdims=True) + EPS) * ng_ref[:, vl]
                o_ref[rows_of(cc), vl] = (_silu(gr_ref[rows_of(cc), vl]) * r).astype(BF16)
                new_rows.append(st[hh * dk:(hh + 1) * dk, :] * cd_ref[h] + kv[u])
            st = jnp.concatenate(new_rows, axis=0)
        st_ref[hg] = st


def _retention(proj, cos_t, sin_t, dm, qd, kd, cd, norm_g, *, layer, batch, seq, heads, dk, dv,
               q_col, k_col, v_col, g_col, chunks=RET_CHUNKS_PER_TILE):
    m = proj.shape[0]
    ts = chunks * RET_CHUNK
    tiles = seq // ts
    qk_w = heads * dk
    v_w = heads * dv

    def tok(width, col):
        return pl.BlockSpec((ts, width), lambda b, t: (b * tiles + t, col // width))

    def const(shape):
        return pl.BlockSpec(shape, lambda b, t: (0,) * len(shape))

    return pl.pallas_call(
        functools.partial(_ret_body, heads=heads, dk=dk, dv=dv, chunks=chunks),
        grid=(batch, tiles),
        in_specs=[
            tok(qk_w, q_col), tok(qk_w, k_col), tok(v_w, v_col), tok(v_w, g_col),
            pl.BlockSpec((ts, LANES), lambda b, t: (t, 0)),
            pl.BlockSpec((ts, LANES), lambda b, t: (t, 0)),
            const(dm.shape), const(qd.shape), const(kd.shape), const(cd.shape),
            _layer_vec(layer, v_w),
        ],
        out_specs=pl.BlockSpec((ts, v_w), lambda b, t: (b * tiles + t, 0)),
        out_shape=jax.ShapeDtypeStruct((m, v_w), BF16),
        scratch_shapes=[pltpu.VMEM((heads * dk // LANES, LANES, dv), F32)],
        compiler_params=_params("parallel", "arbitrary"),
        name="retention",
    )(proj, proj, proj, proj, cos_t, sin_t, dm, qd, kd, cd, norm_g)


def _merge_body(x_ref, yc_ref, ya_ref, yr_ref, wc_ref, wa_ref, wr_ref, g0_ref, g1_ref, g2_ref,
                gb_ref, wo_ref, post_ref, o_ref, mg_ref, *, tn):
    d = o_ref.shape[1]
    for t in range(d // tn):
        cols = slice(t * tn, (t + 1) * tn)
        merged = (jax.nn.sigmoid(g0_ref[:, cols] + gb_ref[0:1, cols]) * _dot(yc_ref[...], wc_ref[:, cols])
                  + jax.nn.sigmoid(g1_ref[:, cols] + gb_ref[1:2, cols]) * _dot(ya_ref[...], wa_ref[:, cols])
                  + jax.nn.sigmoid(g2_ref[:, cols] + gb_ref[2:3, cols]) * _dot(yr_ref[...], wr_ref[:, cols]))
        mg_ref[:, cols] = merged.astype(BF16)
    o_ref[...] = x_ref[...] + _rms(_dot(mg_ref[...], wo_ref[...]), post_ref[...])


def _merge(x, yc, ya, yr, wc, wa, wr, proj, gate_b, wo, post_g, *, layer, gates_col,
           tm=MERGE_ROW_TILE, tn=MERGE_COL_CHUNK):
    m, d = x.shape

    def ytile(a):
        return pl.BlockSpec((tm, a.shape[1]), lambda i: (i, 0))

    def resident(w):
        return pl.BlockSpec((None,) + w.shape[1:], lambda i: (layer, 0, 0),
                            pipeline_mode=pl.Buffered(1))

    def gtile(br):
        return pl.BlockSpec((tm, d), lambda i: (i, gates_col // d + br))

    return pl.pallas_call(
        functools.partial(_merge_body, tn=tn),
        grid=(m // tm,),
        in_specs=[
            pl.BlockSpec((tm, d), lambda i: (i, 0)),
            ytile(yc), ytile(ya), ytile(yr), resident(wc), resident(wa), resident(wr),
            gtile(0), gtile(1), gtile(2),
            pl.BlockSpec((None, N_BRANCH, d), lambda i: (layer, 0, 0)),
            resident(wo),
            _layer_vec(layer, d),
        ],
        out_specs=pl.BlockSpec((tm, d), lambda i: (i, 0)),
        out_shape=jax.ShapeDtypeStruct((m, d), F32),
        scratch_shapes=[pltpu.VMEM((tm, d), BF16)],
        compiler_params=_params("parallel"),
        name="merge",
    )(x, yc, ya, yr, wc, wa, wr, proj, proj, proj, gate_b, wo, post_g)


def _rope_tables(seq, n_rot, theta, group):
    half = n_rot // 2
    inv = 1.0 / (theta ** (jnp.arange(half, dtype=F32) / half))
    ang = jnp.arange(seq, dtype=jnp.int32).astype(F32)[:, None] * inv[None, :]
    cos, sin = jnp.cos(ang), jnp.sin(ang)
    rest = group - n_rot
    cos_g = jnp.concatenate([cos, cos, jnp.ones((seq, rest), F32)], axis=-1)
    sin_g = jnp.concatenate([-sin, sin, jnp.zeros((seq, rest), F32)], axis=-1)
    reps = LANES // group
    return jnp.tile(cos_g, (1, reps)), jnp.tile(sin_g, (1, reps))


def _retention_constants(heads):
    c = RET_CHUNK
    log_g = jnp.log1p(-(2.0 ** (-5.0 - jnp.arange(heads, dtype=F32))))
    idx = jnp.arange(c, dtype=F32)
    diff = idx[:, None] - idx[None, :]
    decay_mask = jnp.exp(jnp.where(diff >= 0, log_g[:, None, None] * diff, -jnp.inf))
    q_decay = jnp.broadcast_to(jnp.exp(log_g[:, None] * (idx + 1.0))[:, :, None], (heads, c, LANES))
    k_decay = jnp.exp(log_g[:, None] * (c - 1.0 - idx))[:, None, :]
    chunk_decay = jnp.broadcast_to(jnp.exp(log_g * c)[:, None, None], (heads, 1, LANES))
    return decay_mask, q_decay, k_decay, chunk_decay


def kernel(x, ffn1_pre_g, ffn1_w_gate, ffn1_w_up, ffn1_w_down, ffn1_post_g, mix_pre_g, w_in, conv_dw_w, conv_dw_b, conv_ln_g, conv_ln_b, ret_norm_g, w_br_conv, w_br_att, w_br_ret, gate_b, w_out, mix_post_g, ffn2_pre_g, ffn2_w_gate, ffn2_w_up, ffn2_w_down, ffn2_post_g):
    batch, seq, d = x.shape
    depth = w_in.shape[0]
    conv_ch = conv_dw_w.shape[2]
    att_w = w_br_att.shape[1]
    ret_v_w = w_br_ret.shape[1]
    att_heads = att_w // ATT_HEAD_DIM
    ret_dv = ret_v_w // RET_HEADS
    ret_dk = ret_dv // 2
    ret_qk_w = RET_HEADS * ret_dk
    sizes = [conv_ch, conv_ch, att_w, att_w, att_w, ret_qk_w, ret_qk_w, ret_v_w, ret_v_w, N_BRANCH * d]
    assert sum(sizes) == w_in.shape[2]
    cols = [0]
    for s in sizes[:-1]:
        cols.append(cols[-1] + s)
    (_, _, qa_col, ka_col, va_col, qr_col, kr_col, vr_col, gr_col, gates_col) = cols

    att_cos, att_sin = _rope_tables(seq, ATT_ROPE_DIMS, ATT_ROPE_THETA, ATT_HEAD_DIM)
    ret_cos, ret_sin = _rope_tables(seq, ret_dk, RET_ROT_THETA, ret_dk)
    dm, qd, kd, cd = _retention_constants(RET_HEADS)

    bf = lambda w: w.astype(BF16)
    rows = lambda v: v.reshape(depth, 1, -1)
    ffn1 = (rows(ffn1_pre_g), ffn1_w_gate, ffn1_w_up, ffn1_w_down, rows(ffn1_post_g))
    ffn2 = (rows(ffn2_pre_g), ffn2_w_gate, ffn2_w_up, ffn2_w_down, rows(ffn2_post_g))
    conv_w = conv_dw_w.reshape(depth, CONV_WIDTH, conv_ch // LANES, LANES).transpose(0, 2, 1, 3)
    conv_p = (conv_w, rows(conv_dw_b), rows(conv_ln_g), rows(conv_ln_b))
    merge_w = (bf(w_br_conv), bf(w_br_att), bf(w_br_ret))
    gate_b3 = gate_b.reshape(depth, N_BRANCH, d)
    wo = bf(w_out)
    mix_pre, mix_post, ret_g = rows(mix_pre_g), rows(mix_post_g), rows(ret_norm_g)

    xf = x.reshape(batch * seq, d)
    for l in range(depth):
        xf = _ffn(xf, *ffn1, layer=l)
        proj = _inproj(xf, mix_pre, w_in, layer=l)
        y_conv = _conv(proj, *conv_p, layer=l, batch=batch, seq=seq, ch=conv_ch)
        y_att = _moba(proj, att_cos, att_sin, batch=batch, seq=seq, heads=att_heads,
                      q_col=qa_col, k_col=ka_col, v_col=va_col)
        y_ret = _retention(proj, ret_cos, ret_sin, dm, qd, kd, cd, ret_g, layer=l,
                           batch=batch, seq=seq, heads=RET_HEADS, dk=ret_dk, dv=ret_dv,
                           q_col=qr_col, k_col=kr_col, v_col=vr_col, g_col=gr_col)
        xf = _merge(xf, y_conv, y_att, y_ret, *merge_w, proj, gate_b3, wo, mix_post,
                    layer=l, gates_col=gates_col)
        xf = _ffn(xf, *ffn2, layer=l)
    return xf.reshape(batch, seq, d)
```

```python
import functools
import math

import jax
import jax.numpy as jnp
from jax import lax
from jax.experimental import pallas as pl
from jax.experimental.pallas import tpu as pltpu

F32 = jnp.float32
BF16 = jnp.bfloat16

EPS = 1e-6
LANES = 128
V7X_VMEM_LIMIT_BYTES = 58 * 1024 * 1024
MASK_VALUE = -1e30
LOG2_E = math.log2(math.e)
BF16_SUBLANES = 16
NORM_ROWS = 16

CONV_WIDTH = 31
CONV_HALO = 32
CONV_NORM_ROWS = 32
ATT_HEAD_DIM = 128
ATT_ROPE_DIMS = ATT_HEAD_DIM // 4
ATT_ROPE_THETA = 500000.0
MOBA_BLOCK = 256
MOBA_TOPK = 3
MOBA_SOFTMAX_ROWS = 128
RET_HEADS = 8
RET_CHUNK = 128
RET_ROT_THETA = 10000.0
N_BRANCH = 3

FFN_ROW_TILE = 1024
FFN_HIDDEN_TILE = 256
INPROJ_ROW_TILE = 2048
INPROJ_COL_TILE = 512
CONV_ROW_TILE = 512
CONV_ACC_ROWS = 128
MOBA_HEADS_PER_STEP = 2
RET_CHUNKS_PER_TILE = 8
MERGE_ROW_TILE = 256
MERGE_COL_CHUNK = 512


def _params(*semantics):
    return pltpu.CompilerParams(dimension_semantics=semantics,
                                vmem_limit_bytes=V7X_VMEM_LIMIT_BYTES)


def _layer_vec(layer, width):
    return pl.BlockSpec((None, 1, width), lambda *_: (layer, 0, 0))


def _rms(x, g):
    return x * lax.rsqrt(jnp.mean(x * x, axis=-1, keepdims=True) + EPS) * g


def _row_chunks(n_rows):
    return [slice(r, r + NORM_ROWS) for r in range(0, n_rows, NORM_ROWS)]


def _silu(x):
    return x * jax.nn.sigmoid(x)


def _dot(a, b):
    return jnp.dot(a, b, preferred_element_type=F32)


def _dot_nt(a, b, precision=None):
    return lax.dot_general(a, b, (((1,), (1,)), ((), ())), precision=precision,
                           preferred_element_type=F32)


def _rope(x, cos, sin_signed, half, group):
    n = x.shape[-1]
    lane = lax.broadcasted_iota(jnp.int32, x.shape, x.ndim - 1)
    first = (lane % group) < half
    partner = jnp.where(first, pltpu.roll(x, n - half, x.ndim - 1), pltpu.roll(x, half, x.ndim - 1))
    return x * cos + partner * sin_signed


def _ffn_body(x_ref, pre_ref, wg_ref, wu_ref, wd_ref, post_ref, o_ref, h_ref):
    j = pl.program_id(1)

    def hidden_tile(first):
        if first:
            for rows in _row_chunks(x_ref.shape[0]):
                h_ref[rows, :] = _rms(x_ref[rows, :], pre_ref[...]).astype(BF16)
        h = h_ref[...]
        g = _dot(h, wg_ref[...].astype(BF16))
        u = _dot(h, wu_ref[...].astype(BF16))
        a = (_silu(g) * u).astype(BF16)
        down = _dot(a, wd_ref[...].astype(BF16))
        if first:
            o_ref[...] = down
        else:
            o_ref[...] += down

    pl.when(j == 0)(lambda: hidden_tile(True))
    pl.when(j > 0)(lambda: hidden_tile(False))

    @pl.when(j == pl.num_programs(1) - 1)
    def _():
        half_post = 0.5 * post_ref[...]
        for rows in _row_chunks(x_ref.shape[0]):
            o_ref[rows, :] = x_ref[rows, :] + _rms(o_ref[rows, :], half_post)


def _ffn(x, pre_g, wg, wu, wd, post_g, *, layer, tm=FFN_ROW_TILE, tf=FFN_HIDDEN_TILE):
    m, d = x.shape
    f = wg.shape[2]
    return pl.pallas_call(
        _ffn_body,
        grid=(m // tm, f // tf),
        in_specs=[
            pl.BlockSpec((tm, d), lambda i, j: (i, 0)),
            _layer_vec(layer, d),
            pl.BlockSpec((None, d, tf), lambda i, j: (layer, 0, j)),
            pl.BlockSpec((None, d, tf), lambda i, j: (layer, 0, j)),
            pl.BlockSpec((None, tf, d), lambda i, j: (layer, j, 0)),
            _layer_vec(layer, d),
        ],
        out_specs=pl.BlockSpec((tm, d), lambda i, j: (i, 0)),
        out_shape=jax.ShapeDtypeStruct((m, d), F32),
        scratch_shapes=[pltpu.VMEM((tm, d), BF16)],
        compiler_params=_params("parallel", "arbitrary"),
        name="ffn",
    )(x, pre_g, wg, wu, wd, post_g)


def _inproj_body(x_ref, g_ref, w_ref, o_ref, h_ref):
    def column_tile(first):
        if first:
            for rows in _row_chunks(x_ref.shape[0]):
                h_ref[rows, :] = _rms(x_ref[rows, :], g_ref[...]).astype(BF16)
        o_ref[...] = _dot(h_ref[...], w_ref[...].astype(BF16))

    j = pl.program_id(1)
    pl.when(j == 0)(lambda: column_tile(True))
    pl.when(j > 0)(lambda: column_tile(False))


def _inproj(x, g, w, *, layer, tm=INPROJ_ROW_TILE, tn=INPROJ_COL_TILE):
    m, d = x.shape
    n = w.shape[2]
    return pl.pallas_call(
        _inproj_body,
        grid=(m // tm, n // tn),
        in_specs=[
            pl.BlockSpec((tm, d), lambda i, j: (i, 0), pipeline_mode=pl.Buffered(1)),
            _layer_vec(layer, d),
            pl.BlockSpec((None, d, tn), lambda i, j: (layer, 0, j)),
        ],
        out_specs=pl.BlockSpec((tm, tn), lambda i, j: (i, j)),
        out_shape=jax.ShapeDtypeStruct((m, n), F32),
        scratch_shapes=[pltpu.VMEM((tm, d), BF16)],
        compiler_params=_params("parallel", "arbitrary"),
        name="inproj",
    )(x, g, w)


def _conv_body(a_ref, g_ref, ap_ref, gp_ref, w_ref, b_ref, lng_ref, lnb_ref, o_ref,
               u_ref, y_ref, *, ts, rows_per_acc):
    nck = u_ref.shape[0]
    row_chunks = [slice(r, r + CONV_NORM_ROWS) for r in range(0, ts, CONV_NORM_ROWS)]

    def store_glu(u, dst_start):
        for c in range(nck):
            u_ref[c, dst_start:dst_start + u.shape[0], :] = u[:, c * LANES:(c + 1) * LANES]

    up = ap_ref[...] * jax.nn.sigmoid(gp_ref[...])
    store_glu(jnp.where(pl.program_id(1) > 0, up, 0.0), 0)
    for rows in row_chunks:
        store_glu(a_ref[rows, :] * jax.nn.sigmoid(g_ref[rows, :]), CONV_HALO + rows.start)

    first = CONV_HALO - (CONV_WIDTH - 1)

    def chunk(c, carry):
        for r in range(ts // rows_per_acc):
            r0 = r * rows_per_acc
            acc = jnp.zeros((rows_per_acc, LANES), F32)
            for w in range(CONV_WIDTH):
                acc = acc + u_ref[c, r0 + first + w:r0 + first + w + rows_per_acc, :] * w_ref[c, w:w + 1, :]
            y_ref[c, r0:r0 + rows_per_acc, :] = acc
        return carry

    lax.fori_loop(0, nck, chunk, 0)

    for rows in row_chunks:
        y = jnp.concatenate([y_ref[c, rows, :] for c in range(nck)], axis=-1) + b_ref[...]
        mu = jnp.mean(y, axis=-1, keepdims=True)
        yc = y - mu
        yn = yc * lax.rsqrt(jnp.mean(yc * yc, axis=-1, keepdims=True) + EPS) * lng_ref[...] + lnb_ref[...]
        o_ref[rows, :] = _silu(yn).astype(BF16)


def _conv(proj, w3, b, ln_g, ln_b, *, layer, batch, seq, ch, ts=CONV_ROW_TILE, rows_per_acc=CONV_ACC_ROWS):
    m = proj.shape[0]
    nck = ch // LANES
    tiles = seq // ts
    halo_per_tile = ts // CONV_HALO

    def cur(col):
        return pl.BlockSpec((ts, ch), lambda bi, i: (bi * tiles + i, col))

    def prev(col):
        return pl.BlockSpec(
            (CONV_HALO, ch),
            lambda bi, i: (jnp.maximum((bi * tiles + i) * halo_per_tile - 1, 0), col))

    vec = _layer_vec(layer, ch)
    return pl.pallas_call(
        functools.partial(_conv_body, ts=ts, rows_per_acc=rows_per_acc),
        grid=(batch, tiles),
        in_specs=[cur(0), cur(1), prev(0), prev(1),
                  pl.BlockSpec((None, nck, CONV_WIDTH, LANES), lambda bi, i: (layer, 0, 0, 0)),
                  vec, vec, vec],
        out_specs=pl.BlockSpec((ts, ch), lambda bi, i: (bi * tiles + i, 0)),
        out_shape=jax.ShapeDtypeStruct((m, ch), BF16),
        scratch_shapes=[pltpu.VMEM((nck, CONV_HALO + ts, LANES), F32),
                        pltpu.VMEM((nck, ts, LANES), F32)],
        compiler_params=_params("parallel", "parallel"),
        name="conv",
    )(proj, proj, proj, proj, w3, b, ln_g, ln_b)


def _moba_body(q_ref, k_ref, v_ref, cos_ref, sin_ref, o_ref, kb_ref, vt_ref, km_ref, s_ref, m_ref, p_ref,
               *, nb, n_sel, scale, heads_per_step):
    blk = MOBA_BLOCK
    hd = ATT_HEAD_DIM
    half = ATT_ROPE_DIMS // 2

    def head_lanes(g):
        return slice(g * hd, (g + 1) * hd)

    def prepare_block(g, j):
        rows = slice(j * blk, (j + 1) * blk)
        kj = _rope(k_ref[rows, head_lanes(g)], cos_ref[rows, :], sin_ref[rows, :], half, ATT_HEAD_DIM)
        kb_ref[g, rows, :] = kj.astype(BF16)
        km_ref[g, j:j + 1, :] = jnp.mean(kj, axis=0, keepdims=True)
        vt_ref[g, 0:hd, rows] = v_ref[rows, head_lanes(g)].T.astype(BF16)
        vt_ref[g, hd:, rows] = jnp.ones((vt_ref.shape[1] - hd, blk), BF16)

    kc = MOBA_SOFTMAX_ROWS
    kidx = lax.broadcasted_iota(jnp.int32, (kc, blk), 0)
    qidx = lax.broadcasted_iota(jnp.int32, (kc, blk), 1)

    offs = [blk * i * (i + 1) // 2 for i in range(nb + 1)]
    bias_of = {}

    def scores(g, i):
        rows = slice(i * blk, (i + 1) * blk)
        nk = (i + 1) * blk
        q = _rope(q_ref[rows, head_lanes(g)], cos_ref[rows, :], sin_ref[rows, :], half, ATT_HEAD_DIM)
        s_ref[g, offs[i]:offs[i + 1], :] = _dot_nt(kb_ref[g, 0:nk, :], (q * (scale * LOG2_E)).astype(BF16))
        if i > n_sel:
            gate = _dot_nt(km_ref[g], q, precision=lax.Precision.HIGHEST)
            brow = lax.broadcasted_iota(jnp.int32, (nb, blk), 0)
            rank = jnp.zeros((nb, blk), jnp.int32)
            for jp in range(i):
                gj = gate[jp:jp + 1, :]
                rank = rank + jnp.where((gj > gate) | ((gj == gate) & (jp < brow)), 1, 0)
            bias_of[g, i] = jnp.where(rank < n_sel, 0.0, MASK_VALUE)

    def mask_and_max(g, i):
        m = None
        for c in range((i + 1) * blk // kc):
            keys = slice(offs[i] + c * kc, offs[i] + (c + 1) * kc)
            j = (c * kc) // blk
            sc = s_ref[g, keys, :]
            if j == i:
                sc = jnp.where(kidx + (c * kc - i * blk) <= qidx, sc, MASK_VALUE)
                s_ref[g, keys, :] = sc
            elif (g, i) in bias_of:
                sc = sc + bias_of[g, i][j:j + 1, :]
                s_ref[g, keys, :] = sc
            cm = jnp.max(sc, axis=0, keepdims=True)
            m = cm if m is None else jnp.maximum(m, cm)
        m_ref[g, i] = m

    def weights(g, i):
        m = m_ref[g, i]
        for c in range((i + 1) * blk // kc):
            keys = slice(offs[i] + c * kc, offs[i] + (c + 1) * kc)
            p_ref[g, keys, :] = jnp.exp2(s_ref[g, keys, :] - m).astype(BF16)

    def values(g, i):
        rows = slice(i * blk, (i + 1) * blk)
        acc = _dot(vt_ref[g, :, 0:(i + 1) * blk], p_ref[g, offs[i]:offs[i + 1], :])
        o_ref[rows, head_lanes(g)] = (acc[0:hd, :] / acc[hd:hd + 1, :]).T.astype(BF16)

    for phase in (prepare_block, scores, mask_and_max, weights, values):
        for g in range(heads_per_step):
            for i in range(nb):
                phase(g, i)


def _moba(proj, cos_t, sin_t, *, batch, seq, heads, q_col, k_col, v_col,
          heads_per_step=MOBA_HEADS_PER_STEP):
    m = proj.shape[0]
    hd = ATT_HEAD_DIM
    hw = heads_per_step * hd
    nb = seq // MOBA_BLOCK
    n_sel = min(MOBA_TOPK, nb - 1)
    score_rows = MOBA_BLOCK * nb * (nb + 1) // 2
    table = pl.BlockSpec((seq, hd), lambda b, h: (0, 0))

    def head_group(col):
        return pl.BlockSpec((seq, hw), lambda b, h: (b, col // hw + h))

    g = heads_per_step
    return pl.pallas_call(
        functools.partial(_moba_body, nb=nb, n_sel=n_sel, scale=hd ** -0.5, heads_per_step=g),
        grid=(batch, heads // g),
        in_specs=[head_group(q_col), head_group(k_col), head_group(v_col), table, table],
        out_specs=pl.BlockSpec((seq, hw), lambda b, h: (b, h)),
        out_shape=jax.ShapeDtypeStruct((m, heads * hd), BF16),
        scratch_shapes=[
            pltpu.VMEM((g, seq, hd), BF16),
            pltpu.VMEM((g, hd + BF16_SUBLANES, seq), BF16),
            pltpu.VMEM((g, nb, hd), F32),
            pltpu.VMEM((g, score_rows, MOBA_BLOCK), F32),
            pltpu.VMEM((g, nb, 1, MOBA_BLOCK), F32),
            pltpu.VMEM((g, score_rows, MOBA_BLOCK), BF16),
        ],
        compiler_params=_params("parallel", "parallel"),
        name="moba",
    )(proj, proj, proj, cos_t, sin_t)


def _ret_body(q_ref, k_ref, v_ref, gr_ref, cos_ref, sin_ref, dm_ref, qd_ref, kd_ref, cd_ref,
              ng_ref, o_ref, st_ref, *, heads, dk, dv, chunks):
    c = RET_CHUNK
    per_group = LANES // dk

    @pl.when(pl.program_id(1) == 0)
    def _():
        st_ref[...] = jnp.zeros_like(st_ref)

    lane_head = lax.broadcasted_iota(jnp.int32, (c, LANES), 1) // dk
    rows_of = lambda cc: slice(cc * c, (cc + 1) * c)
    groups = range(heads // per_group)
    units = [(hg, cc, hh) for hg in groups for cc in range(chunks) for hh in range(per_group)]
    head = lambda u: u[0] * per_group + u[2]
    qr, kb, krt = {}, {}, {}
    for hg in groups:
        lanes = slice(hg * LANES, (hg + 1) * LANES)
        for cc in range(chunks):
            cos, sin = cos_ref[rows_of(cc), :], sin_ref[rows_of(cc), :]
            qr[hg, cc] = _rope(q_ref[rows_of(cc), lanes], cos, sin, dk // 2, dk)
            kr = _rope(k_ref[rows_of(cc), lanes], cos, sin, dk // 2, dk) * (dk ** -0.5)
            kb[hg, cc] = kr.astype(BF16)
            krt[hg, cc] = kr.T
    qh = {u: jnp.where(lane_head == u[2], qr[u[0], u[1]], 0.0) for u in units}
    vb = {u: v_ref[rows_of(u[1]), head(u) * dv:(head(u) + 1) * dv].astype(BF16) for u in units}
    inner = {u: _dot_nt(qh[u].astype(BF16), kb[u[0], u[1]]) * dm_ref[head(u)] for u in units}
    ro = {u: _dot(inner[u].astype(BF16), vb[u]) for u in units}
    kv = {u: _dot((krt[u[0], u[1]][u[2] * dk:(u[2] + 1) * dk, :] * kd_ref[head(u)]).astype(BF16), vb[u])
          for u in units}
    for hg in groups:
        st = st_ref[hg]
        for cc in range(chunks):
            stb = st.astype(BF16)
            new_rows = []
            for hh in range(per_group):
                u = (hg, cc, hh)
                h = head(u)
                vl = slice(h * dv, (h + 1) * dv)
                r = ro[u] + _dot((qh[u] * qd_ref[h]).astype(BF16), stb)
                r = r * lax.rsqrt(jnp.mean(r * r, axis=-1, keepdims=True) + EPS) * ng_ref[:, vl]
                o_ref[rows_of(cc), vl] = (_silu(gr_ref[rows_of(cc), vl]) * r).astype(BF16)
                new_rows.append(st[hh * dk:(hh + 1) * dk, :] * cd_ref[h] + kv[u])
            st = jnp.concatenate(new_rows, axis=0)
        st_ref[hg] = st


def _retention(proj, cos_t, sin_t, dm, qd, kd, cd, norm_g, *, layer, batch, seq, heads, dk, dv,
               q_col, k_col, v_col, g_col, chunks=RET_CHUNKS_PER_TILE):
    m = proj.shape[0]
    ts = chunks * RET_CHUNK
    tiles = seq // ts
    qk_w = heads * dk
    v_w = heads * dv

    def tok(width, col):
        return pl.BlockSpec((ts, width), lambda b, t: (b * tiles + t, col // width))

    def const(shape):
        return pl.BlockSpec(shape, lambda b, t: (0,) * len(shape))

    return pl.pallas_call(
        functools.partial(_ret_body, heads=heads, dk=dk, dv=dv, chunks=chunks),
        grid=(batch, tiles),
        in_specs=[
            tok(qk_w, q_col), tok(qk_w, k_col), tok(v_w, v_col), tok(v_w, g_col),
            pl.BlockSpec((ts, LANES), lambda b, t: (t, 0)),
            pl.BlockSpec((ts, LANES), lambda b, t: (t, 0)),
            const(dm.shape), const(qd.shape), const(kd.shape), const(cd.shape),
            _layer_vec(layer, v_w),
        ],
        out_specs=pl.BlockSpec((ts, v_w), lambda b, t: (b * tiles + t, 0)),
        out_shape=jax.ShapeDtypeStruct((m, v_w), BF16),
        scratch_shapes=[pltpu.VMEM((heads * dk // LANES, LANES, dv), F32)],
        compiler_params=_params("parallel", "arbitrary"),
        name="retention",
    )(proj, proj, proj, proj, cos_t, sin_t, dm, qd, kd, cd, norm_g)


def _merge_body(x_ref, yc_ref, ya_ref, yr_ref, wc_ref, wa_ref, wr_ref, g0_ref, g1_ref, g2_ref,
                gb_ref, wo_ref, post_ref, o_ref, mg_ref, *, tn):
    d = o_ref.shape[1]
    for t in range(d // tn):
        cols = slice(t * tn, (t + 1) * tn)
        merged = (jax.nn.sigmoid(g0_ref[:, cols] + gb_ref[0:1, cols]) * _dot(yc_ref[...], wc_ref[:, cols])
                  + jax.nn.sigmoid(g1_ref[:, cols] + gb_ref[1:2, cols]) * _dot(ya_ref[...], wa_ref[:, cols])
                  + jax.nn.sigmoid(g2_ref[:, cols] + gb_ref[2:3, cols]) * _dot(yr_ref[...], wr_ref[:, cols]))
        mg_ref[:, cols] = merged.astype(BF16)
    o_ref[...] = x_ref[...] + _rms(_dot(mg_ref[...], wo_ref[...]), post_ref[...])


def _merge(x, yc, ya, yr, wc, wa, wr, proj, gate_b, wo, post_g, *, layer, gates_col,
           tm=MERGE_ROW_TILE, tn=MERGE_COL_CHUNK):
    m, d = x.shape

    def ytile(a):
        return pl.BlockSpec((tm, a.shape[1]), lambda i: (i, 0))

    def resident(w):
        return pl.BlockSpec((None,) + w.shape[1:], lambda i: (layer, 0, 0),
                            pipeline_mode=pl.Buffered(1))

    def gtile(br):
        return pl.BlockSpec((tm, d), lambda i: (i, gates_col // d + br))

    return pl.pallas_call(
        functools.partial(_merge_body, tn=tn),
        grid=(m // tm,),
        in_specs=[
            pl.BlockSpec((tm, d), lambda i: (i, 0)),
            ytile(yc), ytile(ya), ytile(yr), resident(wc), resident(wa), resident(wr),
            gtile(0), gtile(1), gtile(2),
            pl.BlockSpec((None, N_BRANCH, d), lambda i: (layer, 0, 0)),
            resident(wo),
            _layer_vec(layer, d),
        ],
        out_specs=pl.BlockSpec((tm, d), lambda i: (i, 0)),
        out_shape=jax.ShapeDtypeStruct((m, d), F32),
        scratch_shapes=[pltpu.VMEM((tm, d), BF16)],
        compiler_params=_params("parallel"),
        name="merge",
    )(x, yc, ya, yr, wc, wa, wr, proj, proj, proj, gate_b, wo, post_g)


def _rope_tables(seq, n_rot, theta, group):
    half = n_rot // 2
    inv = 1.0 / (theta ** (jnp.arange(half, dtype=F32) / half))
    ang = jnp.arange(seq, dtype=jnp.int32).astype(F32)[:, None] * inv[None, :]
    cos, sin = jnp.cos(ang), jnp.sin(ang)
    rest = group - n_rot
    cos_g = jnp.concatenate([cos, cos, jnp.ones((seq, rest), F32)], axis=-1)
    sin_g = jnp.concatenate([-sin, sin, jnp.zeros((seq, rest), F32)], axis=-1)
    reps = LANES // group
    return jnp.tile(cos_g, (1, reps)), jnp.tile(sin_g, (1, reps))


def _retention_constants(heads):
    c = RET_CHUNK
    log_g = jnp.log1p(-(2.0 ** (-5.0 - jnp.arange(heads, dtype=F32))))
    idx = jnp.arange(c, dtype=F32)
    diff = idx[:, None] - idx[None, :]
    decay_mask = jnp.exp(jnp.where(diff >= 0, log_g[:, None, None] * diff, -jnp.inf))
    q_decay = jnp.broadcast_to(jnp.exp(log_g[:, None] * (idx + 1.0))[:, :, None], (heads, c, LANES))
    k_decay = jnp.exp(log_g[:, None] * (c - 1.0 - idx))[:, None, :]
    chunk_decay = jnp.broadcast_to(jnp.exp(log_g * c)[:, None, None], (heads, 1, LANES))
    return decay_mask, q_decay, k_decay, chunk_decay


def kernel(x, ffn1_pre_g, ffn1_w_gate, ffn1_w_up, ffn1_w_down, ffn1_post_g, mix_pre_g, w_in, conv_dw_w, conv_dw_b, conv_ln_g, conv_ln_b, ret_norm_g, w_br_conv, w_br_att, w_br_ret, gate_b, w_out, mix_post_g, ffn2_pre_g, ffn2_w_gate, ffn2_w_up, ffn2_w_down, ffn2_post_g):
    batch, seq, d = x.shape
    depth = w_in.shape[0]
    conv_ch = conv_dw_w.shape[2]
    att_w = w_br_att.shape[1]
    ret_v_w = w_br_ret.shape[1]
    att_heads = att_w // ATT_HEAD_DIM
    ret_dv = ret_v_w // RET_HEADS
    ret_dk = ret_dv // 2
    ret_qk_w = RET_HEADS * ret_dk
    sizes = [conv_ch, conv_ch, att_w, att_w, att_w, ret_qk_w, ret_qk_w, ret_v_w, ret_v_w, N_BRANCH * d]
    assert sum(sizes) == w_in.shape[2]
    cols = [0]
    for s in sizes[:-1]:
        cols.append(cols[-1] + s)
    (_, _, qa_col, ka_col, va_col, qr_col, kr_col, vr_col, gr_col, gates_col) = cols

    att_cos, att_sin = _rope_tables(seq, ATT_ROPE_DIMS, ATT_ROPE_THETA, ATT_HEAD_DIM)
    ret_cos, ret_sin = _rope_tables(seq, ret_dk, RET_ROT_THETA, ret_dk)
    dm, qd, kd, cd = _retention_constants(RET_HEADS)

    bf = lambda w: w.astype(BF16)
    rows = lambda v: v.reshape(depth, 1, -1)
    ffn1 = (rows(ffn1_pre_g), ffn1_w_gate, ffn1_w_up, ffn1_w_down, rows(ffn1_post_g))
    ffn2 = (rows(ffn2_pre_g), ffn2_w_gate, ffn2_w_up, ffn2_w_down, rows(ffn2_post_g))
    conv_w = conv_dw_w.reshape(depth, CONV_WIDTH, conv_ch // LANES, LANES).transpose(0, 2, 1, 3)
    conv_p = (conv_w, rows(conv_dw_b), rows(conv_ln_g), rows(conv_ln_b))
    merge_w = (bf(w_br_conv), bf(w_br_att), bf(w_br_ret))
    gate_b3 = gate_b.reshape(depth, N_BRANCH, d)
    wo = bf(w_out)
    mix_pre, mix_post, ret_g = rows(mix_pre_g), rows(mix_post_g), rows(ret_norm_g)

    xf = x.reshape(batch * seq, d)
    for l in range(depth):
        xf = _ffn(xf, *ffn1, layer=l)
        proj = _inproj(xf, mix_pre, w_in, layer=l)
        y_conv = _conv(proj, *conv_p, layer=l, batch=batch, seq=seq, ch=conv_ch)
        y_att = _moba(proj, att_cos, att_sin, batch=batch, seq=seq, heads=att_heads,
                      q_col=qa_col, k_col=ka_col, v_col=va_col)
        y_ret = _retention(proj, ret_cos, ret_sin, dm, qd, kd, cd, ret_g, layer=l,
                           batch=batch, seq=seq, heads=RET_HEADS, dk=ret_dk, dv=ret_dv,
                           q_col=qr_col, k_col=kr_col, v_col=vr_col, g_col=gr_col)
        xf = _merge(xf, y_conv, y_att, y_ret, *merge_w, proj, gate_b3, wo, mix_post,
                    layer=l, gates_col=gates_col)
        xf = _ffn(xf, *ffn2, layer=l)
    return xf.reshape(batch, seq, d)
```

```python
import functools
import math

import jax
import jax.numpy as jnp
from jax import lax
from jax.experimental import pallas as pl
from jax.experimental.pallas import tpu as pltpu

F32 = jnp.float32
BF16 = jnp.bfloat16

EPS = 1e-6
LANES = 128
V7X_VMEM_LIMIT_BYTES = 58 * 1024 * 1024
MASK_VALUE = -1e30
LOG2_E = math.log2(math.e)
BF16_SUBLANES = 16
NORM_ROWS = 16

CONV_WIDTH = 31
CONV_HALO = 32
CONV_NORM_ROWS = 32
ATT_HEAD_DIM = 128
ATT_ROPE_DIMS = ATT_HEAD_DIM // 4
ATT_ROPE_THETA = 500000.0
MOBA_BLOCK = 256
MOBA_TOPK = 3
MOBA_SOFTMAX_ROWS = 128
RET_HEADS = 8
RET_CHUNK = 128
RET_ROT_THETA = 10000.0
N_BRANCH = 3

FFN_ROW_TILE = 1024
FFN_HIDDEN_TILE = 256
INPROJ_ROW_TILE = 2048
INPROJ_COL_TILE = 512
CONV_ROW_TILE = 512
CONV_ACC_ROWS = 128
MOBA_HEADS_PER_STEP = 2
RET_CHUNKS_PER_TILE = 8
MERGE_ROW_TILE = 256
MERGE_COL_CHUNK = 512


def _params(*semantics):
    return pltpu.CompilerParams(dimension_semantics=semantics,
                                vmem_limit_bytes=V7X_VMEM_LIMIT_BYTES)


def _layer_vec(layer, width):
    return pl.BlockSpec((None, 1, width), lambda *_: (layer, 0, 0))


def _rms(x, g):
    return x * lax.rsqrt(jnp.mean(x * x, axis=-1, keepdims=True) + EPS) * g


def _row_chunks(n_rows):
    return [slice(r, r + NORM_ROWS) for r in range(0, n_rows, NORM_ROWS)]


def _silu(x):
    return x * jax.nn.sigmoid(x)


def _dot(a, b):
    return jnp.dot(a, b, preferred_element_type=F32)


def _dot_nt(a, b, precision=None):
    return lax.dot_general(a, b, (((1,), (1,)), ((), ())), precision=precision,
                           preferred_element_type=F32)


def _rope(x, cos, sin_signed, half, group):
    n = x.shape[-1]
    lane = lax.broadcasted_iota(jnp.int32, x.shape, x.ndim - 1)
    first = (lane % group) < half
    partner = jnp.where(first, pltpu.roll(x, n - half, x.ndim - 1), pltpu.roll(x, half, x.ndim - 1))
    return x * cos + partner * sin_signed


def _ffn_body(x_ref, pre_ref, wg_ref, wu_ref, wd_ref, post_ref, o_ref, h_ref):
    j = pl.program_id(1)

    last = pl.num_programs(1) - 1

    def hidden_tile(first=False, final=False):
        if first:
            for rows in _row_chunks(x_ref.shape[0]):
                h_ref[rows, :] = _rms(x_ref[rows, :], pre_ref[...]).astype(BF16)
        h = h_ref[...]
        g = _dot(h, wg_ref[...].astype(BF16))
        u = _dot(h, wu_ref[...].astype(BF16))
        a = (_silu(g) * u).astype(BF16)
        down = _dot(a, wd_ref[...].astype(BF16))
        if first:
            o_ref[...] = down
        else:
            o_ref[...] += down
        if final:
            half_post = 0.5 * post_ref[...]
            for rows in _row_chunks(x_ref.shape[0]):
                o_ref[rows, :] = x_ref[rows, :] + _rms(o_ref[rows, :], half_post)

    pl.when(j == 0)(lambda: hidden_tile(first=True))
    pl.when((j > 0) & (j < last))(hidden_tile)
    pl.when(j == last)(lambda: hidden_tile(final=True))


def _ffn(x, pre_g, wg, wu, wd, post_g, *, layer, tm=FFN_ROW_TILE, tf=FFN_HIDDEN_TILE):
    m, d = x.shape
    f = wg.shape[2]
    return pl.pallas_call(
        _ffn_body,
        grid=(m // tm, f // tf),
        in_specs=[
            pl.BlockSpec((tm, d), lambda i, j: (i, 0)),
            _layer_vec(layer, d),
            pl.BlockSpec((None, d, tf), lambda i, j: (layer, 0, j)),
            pl.BlockSpec((None, d, tf), lambda i, j: (layer, 0, j)),
            pl.BlockSpec((None, tf, d), lambda i, j: (layer, j, 0)),
            _layer_vec(layer, d),
        ],
        out_specs=pl.BlockSpec((tm, d), lambda i, j: (i, 0)),
        out_shape=jax.ShapeDtypeStruct((m, d), F32),
        scratch_shapes=[pltpu.VMEM((tm, d), BF16)],
        compiler_params=_params("parallel", "arbitrary"),
        name="ffn",
    )(x, pre_g, wg, wu, wd, post_g)


def _inproj_body(x_ref, g_ref, w_ref, o_ref, h_ref):
    def column_tile(first):
        if first:
            for rows in _row_chunks(x_ref.shape[0]):
                h_ref[rows, :] = _rms(x_ref[rows, :], g_ref[...]).astype(BF16)
        o_ref[...] = _dot(h_ref[...], w_ref[...].astype(BF16))

    j = pl.program_id(1)
    pl.when(j == 0)(lambda: column_tile(True))
    pl.when(j > 0)(lambda: column_tile(False))


def _inproj(x, g, w, *, layer, tm=INPROJ_ROW_TILE, tn=INPROJ_COL_TILE):
    m, d = x.shape
    n = w.shape[2]
    return pl.pallas_call(
        _inproj_body,
        grid=(m // tm, n // tn),
        in_specs=[
            pl.BlockSpec((tm, d), lambda i, j: (i, 0), pipeline_mode=pl.Buffered(1)),
            _layer_vec(layer, d),
            pl.BlockSpec((None, d, tn), lambda i, j: (layer, 0, j)),
        ],
        out_specs=pl.BlockSpec((tm, tn), lambda i, j: (i, j)),
        out_shape=jax.ShapeDtypeStruct((m, n), F32),
        scratch_shapes=[pltpu.VMEM((tm, d), BF16)],
        compiler_params=_params("parallel", "arbitrary"),
        name="inproj",
    )(x, g, w)


def _conv_body(a_ref, g_ref, ap_ref, gp_ref, w_ref, b_ref, lng_ref, lnb_ref, o_ref,
               u_ref, y_ref, *, ts, rows_per_acc):
    nck = u_ref.shape[0]
    row_chunks = [slice(r, r + CONV_NORM_ROWS) for r in range(0, ts, CONV_NORM_ROWS)]

    def store_glu(u, dst_start):
        for c in range(nck):
            u_ref[c, dst_start:dst_start + u.shape[0], :] = u[:, c * LANES:(c + 1) * LANES]

    up = ap_ref[...] * jax.nn.sigmoid(gp_ref[...])
    store_glu(jnp.where(pl.program_id(1) > 0, up, 0.0), 0)
    for rows in row_chunks:
        store_glu(a_ref[rows, :] * jax.nn.sigmoid(g_ref[rows, :]), CONV_HALO + rows.start)

    first = CONV_HALO - (CONV_WIDTH - 1)

    def chunk(c, carry):
        for r in range(ts // rows_per_acc):
            r0 = r * rows_per_acc
            acc = jnp.zeros((rows_per_acc, LANES), F32)
            for w in range(CONV_WIDTH):
                acc = acc + u_ref[c, r0 + first + w:r0 + first + w + rows_per_acc, :] * w_ref[c, w:w + 1, :]
            y_ref[c, r0:r0 + rows_per_acc, :] = acc
        return carry

    lax.fori_loop(0, nck, chunk, 0)

    for rows in row_chunks:
        y = jnp.concatenate([y_ref[c, rows, :] for c in range(nck)], axis=-1) + b_ref[...]
        mu = jnp.mean(y, axis=-1, keepdims=True)
        yc = y - mu
        yn = yc * lax.rsqrt(jnp.mean(yc * yc, axis=-1, keepdims=True) + EPS) * lng_ref[...] + lnb_ref[...]
        o_ref[rows, :] = _silu(yn).astype(BF16)


def _conv(proj, w3, b, ln_g, ln_b, *, layer, batch, seq, ch, ts=CONV_ROW_TILE, rows_per_acc=CONV_ACC_ROWS):
    m = proj.shape[0]
    nck = ch // LANES
    tiles = seq // ts
    halo_per_tile = ts // CONV_HALO

    def cur(col):
        return pl.BlockSpec((ts, ch), lambda bi, i: (bi * tiles + i, col))

    def prev(col):
        return pl.BlockSpec(
            (CONV_HALO, ch),
            lambda bi, i: (jnp.maximum((bi * tiles + i) * halo_per_tile - 1, 0), col))

    vec = _layer_vec(layer, ch)
    return pl.pallas_call(
        functools.partial(_conv_body, ts=ts, rows_per_acc=rows_per_acc),
        grid=(batch, tiles),
        in_specs=[cur(0), cur(1), prev(0), prev(1),
                  pl.BlockSpec((None, nck, CONV_WIDTH, LANES), lambda bi, i: (layer, 0, 0, 0)),
                  vec, vec, vec],
        out_specs=pl.BlockSpec((ts, ch), lambda bi, i: (bi * tiles + i, 0)),
        out_shape=jax.ShapeDtypeStruct((m, ch), BF16),
        scratch_shapes=[pltpu.VMEM((nck, CONV_HALO + ts, LANES), F32),
                        pltpu.VMEM((nck, ts, LANES), F32)],
        compiler_params=_params("parallel", "parallel"),
        name="conv",
    )(proj, proj, proj, proj, w3, b, ln_g, ln_b)


def _moba_body(q_ref, k_ref, v_ref, cos_ref, sin_ref, o_ref, kb_ref, vt_ref, km_ref, s_ref, m_ref, p_ref,
               *, nb, n_sel, scale, heads_per_step):
    blk = MOBA_BLOCK
    hd = ATT_HEAD_DIM
    half = ATT_ROPE_DIMS // 2

    def head_lanes(g):
        return slice(g * hd, (g + 1) * hd)

    def prepare_block(g, j):
        rows = slice(j * blk, (j + 1) * blk)
        kj = _rope(k_ref[rows, head_lanes(g)], cos_ref[rows, :], sin_ref[rows, :], half, ATT_HEAD_DIM)
        kb_ref[g, rows, :] = kj.astype(BF16)
        km_ref[g, j:j + 1, :] = jnp.mean(kj, axis=0, keepdims=True)
        vt_ref[g, 0:hd, rows] = v_ref[rows, head_lanes(g)].T.astype(BF16)
        vt_ref[g, hd:, rows] = jnp.ones((vt_ref.shape[1] - hd, blk), BF16)

    kc = MOBA_SOFTMAX_ROWS
    kidx = lax.broadcasted_iota(jnp.int32, (kc, blk), 0)
    qidx = lax.broadcasted_iota(jnp.int32, (kc, blk), 1)

    offs = [blk * i * (i + 1) // 2 for i in range(nb + 1)]
    bias_of = {}

    def scores(g, i):
        rows = slice(i * blk, (i + 1) * blk)
        nk = (i + 1) * blk
        q = _rope(q_ref[rows, head_lanes(g)], cos_ref[rows, :], sin_ref[rows, :], half, ATT_HEAD_DIM)
        s_ref[g, offs[i]:offs[i + 1], :] = _dot_nt(kb_ref[g, 0:nk, :], (q * (scale * LOG2_E)).astype(BF16))
        if i > n_sel:
            gate = _dot_nt(km_ref[g], q, precision=lax.Precision.HIGHEST)
            brow = lax.broadcasted_iota(jnp.int32, (nb, blk), 0)
            rank = jnp.zeros((nb, blk), jnp.int32)
            for jp in range(i):
                gj = gate[jp:jp + 1, :]
                rank = rank + jnp.where((gj > gate) | ((gj == gate) & (jp < brow)), 1, 0)
            bias_of[g, i] = jnp.where(rank < n_sel, 0.0, MASK_VALUE)

    def mask_and_max(g, i):
        m = None
        for c in range((i + 1) * blk // kc):
            keys = slice(offs[i] + c * kc, offs[i] + (c + 1) * kc)
            j = (c * kc) // blk
            sc = s_ref[g, keys, :]
            if j == i:
                sc = jnp.where(kidx + (c * kc - i * blk) <= qidx, sc, MASK_VALUE)
                s_ref[g, keys, :] = sc
            elif (g, i) in bias_of:
                sc = sc + bias_of[g, i][j:j + 1, :]
                s_ref[g, keys, :] = sc
            cm = jnp.max(sc, axis=0, keepdims=True)
            m = cm if m is None else jnp.maximum(m, cm)
        m_ref[g, i] = m

    def weights(g, i):
        m = m_ref[g, i]
        for c in range((i + 1) * blk // kc):
            keys = slice(offs[i] + c * kc, offs[i] + (c + 1) * kc)
            p_ref[g, keys, :] = jnp.exp2(s_ref[g, keys, :] - m).astype(BF16)

    def values(g, i):
        rows = slice(i * blk, (i + 1) * blk)
        acc = _dot(vt_ref[g, :, 0:(i + 1) * blk], p_ref[g, offs[i]:offs[i + 1], :])
        o_ref[rows, head_lanes(g)] = (acc[0:hd, :] / acc[hd:hd + 1, :]).T.astype(BF16)

    for phase in (prepare_block, scores, mask_and_max, weights, values):
        for g in range(heads_per_step):
            for i in range(nb):
                phase(g, i)


def _moba(proj, cos_t, sin_t, *, batch, seq, heads, q_col, k_col, v_col,
          heads_per_step=MOBA_HEADS_PER_STEP):
    m = proj.shape[0]
    hd = ATT_HEAD_DIM
    hw = heads_per_step * hd
    nb = seq // MOBA_BLOCK
    n_sel = min(MOBA_TOPK, nb - 1)
    score_rows = MOBA_BLOCK * nb * (nb + 1) // 2
    table = pl.BlockSpec((seq, hd), lambda b, h: (0, 0))

    def head_group(col):
        return pl.BlockSpec((seq, hw), lambda b, h: (b, col // hw + h))

    g = heads_per_step
    return pl.pallas_call(
        functools.partial(_moba_body, nb=nb, n_sel=n_sel, scale=hd ** -0.5, heads_per_step=g),
        grid=(batch, heads // g),
        in_specs=[head_group(q_col), head_group(k_col), head_group(v_col), table, table],
        out_specs=pl.BlockSpec((seq, hw), lambda b, h: (b, h)),
        out_shape=jax.ShapeDtypeStruct((m, heads * hd), BF16),
        scratch_shapes=[
            pltpu.VMEM((g, seq, hd), BF16),
            pltpu.VMEM((g, hd + BF16_SUBLANES, seq), BF16),
            pltpu.VMEM((g, nb, hd), F32),
            pltpu.VMEM((g, score_rows, MOBA_BLOCK), F32),
            pltpu.VMEM((g, nb, 1, MOBA_BLOCK), F32),
            pltpu.VMEM((g, score_rows, MOBA_BLOCK), BF16),
        ],
        compiler_params=_params("parallel", "parallel"),
        name="moba",
    )(proj, proj, proj, cos_t, sin_t)


def _ret_body(q_ref, k_ref, v_ref, gr_ref, cos_ref, sin_ref, dm_ref, qd_ref, kd_ref, cd_ref,
              ng_ref, o_ref, st_ref, *, heads, dk, dv, chunks):
    c = RET_CHUNK
    per_group = LANES // dk

    @pl.when(pl.program_id(1) == 0)
    def _():
        st_ref[...] = jnp.zeros_like(st_ref)

    lane_head = lax.broadcasted_iota(jnp.int32, (c, LANES), 1) // dk
    rows_of = lambda cc: slice(cc * c, (cc + 1) * c)
    groups = range(heads // per_group)
    units = [(hg, cc, hh) for hg in groups for cc in range(chunks) for hh in range(per_group)]
    head = lambda u: u[0] * per_group + u[2]
    qr, kb, krt = {}, {}, {}
    for hg in groups:
        lanes = slice(hg * LANES, (hg + 1) * LANES)
        for cc in range(chunks):
            cos, sin = cos_ref[rows_of(cc), :], sin_ref[rows_of(cc), :]
            qr[hg, cc] = _rope(q_ref[rows_of(cc), lanes], cos, sin, dk // 2, dk)
            kr = _rope(k_ref[rows_of(cc), lanes], cos, sin, dk // 2, dk) * (dk ** -0.5)
            kb[hg, cc] = kr.astype(BF16)
            krt[hg, cc] = kr.T
    qh = {u: jnp.where(lane_head == u[2], qr[u[0], u[1]], 0.0) for u in units}
    vb = {u: v_ref[rows_of(u[1]), head(u) * dv:(head(u) + 1) * dv].astype(BF16) for u in units}
    inner = {u: _dot_nt(qh[u].astype(BF16), kb[u[0], u[1]]) * dm_ref[head(u)] for u in units}
    ro = {u: _dot(inner[u].astype(BF16), vb[u]) for u in units}
    kv = {u: _dot((krt[u[0], u[1]][u[2] * dk:(u[2] + 1) * dk, :] * kd_ref[head(u)]).astype(BF16), vb[u])
          for u in units}
    for hg in groups:
        st = st_ref[hg]
        for cc in range(chunks):
            stb = st.astype(BF16)
            new_rows = []
            for hh in range(per_group):
                u = (hg, cc, hh)
                h = head(u)
                vl = slice(h * dv, (h + 1) * dv)
                r = ro[u] + _dot((qh[u] * qd_ref[h]).astype(BF16), stb)
                r = r * lax.rsqrt(jnp.mean(r * r, axis=-1, keepdims=True) + EPS) * ng_ref[:, vl]
                o_ref[rows_of(cc), vl] = (_silu(gr_ref[rows_of(cc), vl]) * r).astype(BF16)
                new_rows.append(st[hh * dk:(hh + 1) * dk, :] * cd_ref[h] + kv[u])
            st = jnp.concatenate(new_rows, axis=0)
        st_ref[hg] = st


def _retention(proj, cos_t, sin_t, dm, qd, kd, cd, norm_g, *, layer, batch, seq, heads, dk, dv,
               q_col, k_col, v_col, g_col, chunks=RET_CHUNKS_PER_TILE):
    m = proj.shape[0]
    ts = chunks * RET_CHUNK
    tiles = seq // ts
    qk_w = heads * dk
    v_w = heads * dv

    def tok(width, col):
        return pl.BlockSpec((ts, width), lambda b, t: (b * tiles + t, col // width))

    def const(shape):
        return pl.BlockSpec(shape, lambda b, t: (0,) * len(shape))

    return pl.pallas_call(
        functools.partial(_ret_body, heads=heads, dk=dk, dv=dv, chunks=chunks),
        grid=(batch, tiles),
        in_specs=[
            tok(qk_w, q_col), tok(qk_w, k_col), tok(v_w, v_col), tok(v_w, g_col),
            pl.BlockSpec((ts, LANES), lambda b, t: (t, 0)),
            pl.BlockSpec((ts, LANES), lambda b, t: (t, 0)),
            const(dm.shape), const(qd.shape), const(kd.shape), const(cd.shape),
            _layer_vec(layer, v_w),
        ],
        out_specs=pl.BlockSpec((ts, v_w), lambda b, t: (b * tiles + t, 0)),
        out_shape=jax.ShapeDtypeStruct((m, v_w), BF16),
        scratch_shapes=[pltpu.VMEM((heads * dk // LANES, LANES, dv), F32)],
        compiler_params=_params("parallel", "arbitrary"),
        name="retention",
    )(proj, proj, proj, proj, cos_t, sin_t, dm, qd, kd, cd, norm_g)


def _merge_body(x_ref, yc_ref, ya_ref, yr_ref, wc_ref, wa_ref, wr_ref, g0_ref, g1_ref, g2_ref,
                gb_ref, wo_ref, post_ref, o_ref, mg_ref, *, tn):
    d = o_ref.shape[1]
    for t in range(d // tn):
        cols = slice(t * tn, (t + 1) * tn)
        merged = (jax.nn.sigmoid(g0_ref[:, cols] + gb_ref[0:1, cols]) * _dot(yc_ref[...], wc_ref[:, cols])
                  + jax.nn.sigmoid(g1_ref[:, cols] + gb_ref[1:2, cols]) * _dot(ya_ref[...], wa_ref[:, cols])
                  + jax.nn.sigmoid(g2_ref[:, cols] + gb_ref[2:3, cols]) * _dot(yr_ref[...], wr_ref[:, cols]))
        mg_ref[:, cols] = merged.astype(BF16)
    o_ref[...] = x_ref[...] + _rms(_dot(mg_ref[...], wo_ref[...]), post_ref[...])


def _merge(x, yc, ya, yr, wc, wa, wr, proj, gate_b, wo, post_g, *, layer, gates_col,
           tm=MERGE_ROW_TILE, tn=MERGE_COL_CHUNK):
    m, d = x.shape

    def ytile(a):
        return pl.BlockSpec((tm, a.shape[1]), lambda i: (i, 0))

    def resident(w):
        return pl.BlockSpec((None,) + w.shape[1:], lambda i: (layer, 0, 0),
                            pipeline_mode=pl.Buffered(1))

    def gtile(br):
        return pl.BlockSpec((tm, d), lambda i: (i, gates_col // d + br))

    return pl.pallas_call(
        functools.partial(_merge_body, tn=tn),
        grid=(m // tm,),
        in_specs=[
            pl.BlockSpec((tm, d), lambda i: (i, 0)),
            ytile(yc), ytile(ya), ytile(yr), resident(wc), resident(wa), resident(wr),
            gtile(0), gtile(1), gtile(2),
            pl.BlockSpec((None, N_BRANCH, d), lambda i: (layer, 0, 0)),
            resident(wo),
            _layer_vec(layer, d),
        ],
        out_specs=pl.BlockSpec((tm, d), lambda i: (i, 0)),
        out_shape=jax.ShapeDtypeStruct((m, d), F32),
        scratch_shapes=[pltpu.VMEM((tm, d), BF16)],
        compiler_params=_params("parallel"),
        name="merge",
    )(x, yc, ya, yr, wc, wa, wr, proj, proj, proj, gate_b, wo, post_g)


def _rope_tables(seq, n_rot, theta, group):
    half = n_rot // 2
    inv = 1.0 / (theta ** (jnp.arange(half, dtype=F32) / half))
    ang = jnp.arange(seq, dtype=jnp.int32).astype(F32)[:, None] * inv[None, :]
    cos, sin = jnp.cos(ang), jnp.sin(ang)
    rest = group - n_rot
    cos_g = jnp.concatenate([cos, cos, jnp.ones((seq, rest), F32)], axis=-1)
    sin_g = jnp.concatenate([-sin, sin, jnp.zeros((seq, rest), F32)], axis=-1)
    reps = LANES // group
    return jnp.tile(cos_g, (1, reps)), jnp.tile(sin_g, (1, reps))


def _retention_constants(heads):
    c = RET_CHUNK
    log_g = jnp.log1p(-(2.0 ** (-5.0 - jnp.arange(heads, dtype=F32))))
    idx = jnp.arange(c, dtype=F32)
    diff = idx[:, None] - idx[None, :]
    decay_mask = jnp.exp(jnp.where(diff >= 0, log_g[:, None, None] * diff, -jnp.inf))
    q_decay = jnp.broadcast_to(jnp.exp(log_g[:, None] * (idx + 1.0))[:, :, None], (heads, c, LANES))
    k_decay = jnp.exp(log_g[:, None] * (c - 1.0 - idx))[:, None, :]
    chunk_decay = jnp.broadcast_to(jnp.exp(log_g * c)[:, None, None], (heads, 1, LANES))
    return decay_mask, q_decay, k_decay, chunk_decay


def kernel(x, ffn1_pre_g, ffn1_w_gate, ffn1_w_up, ffn1_w_down, ffn1_post_g, mix_pre_g, w_in, conv_dw_w, conv_dw_b, conv_ln_g, conv_ln_b, ret_norm_g, w_br_conv, w_br_att, w_br_ret, gate_b, w_out, mix_post_g, ffn2_pre_g, ffn2_w_gate, ffn2_w_up, ffn2_w_down, ffn2_post_g):
    batch, seq, d = x.shape
    depth = w_in.shape[0]
    conv_ch = conv_dw_w.shape[2]
    att_w = w_br_att.shape[1]
    ret_v_w = w_br_ret.shape[1]
    att_heads = att_w // ATT_HEAD_DIM
    ret_dv = ret_v_w // RET_HEADS
    ret_dk = ret_dv // 2
    ret_qk_w = RET_HEADS * ret_dk
    sizes = [conv_ch, conv_ch, att_w, att_w, att_w, ret_qk_w, ret_qk_w, ret_v_w, ret_v_w, N_BRANCH * d]
    assert sum(sizes) == w_in.shape[2]
    cols = [0]
    for s in sizes[:-1]:
        cols.append(cols[-1] + s)
    (_, _, qa_col, ka_col, va_col, qr_col, kr_col, vr_col, gr_col, gates_col) = cols

    att_cos, att_sin = _rope_tables(seq, ATT_ROPE_DIMS, ATT_ROPE_THETA, ATT_HEAD_DIM)
    ret_cos, ret_sin = _rope_tables(seq, ret_dk, RET_ROT_THETA, ret_dk)
    dm, qd, kd, cd = _retention_constants(RET_HEADS)

    bf = lambda w: w.astype(BF16)
    rows = lambda v: v.reshape(depth, 1, -1)
    ffn1 = (rows(ffn1_pre_g), ffn1_w_gate, ffn1_w_up, ffn1_w_down, rows(ffn1_post_g))
    ffn2 = (rows(ffn2_pre_g), ffn2_w_gate, ffn2_w_up, ffn2_w_down, rows(ffn2_post_g))
    conv_w = conv_dw_w.reshape(depth, CONV_WIDTH, conv_ch // LANES, LANES).transpose(0, 2, 1, 3)
    conv_p = (conv_w, rows(conv_dw_b), rows(conv_ln_g), rows(conv_ln_b))
    merge_w = (bf(w_br_conv), bf(w_br_att), bf(w_br_ret))
    gate_b3 = gate_b.reshape(depth, N_BRANCH, d)
    wo = bf(w_out)
    mix_pre, mix_post, ret_g = rows(mix_pre_g), rows(mix_post_g), rows(ret_norm_g)

    xf = x.reshape(batch * seq, d)
    for l in range(depth):
        xf = _ffn(xf, *ffn1, layer=l)
        proj = _inproj(xf, mix_pre, w_in, layer=l)
        y_conv = _conv(proj, *conv_p, layer=l, batch=batch, seq=seq, ch=conv_ch)
        y_att = _moba(proj, att_cos, att_sin, batch=batch, seq=seq, heads=att_heads,
                      q_col=qa_col, k_col=ka_col, v_col=va_col)
        y_ret = _retention(proj, ret_cos, ret_sin, dm, qd, kd, cd, ret_g, layer=l,
                           batch=batch, seq=seq, heads=RET_HEADS, dk=ret_dk, dv=ret_dv,
                           q_col=qr_col, k_col=kr_col, v_col=vr_col, g_col=gr_col)
        xf = _merge(xf, y_conv, y_att, y_ret, *merge_w, proj, gate_b3, wo, mix_post,
                    layer=l, gates_col=gates_col)
        xf = _ffn(xf, *ffn2, layer=l)
    return xf.reshape(batch, seq, d)
```

```python
import functools
import math

import jax
import jax.numpy as jnp
from jax import lax
from jax.experimental import pallas as pl
from jax.experimental.pallas import tpu as pltpu

F32 = jnp.float32
BF16 = jnp.bfloat16

EPS = 1e-6
LANES = 128
V7X_VMEM_LIMIT_BYTES = 58 * 1024 * 1024
MASK_VALUE = -1e30
LOG2_E = math.log2(math.e)
BF16_SUBLANES = 16
NORM_ROWS = 16

CONV_WIDTH = 31
CONV_HALO = 32
CONV_NORM_ROWS = 32
ATT_HEAD_DIM = 128
ATT_ROPE_DIMS = ATT_HEAD_DIM // 4
ATT_ROPE_THETA = 500000.0
MOBA_BLOCK = 256
MOBA_TOPK = 3
MOBA_SOFTMAX_ROWS = 128
RET_HEADS = 8
RET_CHUNK = 128
RET_ROT_THETA = 10000.0
N_BRANCH = 3

FFN_ROW_TILE = 1024
FFN_HIDDEN_TILE = 256
INPROJ_ROW_TILE = 2048
INPROJ_COL_TILE = 512
CONV_ROW_TILE = 512
CONV_ACC_ROWS = 128
MOBA_HEADS_PER_STEP = 2
RET_CHUNKS_PER_TILE = 8
MERGE_ROW_TILE = 256
MERGE_COL_CHUNK = 512


def _params(*semantics):
    return pltpu.CompilerParams(dimension_semantics=semantics,
                                vmem_limit_bytes=V7X_VMEM_LIMIT_BYTES)


def _layer_vec(layer, width):
    return pl.BlockSpec((None, 1, width), lambda *_: (layer, 0, 0))


def _rms(x, g):
    return x * lax.rsqrt(jnp.mean(x * x, axis=-1, keepdims=True) + EPS) * g


def _row_chunks(n_rows):
    return [slice(r, r + NORM_ROWS) for r in range(0, n_rows, NORM_ROWS)]


def _silu(x):
    return x * jax.nn.sigmoid(x)


def _dot(a, b):
    return jnp.dot(a, b, preferred_element_type=F32)


def _dot_nt(a, b, precision=None):
    return lax.dot_general(a, b, (((1,), (1,)), ((), ())), precision=precision,
                           preferred_element_type=F32)


def _rope(x, cos, sin_signed, half, group):
    n = x.shape[-1]
    lane = lax.broadcasted_iota(jnp.int32, x.shape, x.ndim - 1)
    first = (lane % group) < half
    partner = jnp.where(first, pltpu.roll(x, n - half, x.ndim - 1), pltpu.roll(x, half, x.ndim - 1))
    return x * cos + partner * sin_signed


def _ffn_body(x_ref, pre_ref, wg_ref, wu_ref, wd_ref, post_ref, o_ref, h_ref):
    j = pl.program_id(1)

    last = pl.num_programs(1) - 1

    def hidden_tile(first=False, final=False):
        if first:
            for rows in _row_chunks(x_ref.shape[0]):
                h_ref[rows, :] = _rms(x_ref[rows, :], pre_ref[...]).astype(BF16)
        h = h_ref[...]
        g = _dot(h, wg_ref[...].astype(BF16))
        u = _dot(h, wu_ref[...].astype(BF16))
        a = (_silu(g) * u).astype(BF16)
        down = _dot(a, wd_ref[...].astype(BF16))
        if first:
            o_ref[...] = down
        else:
            o_ref[...] += down
        if final:
            half_post = 0.5 * post_ref[...]
            for rows in _row_chunks(x_ref.shape[0]):
                o_ref[rows, :] = x_ref[rows, :] + _rms(o_ref[rows, :], half_post)

    pl.when(j == 0)(lambda: hidden_tile(first=True))
    pl.when((j > 0) & (j < last))(hidden_tile)
    pl.when(j == last)(lambda: hidden_tile(final=True))


def _ffn(x, pre_g, wg, wu, wd, post_g, *, layer, tm=FFN_ROW_TILE, tf=FFN_HIDDEN_TILE):
    m, d = x.shape
    f = wg.shape[2]
    return pl.pallas_call(
        _ffn_body,
        grid=(m // tm, f // tf),
        in_specs=[
            pl.BlockSpec((tm, d), lambda i, j: (i, 0)),
            _layer_vec(layer, d),
            pl.BlockSpec((None, d, tf), lambda i, j: (layer, 0, j)),
            pl.BlockSpec((None, d, tf), lambda i, j: (layer, 0, j)),
            pl.BlockSpec((None, tf, d), lambda i, j: (layer, j, 0)),
            _layer_vec(layer, d),
        ],
        out_specs=pl.BlockSpec((tm, d), lambda i, j: (i, 0)),
        out_shape=jax.ShapeDtypeStruct((m, d), F32),
        scratch_shapes=[pltpu.VMEM((tm, d), BF16)],
        compiler_params=_params("parallel", "arbitrary"),
        name="ffn",
    )(x, pre_g, wg, wu, wd, post_g)


def _inproj_body(x_hbm, g_ref, w_ref, o_ref, h_ref, x_ref, x_sem):
    i = pl.program_id(0)
    j = pl.program_id(1)
    tm = x_ref.shape[0]

    def x_copy(tile):
        return pltpu.make_async_copy(x_hbm.at[pl.ds(tile * tm, tm), :], x_ref, x_sem)

    def column_tile(first):
        if first:
            x_copy(i).wait()
            for rows in _row_chunks(tm):
                h_ref[rows, :] = _rms(x_ref[rows, :], g_ref[...]).astype(BF16)
        o_ref[...] = _dot(h_ref[...], w_ref[...].astype(BF16))

    pl.when((i == 0) & (j == 0))(lambda: x_copy(0).start())
    pl.when(j == 0)(lambda: column_tile(True))
    pl.when(j > 0)(lambda: column_tile(False))
    pl.when((j == 1) & (i + 1 < pl.num_programs(0)))(lambda: x_copy(i + 1).start())


def _inproj(x, g, w, *, layer, tm=INPROJ_ROW_TILE, tn=INPROJ_COL_TILE):
    m, d = x.shape
    n = w.shape[2]
    assert n // tn >= 2
    return pl.pallas_call(
        _inproj_body,
        grid=(m // tm, n // tn),
        in_specs=[
            pl.BlockSpec(memory_space=pl.ANY),
            _layer_vec(layer, d),
            pl.BlockSpec((None, d, tn), lambda i, j: (layer, 0, j)),
        ],
        out_specs=pl.BlockSpec((tm, tn), lambda i, j: (i, j)),
        out_shape=jax.ShapeDtypeStruct((m, n), F32),
        scratch_shapes=[pltpu.VMEM((tm, d), BF16), pltpu.VMEM((tm, d), F32), pltpu.SemaphoreType.DMA(())],
        compiler_params=_params("arbitrary", "arbitrary"),
        name="inproj",
    )(x, g, w)


def _conv_body(a_ref, g_ref, ap_ref, gp_ref, w_ref, b_ref, lng_ref, lnb_ref, o_ref,
               u_ref, y_ref, *, ts, rows_per_acc):
    nck = u_ref.shape[0]
    row_chunks = [slice(r, r + CONV_NORM_ROWS) for r in range(0, ts, CONV_NORM_ROWS)]

    def store_glu(u, dst_start):
        for c in range(nck):
            u_ref[c, dst_start:dst_start + u.shape[0], :] = u[:, c * LANES:(c + 1) * LANES]

    up = ap_ref[...] * jax.nn.sigmoid(gp_ref[...])
    store_glu(jnp.where(pl.program_id(1) > 0, up, 0.0), 0)
    for rows in row_chunks:
        store_glu(a_ref[rows, :] * jax.nn.sigmoid(g_ref[rows, :]), CONV_HALO + rows.start)

    first = CONV_HALO - (CONV_WIDTH - 1)

    def chunk(c, carry):
        for r in range(ts // rows_per_acc):
            r0 = r * rows_per_acc
            acc = jnp.zeros((rows_per_acc, LANES), F32)
            for w in range(CONV_WIDTH):
                acc = acc + u_ref[c, r0 + first + w:r0 + first + w + rows_per_acc, :] * w_ref[c, w:w + 1, :]
            y_ref[c, r0:r0 + rows_per_acc, :] = acc
        return carry

    lax.fori_loop(0, nck, chunk, 0)

    for rows in row_chunks:
        y = jnp.concatenate([y_ref[c, rows, :] for c in range(nck)], axis=-1) + b_ref[...]
        mu = jnp.mean(y, axis=-1, keepdims=True)
        yc = y - mu
        yn = yc * lax.rsqrt(jnp.mean(yc * yc, axis=-1, keepdims=True) + EPS) * lng_ref[...] + lnb_ref[...]
        o_ref[rows, :] = _silu(yn).astype(BF16)


def _conv(proj, w3, b, ln_g, ln_b, *, layer, batch, seq, ch, ts=CONV_ROW_TILE, rows_per_acc=CONV_ACC_ROWS):
    m = proj.shape[0]
    nck = ch // LANES
    tiles = seq // ts
    halo_per_tile = ts // CONV_HALO

    def cur(col):
        return pl.BlockSpec((ts, ch), lambda bi, i: (bi * tiles + i, col))

    def prev(col):
        return pl.BlockSpec(
            (CONV_HALO, ch),
            lambda bi, i: (jnp.maximum((bi * tiles + i) * halo_per_tile - 1, 0), col))

    vec = _layer_vec(layer, ch)
    return pl.pallas_call(
        functools.partial(_conv_body, ts=ts, rows_per_acc=rows_per_acc),
        grid=(batch, tiles),
        in_specs=[cur(0), cur(1), prev(0), prev(1),
                  pl.BlockSpec((None, nck, CONV_WIDTH, LANES), lambda bi, i: (layer, 0, 0, 0)),
                  vec, vec, vec],
        out_specs=pl.BlockSpec((ts, ch), lambda bi, i: (bi * tiles + i, 0)),
        out_shape=jax.ShapeDtypeStruct((m, ch), BF16),
        scratch_shapes=[pltpu.VMEM((nck, CONV_HALO + ts, LANES), F32),
                        pltpu.VMEM((nck, ts, LANES), F32)],
        compiler_params=_params("parallel", "parallel"),
        name="conv",
    )(proj, proj, proj, proj, w3, b, ln_g, ln_b)


def _moba_body(q_ref, k_ref, v_ref, cos_ref, sin_ref, o_ref, kb_ref, vt_ref, km_ref, s_ref, m_ref, p_ref,
               *, nb, n_sel, scale, heads_per_step):
    blk = MOBA_BLOCK
    hd = ATT_HEAD_DIM
    half = ATT_ROPE_DIMS // 2

    def head_lanes(g):
        return slice(g * hd, (g + 1) * hd)

    def prepare_block(g, j):
        rows = slice(j * blk, (j + 1) * blk)
        kj = _rope(k_ref[rows, head_lanes(g)], cos_ref[rows, :], sin_ref[rows, :], half, ATT_HEAD_DIM)
        kb_ref[g, rows, :] = kj.astype(BF16)
        km_ref[g, j:j + 1, :] = jnp.mean(kj, axis=0, keepdims=True)
        vt_ref[g, 0:hd, rows] = v_ref[rows, head_lanes(g)].T.astype(BF16)
        vt_ref[g, hd:, rows] = jnp.ones((vt_ref.shape[1] - hd, blk), BF16)

    kc = MOBA_SOFTMAX_ROWS
    kidx = lax.broadcasted_iota(jnp.int32, (kc, blk), 0)
    qidx = lax.broadcasted_iota(jnp.int32, (kc, blk), 1)

    offs = [blk * i * (i + 1) // 2 for i in range(nb + 1)]
    bias_of = {}

    def scores(g, i):
        rows = slice(i * blk, (i + 1) * blk)
        nk = (i + 1) * blk
        q = _rope(q_ref[rows, head_lanes(g)], cos_ref[rows, :], sin_ref[rows, :], half, ATT_HEAD_DIM)
        s_ref[g, offs[i]:offs[i + 1], :] = _dot_nt(kb_ref[g, 0:nk, :], (q * (scale * LOG2_E)).astype(BF16))
        if i > n_sel:
            gate = _dot_nt(km_ref[g], q, precision=lax.Precision.HIGHEST)
            brow = lax.broadcasted_iota(jnp.int32, (nb, blk), 0)
            rank = jnp.zeros((nb, blk), jnp.int32)
            for jp in range(i):
                gj = gate[jp:jp + 1, :]
                rank = rank + jnp.where((gj > gate) | ((gj == gate) & (jp < brow)), 1, 0)
            bias_of[g, i] = jnp.where(rank < n_sel, 0.0, MASK_VALUE)

    def mask_and_max(g, i):
        m = None
        for c in range((i + 1) * blk // kc):
            keys = slice(offs[i] + c * kc, offs[i] + (c + 1) * kc)
            j = (c * kc) // blk
            sc = s_ref[g, keys, :]
            if j == i:
                sc = jnp.where(kidx + (c * kc - i * blk) <= qidx, sc, MASK_VALUE)
                s_ref[g, keys, :] = sc
            elif (g, i) in bias_of:
                sc = sc + bias_of[g, i][j:j + 1, :]
                s_ref[g, keys, :] = sc
            cm = jnp.max(sc, axis=0, keepdims=True)
            m = cm if m is None else jnp.maximum(m, cm)
        m_ref[g, i] = m

    def weights(g, i):
        m = m_ref[g, i]
        for c in range((i + 1) * blk // kc):
            keys = slice(offs[i] + c * kc, offs[i] + (c + 1) * kc)
            p_ref[g, keys, :] = jnp.exp2(s_ref[g, keys, :] - m).astype(BF16)

    def values(g, i):
        rows = slice(i * blk, (i + 1) * blk)
        acc = _dot(vt_ref[g, :, 0:(i + 1) * blk], p_ref[g, offs[i]:offs[i + 1], :])
        o_ref[rows, head_lanes(g)] = (acc[0:hd, :] / acc[hd:hd + 1, :]).T.astype(BF16)

    for phase in (prepare_block, scores, mask_and_max, weights, values):
        for g in range(heads_per_step):
            for i in range(nb):
                phase(g, i)


def _moba(proj, cos_t, sin_t, *, batch, seq, heads, q_col, k_col, v_col,
          heads_per_step=MOBA_HEADS_PER_STEP):
    m = proj.shape[0]
    hd = ATT_HEAD_DIM
    hw = heads_per_step * hd
    nb = seq // MOBA_BLOCK
    n_sel = min(MOBA_TOPK, nb - 1)
    score_rows = MOBA_BLOCK * nb * (nb + 1) // 2
    table = pl.BlockSpec((seq, hd), lambda b, h: (0, 0))

    def head_group(col):
        return pl.BlockSpec((seq, hw), lambda b, h: (b, col // hw + h))

    g = heads_per_step
    return pl.pallas_call(
        functools.partial(_moba_body, nb=nb, n_sel=n_sel, scale=hd ** -0.5, heads_per_step=g),
        grid=(batch, heads // g),
        in_specs=[head_group(q_col), head_group(k_col), head_group(v_col), table, table],
        out_specs=pl.BlockSpec((seq, hw), lambda b, h: (b, h)),
        out_shape=jax.ShapeDtypeStruct((m, heads * hd), BF16),
        scratch_shapes=[
            pltpu.VMEM((g, seq, hd), BF16),
            pltpu.VMEM((g, hd + BF16_SUBLANES, seq), BF16),
            pltpu.VMEM((g, nb, hd), F32),
            pltpu.VMEM((g, score_rows, MOBA_BLOCK), F32),
            pltpu.VMEM((g, nb, 1, MOBA_BLOCK), F32),
            pltpu.VMEM((g, score_rows, MOBA_BLOCK), BF16),
        ],
        compiler_params=_params("parallel", "parallel"),
        name="moba",
    )(proj, proj, proj, cos_t, sin_t)


def _ret_body(q_ref, k_ref, v_ref, gr_ref, cos_ref, sin_ref, dm_ref, qd_ref, kd_ref, cd_ref,
              ng_ref, o_ref, st_ref, *, heads, dk, dv, chunks):
    c = RET_CHUNK
    per_group = LANES // dk

    @pl.when(pl.program_id(1) == 0)
    def _():
        st_ref[...] = jnp.zeros_like(st_ref)

    lane_head = lax.broadcasted_iota(jnp.int32, (c, LANES), 1) // dk
    rows_of = lambda cc: slice(cc * c, (cc + 1) * c)
    groups = range(heads // per_group)
    units = [(hg, cc, hh) for hg in groups for cc in range(chunks) for hh in range(per_group)]
    head = lambda u: u[0] * per_group + u[2]
    qr, kb, krt = {}, {}, {}
    for hg in groups:
        lanes = slice(hg * LANES, (hg + 1) * LANES)
        for cc in range(chunks):
            cos, sin = cos_ref[rows_of(cc), :], sin_ref[rows_of(cc), :]
            qr[hg, cc] = _rope(q_ref[rows_of(cc), lanes], cos, sin, dk // 2, dk)
            kr = _rope(k_ref[rows_of(cc), lanes], cos, sin, dk // 2, dk) * (dk ** -0.5)
            kb[hg, cc] = kr.astype(BF16)
            krt[hg, cc] = kr.T
    qh = {u: jnp.where(lane_head == u[2], qr[u[0], u[1]], 0.0) for u in units}
    vb = {u: v_ref[rows_of(u[1]), head(u) * dv:(head(u) + 1) * dv].astype(BF16) for u in units}
    inner = {u: _dot_nt(qh[u].astype(BF16), kb[u[0], u[1]]) * dm_ref[head(u)] for u in units}
    ro = {u: _dot(inner[u].astype(BF16), vb[u]) for u in units}
    kv = {u: _dot((krt[u[0], u[1]][u[2] * dk:(u[2] + 1) * dk, :] * kd_ref[head(u)]).astype(BF16), vb[u])
          for u in units}
    for hg in groups:
        st = st_ref[hg]
        for cc in range(chunks):
            stb = st.astype(BF16)
            new_rows = []
            for hh in range(per_group):
                u = (hg, cc, hh)
                h = head(u)
                vl = slice(h * dv, (h + 1) * dv)
                r = ro[u] + _dot((qh[u] * qd_ref[h]).astype(BF16), stb)
                r = r * lax.rsqrt(jnp.mean(r * r, axis=-1, keepdims=True) + EPS) * ng_ref[:, vl]
                o_ref[rows_of(cc), vl] = (_silu(gr_ref[rows_of(cc), vl]) * r).astype(BF16)
                new_rows.append(st[hh * dk:(hh + 1) * dk, :] * cd_ref[h] + kv[u])
            st = jnp.concatenate(new_rows, axis=0)
        st_ref[hg] = st


def _retention(proj, cos_t, sin_t, dm, qd, kd, cd, norm_g, *, layer, batch, seq, heads, dk, dv,
               q_col, k_col, v_col, g_col, chunks=RET_CHUNKS_PER_TILE):
    m = proj.shape[0]
    ts = chunks * RET_CHUNK
    tiles = seq // ts
    qk_w = heads * dk
    v_w = heads * dv

    def tok(width, col):
        return pl.BlockSpec((ts, width), lambda b, t: (b * tiles + t, col // width))

    def const(shape):
        return pl.BlockSpec(shape, lambda b, t: (0,) * len(shape))

    return pl.pallas_call(
        functools.partial(_ret_body, heads=heads, dk=dk, dv=dv, chunks=chunks),
        grid=(batch, tiles),
        in_specs=[
            tok(qk_w, q_col), tok(qk_w, k_col), tok(v_w, v_col), tok(v_w, g_col),
            pl.BlockSpec((ts, LANES), lambda b, t: (t, 0)),
            pl.BlockSpec((ts, LANES), lambda b, t: (t, 0)),
            const(dm.shape), const(qd.shape), const(kd.shape), const(cd.shape),
            _layer_vec(layer, v_w),
        ],
        out_specs=pl.BlockSpec((ts, v_w), lambda b, t: (b * tiles + t, 0)),
        out_shape=jax.ShapeDtypeStruct((m, v_w), BF16),
        scratch_shapes=[pltpu.VMEM((heads * dk // LANES, LANES, dv), F32)],
        compiler_params=_params("parallel", "arbitrary"),
        name="retention",
    )(proj, proj, proj, proj, cos_t, sin_t, dm, qd, kd, cd, norm_g)


def _merge_body(x_ref, yc_ref, ya_ref, yr_ref, wc_ref, wa_ref, wr_ref, g0_ref, g1_ref, g2_ref,
                gb_ref, wo_ref, post_ref, o_ref, mg_ref, *, tn):
    d = o_ref.shape[1]
    for t in range(d // tn):
        cols = slice(t * tn, (t + 1) * tn)
        merged = (jax.nn.sigmoid(g0_ref[:, cols] + gb_ref[0:1, cols]) * _dot(yc_ref[...], wc_ref[:, cols])
                  + jax.nn.sigmoid(g1_ref[:, cols] + gb_ref[1:2, cols]) * _dot(ya_ref[...], wa_ref[:, cols])
                  + jax.nn.sigmoid(g2_ref[:, cols] + gb_ref[2:3, cols]) * _dot(yr_ref[...], wr_ref[:, cols]))
        mg_ref[:, cols] = merged.astype(BF16)
    o_ref[...] = x_ref[...] + _rms(_dot(mg_ref[...], wo_ref[...]), post_ref[...])


def _merge(x, yc, ya, yr, wc, wa, wr, proj, gate_b, wo, post_g, *, layer, gates_col,
           tm=MERGE_ROW_TILE, tn=MERGE_COL_CHUNK):
    m, d = x.shape

    def ytile(a):
        return pl.BlockSpec((tm, a.shape[1]), lambda i: (i, 0))

    def resident(w):
        return pl.BlockSpec((None,) + w.shape[1:], lambda i: (layer, 0, 0),
                            pipeline_mode=pl.Buffered(1))

    def gtile(br):
        return pl.BlockSpec((tm, d), lambda i: (i, gates_col // d + br))

    return pl.pallas_call(
        functools.partial(_merge_body, tn=tn),
        grid=(m // tm,),
        in_specs=[
            pl.BlockSpec((tm, d), lambda i: (i, 0)),
            ytile(yc), ytile(ya), ytile(yr), resident(wc), resident(wa), resident(wr),
            gtile(0), gtile(1), gtile(2),
            pl.BlockSpec((None, N_BRANCH, d), lambda i: (layer, 0, 0)),
            resident(wo),
            _layer_vec(layer, d),
        ],
        out_specs=pl.BlockSpec((tm, d), lambda i: (i, 0)),
        out_shape=jax.ShapeDtypeStruct((m, d), F32),
        scratch_shapes=[pltpu.VMEM((tm, d), BF16)],
        compiler_params=_params("parallel"),
        name="merge",
    )(x, yc, ya, yr, wc, wa, wr, proj, proj, proj, gate_b, wo, post_g)


def _rope_tables(seq, n_rot, theta, group):
    half = n_rot // 2
    inv = 1.0 / (theta ** (jnp.arange(half, dtype=F32) / half))
    ang = jnp.arange(seq, dtype=jnp.int32).astype(F32)[:, None] * inv[None, :]
    cos, sin = jnp.cos(ang), jnp.sin(ang)
    rest = group - n_rot
    cos_g = jnp.concatenate([cos, cos, jnp.ones((seq, rest), F32)], axis=-1)
    sin_g = jnp.concatenate([-sin, sin, jnp.zeros((seq, rest), F32)], axis=-1)
    reps = LANES // group
    return jnp.tile(cos_g, (1, reps)), jnp.tile(sin_g, (1, reps))


def _retention_constants(heads):
    c = RET_CHUNK
    log_g = jnp.log1p(-(2.0 ** (-5.0 - jnp.arange(heads, dtype=F32))))
    idx = jnp.arange(c, dtype=F32)
    diff = idx[:, None] - idx[None, :]
    decay_mask = jnp.exp(jnp.where(diff >= 0, log_g[:, None, None] * diff, -jnp.inf))
    q_decay = jnp.broadcast_to(jnp.exp(log_g[:, None] * (idx + 1.0))[:, :, None], (heads, c, LANES))
    k_decay = jnp.exp(log_g[:, None] * (c - 1.0 - idx))[:, None, :]
    chunk_decay = jnp.broadcast_to(jnp.exp(log_g * c)[:, None, None], (heads, 1, LANES))
    return decay_mask, q_decay, k_decay, chunk_decay


def kernel(x, ffn1_pre_g, ffn1_w_gate, ffn1_w_up, ffn1_w_down, ffn1_post_g, mix_pre_g, w_in, conv_dw_w, conv_dw_b, conv_ln_g, conv_ln_b, ret_norm_g, w_br_conv, w_br_att, w_br_ret, gate_b, w_out, mix_post_g, ffn2_pre_g, ffn2_w_gate, ffn2_w_up, ffn2_w_down, ffn2_post_g):
    batch, seq, d = x.shape
    depth = w_in.shape[0]
    conv_ch = conv_dw_w.shape[2]
    att_w = w_br_att.shape[1]
    ret_v_w = w_br_ret.shape[1]
    att_heads = att_w // ATT_HEAD_DIM
    ret_dv = ret_v_w // RET_HEADS
    ret_dk = ret_dv // 2
    ret_qk_w = RET_HEADS * ret_dk
    sizes = [conv_ch, conv_ch, att_w, att_w, att_w, ret_qk_w, ret_qk_w, ret_v_w, ret_v_w, N_BRANCH * d]
    assert sum(sizes) == w_in.shape[2]
    cols = [0]
    for s in sizes[:-1]:
        cols.append(cols[-1] + s)
    (_, _, qa_col, ka_col, va_col, qr_col, kr_col, vr_col, gr_col, gates_col) = cols

    att_cos, att_sin = _rope_tables(seq, ATT_ROPE_DIMS, ATT_ROPE_THETA, ATT_HEAD_DIM)
    ret_cos, ret_sin = _rope_tables(seq, ret_dk, RET_ROT_THETA, ret_dk)
    dm, qd, kd, cd = _retention_constants(RET_HEADS)

    bf = lambda w: w.astype(BF16)
    rows = lambda v: v.reshape(depth, 1, -1)
    ffn1 = (rows(ffn1_pre_g), ffn1_w_gate, ffn1_w_up, ffn1_w_down, rows(ffn1_post_g))
    ffn2 = (rows(ffn2_pre_g), ffn2_w_gate, ffn2_w_up, ffn2_w_down, rows(ffn2_post_g))
    conv_w = conv_dw_w.reshape(depth, CONV_WIDTH, conv_ch // LANES, LANES).transpose(0, 2, 1, 3)
    conv_p = (conv_w, rows(conv_dw_b), rows(conv_ln_g), rows(conv_ln_b))
    merge_w = (bf(w_br_conv), bf(w_br_att), bf(w_br_ret))
    gate_b3 = gate_b.reshape(depth, N_BRANCH, d)
    wo = bf(w_out)
    mix_pre, mix_post, ret_g = rows(mix_pre_g), rows(mix_post_g), rows(ret_norm_g)

    xf = x.reshape(batch * seq, d)
    for l in range(depth):
        xf = _ffn(xf, *ffn1, layer=l)
        proj = _inproj(xf, mix_pre, w_in, layer=l)
        y_conv = _conv(proj, *conv_p, layer=l, batch=batch, seq=seq, ch=conv_ch)
        y_att = _moba(proj, att_cos, att_sin, batch=batch, seq=seq, heads=att_heads,
                      q_col=qa_col, k_col=ka_col, v_col=va_col)
        y_ret = _retention(proj, ret_cos, ret_sin, dm, qd, kd, cd, ret_g, layer=l,
                           batch=batch, seq=seq, heads=RET_HEADS, dk=ret_dk, dv=ret_dv,
                           q_col=qr_col, k_col=kr_col, v_col=vr_col, g_col=gr_col)
        xf = _merge(xf, y_conv, y_att, y_ret, *merge_w, proj, gate_b3, wo, mix_post,
                    layer=l, gates_col=gates_col)
        xf = _ffn(xf, *ffn2, layer=l)
    return xf.reshape(batch, seq, d)
```

```python
import functools
import math

import jax
import jax.numpy as jnp
from jax import lax
from jax.experimental import pallas as pl
from jax.experimental.pallas import tpu as pltpu

F32 = jnp.float32
BF16 = jnp.bfloat16

EPS = 1e-6
LANES = 128
V7X_VMEM_LIMIT_BYTES = 58 * 1024 * 1024
MASK_VALUE = -1e30
LOG2_E = math.log2(math.e)
BF16_SUBLANES = 16
NORM_ROWS = 16

CONV_WIDTH = 31
CONV_HALO = 32
CONV_NORM_ROWS = 32
ATT_HEAD_DIM = 128
ATT_ROPE_DIMS = ATT_HEAD_DIM // 4
ATT_ROPE_THETA = 500000.0
MOBA_BLOCK = 256
MOBA_TOPK = 3
MOBA_SOFTMAX_ROWS = 128
RET_HEADS = 8
RET_CHUNK = 128
RET_ROT_THETA = 10000.0
N_BRANCH = 3

FFN_ROW_TILE = 1024
FFN_HIDDEN_TILE = 256
INPROJ_ROW_TILE = 2048
INPROJ_COL_TILE = 512
CONV_ROW_TILE = 512
CONV_ACC_ROWS = 128
MOBA_HEADS_PER_STEP = 2
RET_CHUNKS_PER_TILE = 8
MERGE_ROW_TILE = 256
MERGE_COL_CHUNK = 512
MERGE_STAGE_ROWS = 256


def _params(*semantics):
    return pltpu.CompilerParams(dimension_semantics=semantics,
                                vmem_limit_bytes=V7X_VMEM_LIMIT_BYTES)


def _layer_vec(layer, width):
    return pl.BlockSpec((None, 1, width), lambda *_: (layer, 0, 0))


def _rms(x, g):
    return x * lax.rsqrt(jnp.mean(x * x, axis=-1, keepdims=True) + EPS) * g


def _row_chunks(n_rows):
    return [slice(r, r + NORM_ROWS) for r in range(0, n_rows, NORM_ROWS)]


def _silu(x):
    return x * jax.nn.sigmoid(x)


def _dot(a, b):
    return jnp.dot(a, b, preferred_element_type=F32)


def _dot_nt(a, b, precision=None):
    return lax.dot_general(a, b, (((1,), (1,)), ((), ())), precision=precision,
                           preferred_element_type=F32)


def _rope(x, cos, sin_signed, half, group):
    n = x.shape[-1]
    lane = lax.broadcasted_iota(jnp.int32, x.shape, x.ndim - 1)
    first = (lane % group) < half
    partner = jnp.where(first, pltpu.roll(x, n - half, x.ndim - 1), pltpu.roll(x, half, x.ndim - 1))
    return x * cos + partner * sin_signed


def _ffn_body(x_ref, pre_ref, wg_ref, wu_ref, wd_ref, post_ref, o_ref, h_ref):
    j = pl.program_id(1)

    last = pl.num_programs(1) - 1

    def hidden_tile(first=False, final=False):
        if first:
            for rows in _row_chunks(x_ref.shape[0]):
                h_ref[rows, :] = _rms(x_ref[rows, :], pre_ref[...]).astype(BF16)
        h = h_ref[...]
        g = _dot(h, wg_ref[...].astype(BF16))
        u = _dot(h, wu_ref[...].astype(BF16))
        a = (_silu(g) * u).astype(BF16)
        down = _dot(a, wd_ref[...].astype(BF16))
        if first:
            o_ref[...] = down
        else:
            o_ref[...] += down
        if final:
            half_post = 0.5 * post_ref[...]
            for rows in _row_chunks(x_ref.shape[0]):
                o_ref[rows, :] = x_ref[rows, :] + _rms(o_ref[rows, :], half_post)

    pl.when(j == 0)(lambda: hidden_tile(first=True))
    pl.when((j > 0) & (j < last))(hidden_tile)
    pl.when(j == last)(lambda: hidden_tile(final=True))


def _ffn(x, pre_g, wg, wu, wd, post_g, *, layer, tm=FFN_ROW_TILE, tf=FFN_HIDDEN_TILE):
    m, d = x.shape
    f = wg.shape[2]
    return pl.pallas_call(
        _ffn_body,
        grid=(m // tm, f // tf),
        in_specs=[
            pl.BlockSpec((tm, d), lambda i, j: (i, 0)),
            _layer_vec(layer, d),
            pl.BlockSpec((None, d, tf), lambda i, j: (layer, 0, j)),
            pl.BlockSpec((None, d, tf), lambda i, j: (layer, 0, j)),
            pl.BlockSpec((None, tf, d), lambda i, j: (layer, j, 0)),
            _layer_vec(layer, d),
        ],
        out_specs=pl.BlockSpec((tm, d), lambda i, j: (i, 0)),
        out_shape=jax.ShapeDtypeStruct((m, d), F32),
        scratch_shapes=[pltpu.VMEM((tm, d), BF16)],
        compiler_params=_params("parallel", "arbitrary"),
        name="ffn",
    )(x, pre_g, wg, wu, wd, post_g)


def _inproj_body(x_hbm, g_ref, w_ref, o_ref, h_ref, x_ref, x_sem):
    i = pl.program_id(0)
    j = pl.program_id(1)
    tm = x_ref.shape[0]

    def x_copy(tile):
        return pltpu.make_async_copy(x_hbm.at[pl.ds(tile * tm, tm), :], x_ref, x_sem)

    def column_tile(first):
        if first:
            x_copy(i).wait()
            for rows in _row_chunks(tm):
                h_ref[rows, :] = _rms(x_ref[rows, :], g_ref[...]).astype(BF16)
        o_ref[...] = _dot(h_ref[...], w_ref[...].astype(BF16))

    pl.when((i == 0) & (j == 0))(lambda: x_copy(0).start())
    pl.when(j == 0)(lambda: column_tile(True))
    pl.when(j > 0)(lambda: column_tile(False))
    pl.when((j == 1) & (i + 1 < pl.num_programs(0)))(lambda: x_copy(i + 1).start())


def _inproj(x, g, w, *, layer, tm=INPROJ_ROW_TILE, tn=INPROJ_COL_TILE):
    m, d = x.shape
    n = w.shape[2]
    assert n // tn >= 2
    return pl.pallas_call(
        _inproj_body,
        grid=(m // tm, n // tn),
        in_specs=[
            pl.BlockSpec(memory_space=pl.ANY),
            _layer_vec(layer, d),
            pl.BlockSpec((None, d, tn), lambda i, j: (layer, 0, j)),
        ],
        out_specs=pl.BlockSpec((tm, tn), lambda i, j: (i, j)),
        out_shape=jax.ShapeDtypeStruct((m, n), F32),
        scratch_shapes=[pltpu.VMEM((tm, d), BF16), pltpu.VMEM((tm, d), F32), pltpu.SemaphoreType.DMA(())],
        compiler_params=_params("arbitrary", "arbitrary"),
        name="inproj",
    )(x, g, w)


def _conv_body(a_ref, g_ref, ap_ref, gp_ref, w_ref, b_ref, lng_ref, lnb_ref, o_ref,
               u_ref, y_ref, *, ts, rows_per_acc):
    nck = u_ref.shape[0]
    row_chunks = [slice(r, r + CONV_NORM_ROWS) for r in range(0, ts, CONV_NORM_ROWS)]

    def store_glu(u, dst_start):
        for c in range(nck):
            u_ref[c, dst_start:dst_start + u.shape[0], :] = u[:, c * LANES:(c + 1) * LANES]

    up = ap_ref[...] * jax.nn.sigmoid(gp_ref[...])
    store_glu(jnp.where(pl.program_id(1) > 0, up, 0.0), 0)
    for rows in row_chunks:
        store_glu(a_ref[rows, :] * jax.nn.sigmoid(g_ref[rows, :]), CONV_HALO + rows.start)

    first = CONV_HALO - (CONV_WIDTH - 1)

    def chunk(c, carry):
        for r in range(ts // rows_per_acc):
            r0 = r * rows_per_acc
            acc = jnp.zeros((rows_per_acc, LANES), F32)
            for w in range(CONV_WIDTH):
                acc = acc + u_ref[c, r0 + first + w:r0 + first + w + rows_per_acc, :] * w_ref[c, w:w + 1, :]
            y_ref[c, r0:r0 + rows_per_acc, :] = acc
        return carry

    lax.fori_loop(0, nck, chunk, 0)

    for rows in row_chunks:
        y = jnp.concatenate([y_ref[c, rows, :] for c in range(nck)], axis=-1) + b_ref[...]
        mu = jnp.mean(y, axis=-1, keepdims=True)
        yc = y - mu
        yn = yc * lax.rsqrt(jnp.mean(yc * yc, axis=-1, keepdims=True) + EPS) * lng_ref[...] + lnb_ref[...]
        o_ref[rows, :] = _silu(yn).astype(BF16)


def _conv(proj, w3, b, ln_g, ln_b, *, layer, batch, seq, ch, ts=CONV_ROW_TILE, rows_per_acc=CONV_ACC_ROWS):
    m = proj.shape[0]
    nck = ch // LANES
    tiles = seq // ts
    halo_per_tile = ts // CONV_HALO

    def cur(col):
        return pl.BlockSpec((ts, ch), lambda bi, i: (bi * tiles + i, col))

    def prev(col):
        return pl.BlockSpec(
            (CONV_HALO, ch),
            lambda bi, i: (jnp.maximum((bi * tiles + i) * halo_per_tile - 1, 0), col))

    vec = _layer_vec(layer, ch)
    return pl.pallas_call(
        functools.partial(_conv_body, ts=ts, rows_per_acc=rows_per_acc),
        grid=(batch, tiles),
        in_specs=[cur(0), cur(1), prev(0), prev(1),
                  pl.BlockSpec((None, nck, CONV_WIDTH, LANES), lambda bi, i: (layer, 0, 0, 0)),
                  vec, vec, vec],
        out_specs=pl.BlockSpec((ts, ch), lambda bi, i: (bi * tiles + i, 0)),
        out_shape=jax.ShapeDtypeStruct((m, ch), BF16),
        scratch_shapes=[pltpu.VMEM((nck, CONV_HALO + ts, LANES), F32),
                        pltpu.VMEM((nck, ts, LANES), F32)],
        compiler_params=_params("parallel", "parallel"),
        name="conv",
    )(proj, proj, proj, proj, w3, b, ln_g, ln_b)


def _moba_body(q_ref, k_ref, v_ref, cos_ref, sin_ref, o_ref, kb_ref, vt_ref, km_ref, s_ref, m_ref, p_ref,
               *, nb, n_sel, scale, heads_per_step):
    blk = MOBA_BLOCK
    hd = ATT_HEAD_DIM
    half = ATT_ROPE_DIMS // 2

    def head_lanes(g):
        return slice(g * hd, (g + 1) * hd)

    def prepare_block(g, j):
        rows = slice(j * blk, (j + 1) * blk)
        kj = _rope(k_ref[rows, head_lanes(g)], cos_ref[rows, :], sin_ref[rows, :], half, ATT_HEAD_DIM)
        kb_ref[g, rows, :] = kj.astype(BF16)
        km_ref[g, j:j + 1, :] = jnp.mean(kj, axis=0, keepdims=True)
        vt_ref[g, 0:hd, rows] = v_ref[rows, head_lanes(g)].T.astype(BF16)
        vt_ref[g, hd:, rows] = jnp.ones((vt_ref.shape[1] - hd, blk), BF16)

    kc = MOBA_SOFTMAX_ROWS
    kidx = lax.broadcasted_iota(jnp.int32, (kc, blk), 0)
    qidx = lax.broadcasted_iota(jnp.int32, (kc, blk), 1)

    offs = [blk * i * (i + 1) // 2 for i in range(nb + 1)]
    bias_of = {}

    def scores(g, i):
        rows = slice(i * blk, (i + 1) * blk)
        nk = (i + 1) * blk
        q = _rope(q_ref[rows, head_lanes(g)], cos_ref[rows, :], sin_ref[rows, :], half, ATT_HEAD_DIM)
        s_ref[g, offs[i]:offs[i + 1], :] = _dot_nt(kb_ref[g, 0:nk, :], (q * (scale * LOG2_E)).astype(BF16))
        if i > n_sel:
            gate = _dot_nt(km_ref[g], q, precision=lax.Precision.HIGHEST)
            brow = lax.broadcasted_iota(jnp.int32, (nb, blk), 0)
            rank = jnp.zeros((nb, blk), jnp.int32)
            for jp in range(i):
                gj = gate[jp:jp + 1, :]
                rank = rank + jnp.where((gj > gate) | ((gj == gate) & (jp < brow)), 1, 0)
            bias_of[g, i] = jnp.where(rank < n_sel, 0.0, MASK_VALUE)

    def mask_and_max(g, i):
        m = None
        for c in range((i + 1) * blk // kc):
            keys = slice(offs[i] + c * kc, offs[i] + (c + 1) * kc)
            j = (c * kc) // blk
            sc = s_ref[g, keys, :]
            if j == i:
                sc = jnp.where(kidx + (c * kc - i * blk) <= qidx, sc, MASK_VALUE)
                s_ref[g, keys, :] = sc
            elif (g, i) in bias_of:
                sc = sc + bias_of[g, i][j:j + 1, :]
                s_ref[g, keys, :] = sc
            cm = jnp.max(sc, axis=0, keepdims=True)
            m = cm if m is None else jnp.maximum(m, cm)
        m_ref[g, i] = m

    def weights(g, i):
        m = m_ref[g, i]
        for c in range((i + 1) * blk // kc):
            keys = slice(offs[i] + c * kc, offs[i] + (c + 1) * kc)
            p_ref[g, keys, :] = jnp.exp2(s_ref[g, keys, :] - m).astype(BF16)

    def values(g, i):
        rows = slice(i * blk, (i + 1) * blk)
        acc = _dot(vt_ref[g, :, 0:(i + 1) * blk], p_ref[g, offs[i]:offs[i + 1], :])
        o_ref[rows, head_lanes(g)] = (acc[0:hd, :] / acc[hd:hd + 1, :]).T.astype(BF16)

    for phase in (prepare_block, scores, mask_and_max, weights, values):
        for g in range(heads_per_step):
            for i in range(nb):
                phase(g, i)


def _moba(proj, cos_t, sin_t, *, batch, seq, heads, q_col, k_col, v_col,
          heads_per_step=MOBA_HEADS_PER_STEP):
    m = proj.shape[0]
    hd = ATT_HEAD_DIM
    hw = heads_per_step * hd
    nb = seq // MOBA_BLOCK
    n_sel = min(MOBA_TOPK, nb - 1)
    score_rows = MOBA_BLOCK * nb * (nb + 1) // 2
    table = pl.BlockSpec((seq, hd), lambda b, h: (0, 0))

    def head_group(col):
        return pl.BlockSpec((seq, hw), lambda b, h: (b, col // hw + h))

    g = heads_per_step
    return pl.pallas_call(
        functools.partial(_moba_body, nb=nb, n_sel=n_sel, scale=hd ** -0.5, heads_per_step=g),
        grid=(batch, heads // g),
        in_specs=[head_group(q_col), head_group(k_col), head_group(v_col), table, table],
        out_specs=pl.BlockSpec((seq, hw), lambda b, h: (b, h)),
        out_shape=jax.ShapeDtypeStruct((m, heads * hd), BF16),
        scratch_shapes=[
            pltpu.VMEM((g, seq, hd), BF16),
            pltpu.VMEM((g, hd + BF16_SUBLANES, seq), BF16),
            pltpu.VMEM((g, nb, hd), F32),
            pltpu.VMEM((g, score_rows, MOBA_BLOCK), F32),
            pltpu.VMEM((g, nb, 1, MOBA_BLOCK), F32),
            pltpu.VMEM((g, score_rows, MOBA_BLOCK), BF16),
        ],
        compiler_params=_params("parallel", "parallel"),
        name="moba",
    )(proj, proj, proj, cos_t, sin_t)


def _ret_body(q_ref, k_ref, v_ref, gr_ref, cos_ref, sin_ref, dm_ref, qd_ref, kd_ref, cd_ref,
              ng_ref, o_ref, st_ref, *, heads, dk, dv, chunks):
    c = RET_CHUNK
    per_group = LANES // dk

    @pl.when(pl.program_id(1) == 0)
    def _():
        st_ref[...] = jnp.zeros_like(st_ref)

    lane_head = lax.broadcasted_iota(jnp.int32, (c, LANES), 1) // dk
    rows_of = lambda cc: slice(cc * c, (cc + 1) * c)
    groups = range(heads // per_group)
    units = [(hg, cc, hh) for hg in groups for cc in range(chunks) for hh in range(per_group)]
    head = lambda u: u[0] * per_group + u[2]
    qr, kb, krt = {}, {}, {}
    for hg in groups:
        lanes = slice(hg * LANES, (hg + 1) * LANES)
        for cc in range(chunks):
            cos, sin = cos_ref[rows_of(cc), :], sin_ref[rows_of(cc), :]
            qr[hg, cc] = _rope(q_ref[rows_of(cc), lanes], cos, sin, dk // 2, dk)
            kr = _rope(k_ref[rows_of(cc), lanes], cos, sin, dk // 2, dk) * (dk ** -0.5)
            kb[hg, cc] = kr.astype(BF16)
            krt[hg, cc] = kr.T
    qh = {u: jnp.where(lane_head == u[2], qr[u[0], u[1]], 0.0) for u in units}
    vb = {u: v_ref[rows_of(u[1]), head(u) * dv:(head(u) + 1) * dv].astype(BF16) for u in units}
    inner = {u: _dot_nt(qh[u].astype(BF16), kb[u[0], u[1]]) * dm_ref[head(u)] for u in units}
    ro = {u: _dot(inner[u].astype(BF16), vb[u]) for u in units}
    kv = {u: _dot((krt[u[0], u[1]][u[2] * dk:(u[2] + 1) * dk, :] * kd_ref[head(u)]).astype(BF16), vb[u])
          for u in units}
    for hg in groups:
        st = st_ref[hg]
        for cc in range(chunks):
            stb = st.astype(BF16)
            new_rows = []
            for hh in range(per_group):
                u = (hg, cc, hh)
                h = head(u)
                vl = slice(h * dv, (h + 1) * dv)
                r = ro[u] + _dot((qh[u] * qd_ref[h]).astype(BF16), stb)
                r = r * lax.rsqrt(jnp.mean(r * r, axis=-1, keepdims=True) + EPS) * ng_ref[:, vl]
                o_ref[rows_of(cc), vl] = (_silu(gr_ref[rows_of(cc), vl]) * r).astype(BF16)
                new_rows.append(st[hh * dk:(hh + 1) * dk, :] * cd_ref[h] + kv[u])
            st = jnp.concatenate(new_rows, axis=0)
        st_ref[hg] = st


def _retention(proj, cos_t, sin_t, dm, qd, kd, cd, norm_g, *, layer, batch, seq, heads, dk, dv,
               q_col, k_col, v_col, g_col, chunks=RET_CHUNKS_PER_TILE):
    m = proj.shape[0]
    ts = chunks * RET_CHUNK
    tiles = seq // ts
    qk_w = heads * dk
    v_w = heads * dv

    def tok(width, col):
        return pl.BlockSpec((ts, width), lambda b, t: (b * tiles + t, col // width))

    def const(shape):
        return pl.BlockSpec(shape, lambda b, t: (0,) * len(shape))

    return pl.pallas_call(
        functools.partial(_ret_body, heads=heads, dk=dk, dv=dv, chunks=chunks),
        grid=(batch, tiles),
        in_specs=[
            tok(qk_w, q_col), tok(qk_w, k_col), tok(v_w, v_col), tok(v_w, g_col),
            pl.BlockSpec((ts, LANES), lambda b, t: (t, 0)),
            pl.BlockSpec((ts, LANES), lambda b, t: (t, 0)),
            const(dm.shape), const(qd.shape), const(kd.shape), const(cd.shape),
            _layer_vec(layer, v_w),
        ],
        out_specs=pl.BlockSpec((ts, v_w), lambda b, t: (b * tiles + t, 0)),
        out_shape=jax.ShapeDtypeStruct((m, v_w), BF16),
        scratch_shapes=[pltpu.VMEM((heads * dk // LANES, LANES, dv), F32)],
        compiler_params=_params("parallel", "arbitrary"),
        name="retention",
    )(proj, proj, proj, proj, cos_t, sin_t, dm, qd, kd, cd, norm_g)


def _merge_body(x_ref, yc_ref, ya_ref, yr_ref, wc_hbm, wa_hbm, wr_hbm, g0_ref, g1_ref, g2_ref,
                gb_ref, wo_hbm, post_ref, o_ref, mg_ref, wc_ref, wa_ref, wr_ref, wo_ref, stage_ref, sems,
                *, tn, layer):
    d = o_ref.shape[1]

    @pl.when(pl.program_id(0) == 0)
    def _():
        rows = stage_ref.shape[1]
        chunks = [(src, dst, r) for src, dst in ((wc_hbm, wc_ref), (wa_hbm, wa_ref), (wr_hbm, wr_ref), (wo_hbm, wo_ref))
                  for r in range(0, dst.shape[0], rows)]

        def chunk_copy(c):
            src, _, r = chunks[c]
            return pltpu.make_async_copy(src.at[layer, pl.ds(r, rows), :], stage_ref.at[c % 2], sems.at[c % 2])

        chunk_copy(0).start()
        for c, (_, dst, r) in enumerate(chunks):
            if c + 1 < len(chunks):
                chunk_copy(c + 1).start()
            chunk_copy(c).wait()
            dst[r:r + rows, :] = stage_ref[c % 2].astype(BF16)

    for t in range(d // tn):
        cols = slice(t * tn, (t + 1) * tn)
        merged = (jax.nn.sigmoid(g0_ref[:, cols] + gb_ref[0:1, cols]) * _dot(yc_ref[...], wc_ref[:, cols])
                  + jax.nn.sigmoid(g1_ref[:, cols] + gb_ref[1:2, cols]) * _dot(ya_ref[...], wa_ref[:, cols])
                  + jax.nn.sigmoid(g2_ref[:, cols] + gb_ref[2:3, cols]) * _dot(yr_ref[...], wr_ref[:, cols]))
        mg_ref[:, cols] = merged.astype(BF16)
    o_ref[...] = x_ref[...] + _rms(_dot(mg_ref[...], wo_ref[...]), post_ref[...])


def _merge(x, yc, ya, yr, wc, wa, wr, proj, gate_b, wo, post_g, *, layer, gates_col,
           tm=MERGE_ROW_TILE, tn=MERGE_COL_CHUNK):
    m, d = x.shape
    hbm = pl.BlockSpec(memory_space=pl.ANY)

    def ytile(a):
        return pl.BlockSpec((tm, a.shape[1]), lambda i: (i, 0))

    def gtile(br):
        return pl.BlockSpec((tm, d), lambda i: (i, gates_col // d + br))

    return pl.pallas_call(
        functools.partial(_merge_body, tn=tn, layer=layer),
        grid=(m // tm,),
        in_specs=[
            pl.BlockSpec((tm, d), lambda i: (i, 0)),
            ytile(yc), ytile(ya), ytile(yr), hbm, hbm, hbm,
            gtile(0), gtile(1), gtile(2),
            pl.BlockSpec((None, N_BRANCH, d), lambda i: (layer, 0, 0)),
            hbm,
            _layer_vec(layer, d),
        ],
        out_specs=pl.BlockSpec((tm, d), lambda i: (i, 0)),
        out_shape=jax.ShapeDtypeStruct((m, d), F32),
        scratch_shapes=[
            pltpu.VMEM((tm, d), BF16),
            pltpu.VMEM(wc.shape[1:], BF16), pltpu.VMEM(wa.shape[1:], BF16), pltpu.VMEM(wr.shape[1:], BF16),
            pltpu.VMEM(wo.shape[1:], BF16),
            pltpu.VMEM((2, MERGE_STAGE_ROWS, d), F32),
            pltpu.SemaphoreType.DMA((2,)),
        ],
        compiler_params=_params("arbitrary"),
        name="merge",
    )(x, yc, ya, yr, wc, wa, wr, proj, proj, proj, gate_b, wo, post_g)


def _rope_tables(seq, n_rot, theta, group):
    half = n_rot // 2
    inv = 1.0 / (theta ** (jnp.arange(half, dtype=F32) / half))
    ang = jnp.arange(seq, dtype=jnp.int32).astype(F32)[:, None] * inv[None, :]
    cos, sin = jnp.cos(ang), jnp.sin(ang)
    rest = group - n_rot
    cos_g = jnp.concatenate([cos, cos, jnp.ones((seq, rest), F32)], axis=-1)
    sin_g = jnp.concatenate([-sin, sin, jnp.zeros((seq, rest), F32)], axis=-1)
    reps = LANES // group
    return jnp.tile(cos_g, (1, reps)), jnp.tile(sin_g, (1, reps))


def _retention_constants(heads):
    c = RET_CHUNK
    log_g = jnp.log1p(-(2.0 ** (-5.0 - jnp.arange(heads, dtype=F32))))
    idx = jnp.arange(c, dtype=F32)
    diff = idx[:, None] - idx[None, :]
    decay_mask = jnp.exp(jnp.where(diff >= 0, log_g[:, None, None] * diff, -jnp.inf))
    q_decay = jnp.broadcast_to(jnp.exp(log_g[:, None] * (idx + 1.0))[:, :, None], (heads, c, LANES))
    k_decay = jnp.exp(log_g[:, None] * (c - 1.0 - idx))[:, None, :]
    chunk_decay = jnp.broadcast_to(jnp.exp(log_g * c)[:, None, None], (heads, 1, LANES))
    return decay_mask, q_decay, k_decay, chunk_decay


def kernel(x, ffn1_pre_g, ffn1_w_gate, ffn1_w_up, ffn1_w_down, ffn1_post_g, mix_pre_g, w_in, conv_dw_w, conv_dw_b, conv_ln_g, conv_ln_b, ret_norm_g, w_br_conv, w_br_att, w_br_ret, gate_b, w_out, mix_post_g, ffn2_pre_g, ffn2_w_gate, ffn2_w_up, ffn2_w_down, ffn2_post_g):
    batch, seq, d = x.shape
    depth = w_in.shape[0]
    conv_ch = conv_dw_w.shape[2]
    att_w = w_br_att.shape[1]
    ret_v_w = w_br_ret.shape[1]
    att_heads = att_w // ATT_HEAD_DIM
    ret_dv = ret_v_w // RET_HEADS
    ret_dk = ret_dv // 2
    ret_qk_w = RET_HEADS * ret_dk
    sizes = [conv_ch, conv_ch, att_w, att_w, att_w, ret_qk_w, ret_qk_w, ret_v_w, ret_v_w, N_BRANCH * d]
    assert sum(sizes) == w_in.shape[2]
    cols = [0]
    for s in sizes[:-1]:
        cols.append(cols[-1] + s)
    (_, _, qa_col, ka_col, va_col, qr_col, kr_col, vr_col, gr_col, gates_col) = cols

    att_cos, att_sin = _rope_tables(seq, ATT_ROPE_DIMS, ATT_ROPE_THETA, ATT_HEAD_DIM)
    ret_cos, ret_sin = _rope_tables(seq, ret_dk, RET_ROT_THETA, ret_dk)
    dm, qd, kd, cd = _retention_constants(RET_HEADS)

    rows = lambda v: v.reshape(depth, 1, -1)
    ffn1 = (rows(ffn1_pre_g), ffn1_w_gate, ffn1_w_up, ffn1_w_down, rows(ffn1_post_g))
    ffn2 = (rows(ffn2_pre_g), ffn2_w_gate, ffn2_w_up, ffn2_w_down, rows(ffn2_post_g))
    conv_w = conv_dw_w.reshape(depth, CONV_WIDTH, conv_ch // LANES, LANES).transpose(0, 2, 1, 3)
    conv_p = (conv_w, rows(conv_dw_b), rows(conv_ln_g), rows(conv_ln_b))
    merge_w = (w_br_conv, w_br_att, w_br_ret)
    gate_b3 = gate_b.reshape(depth, N_BRANCH, d)
    wo = w_out
    mix_pre, mix_post, ret_g = rows(mix_pre_g), rows(mix_post_g), rows(ret_norm_g)

    xf = x.reshape(batch * seq, d)
    for l in range(depth):
        xf = _ffn(xf, *ffn1, layer=l)
        proj = _inproj(xf, mix_pre, w_in, layer=l)
        y_conv = _conv(proj, *conv_p, layer=l, batch=batch, seq=seq, ch=conv_ch)
        y_att = _moba(proj, att_cos, att_sin, batch=batch, seq=seq, heads=att_heads,
                      q_col=qa_col, k_col=ka_col, v_col=va_col)
        y_ret = _retention(proj, ret_cos, ret_sin, dm, qd, kd, cd, ret_g, layer=l,
                           batch=batch, seq=seq, heads=RET_HEADS, dk=ret_dk, dv=ret_dv,
                           q_col=qr_col, k_col=kr_col, v_col=vr_col, g_col=gr_col)
        xf = _merge(xf, y_conv, y_att, y_ret, *merge_w, proj, gate_b3, wo, mix_post,
                    layer=l, gates_col=gates_col)
        xf = _ffn(xf, *ffn2, layer=l)
    return xf.reshape(batch, seq, d)
```

```python
import functools
import math

import jax
import jax.numpy as jnp
from jax import lax
from jax.experimental import pallas as pl
from jax.experimental.pallas import tpu as pltpu

F32 = jnp.float32
BF16 = jnp.bfloat16

EPS = 1e-6
LANES = 128
V7X_VMEM_LIMIT_BYTES = 58 * 1024 * 1024
MASK_VALUE = -1e30
LOG2_E = math.log2(math.e)
BF16_SUBLANES = 16
NORM_ROWS = 16

CONV_WIDTH = 31
CONV_HALO = 32
CONV_NORM_ROWS = 32
ATT_HEAD_DIM = 128
ATT_ROPE_DIMS = ATT_HEAD_DIM // 4
ATT_ROPE_THETA = 500000.0
MOBA_BLOCK = 256
MOBA_TOPK = 3
MOBA_SOFTMAX_ROWS = 128
RET_HEADS = 8
RET_CHUNK = 128
RET_ROT_THETA = 10000.0
N_BRANCH = 3

FFN_ROW_TILE = 1024
FFN_HIDDEN_TILE = 256
INPROJ_ROW_TILE = 2048
INPROJ_COL_TILE = 512
CONV_ROW_TILE = 512
CONV_ACC_ROWS = 128
MOBA_HEADS_PER_STEP = 2
RET_CHUNKS_PER_TILE = 8
MERGE_ROW_TILE = 256
MERGE_COL_CHUNK = 512
MERGE_STAGE_ROWS = 256


def _params(*semantics):
    return pltpu.CompilerParams(dimension_semantics=semantics,
                                vmem_limit_bytes=V7X_VMEM_LIMIT_BYTES)


def _layer_vec(layer, width):
    return pl.BlockSpec((None, 1, width), lambda *_: (layer, 0, 0))


def _rms(x, g):
    return x * lax.rsqrt(jnp.mean(x * x, axis=-1, keepdims=True) + EPS) * g


def _row_chunks(n_rows):
    return [slice(r, r + NORM_ROWS) for r in range(0, n_rows, NORM_ROWS)]


def _silu(x):
    return x * jax.nn.sigmoid(x)


def _dot(a, b):
    return jnp.dot(a, b, preferred_element_type=F32)


def _dot_nt(a, b, precision=None):
    return lax.dot_general(a, b, (((1,), (1,)), ((), ())), precision=precision,
                           preferred_element_type=F32)


def _rope(x, cos, sin_signed, half, group):
    n = x.shape[-1]
    lane = lax.broadcasted_iota(jnp.int32, x.shape, x.ndim - 1)
    first = (lane % group) < half
    partner = jnp.where(first, pltpu.roll(x, n - half, x.ndim - 1), pltpu.roll(x, half, x.ndim - 1))
    return x * cos + partner * sin_signed


def _ffn_body(x_ref, pre_ref, wg_ref, wu_ref, wd_ref, post_ref, o_ref, h_ref):
    j = pl.program_id(1)

    last = pl.num_programs(1) - 1

    def hidden_tile(first=False, final=False):
        if first:
            for rows in _row_chunks(x_ref.shape[0]):
                h_ref[rows, :] = _rms(x_ref[rows, :], pre_ref[...]).astype(BF16)
        h = h_ref[...]
        g = _dot(h, wg_ref[...].astype(BF16))
        u = _dot(h, wu_ref[...].astype(BF16))
        a = (_silu(g) * u).astype(BF16)
        down = _dot(a, wd_ref[...].astype(BF16))
        if first:
            o_ref[...] = down
        else:
            o_ref[...] += down
        if final:
            half_post = 0.5 * post_ref[...]
            for rows in _row_chunks(x_ref.shape[0]):
                o_ref[rows, :] = x_ref[rows, :] + _rms(o_ref[rows, :], half_post)

    pl.when(j == 0)(lambda: hidden_tile(first=True))
    pl.when((j > 0) & (j < last))(hidden_tile)
    pl.when(j == last)(lambda: hidden_tile(final=True))


def _ffn(x, pre_g, wg, wu, wd, post_g, *, layer, tm=FFN_ROW_TILE, tf=FFN_HIDDEN_TILE):
    m, d = x.shape
    f = wg.shape[2]
    tile_specs = [
        pl.BlockSpec((tm, d), lambda i, j: (i, 0)),
        _layer_vec(layer, d),
        pl.BlockSpec((None, d, tf), lambda i, j: (layer, 0, j)),
        pl.BlockSpec((None, d, tf), lambda i, j: (layer, 0, j)),
        pl.BlockSpec((None, tf, d), lambda i, j: (layer, j, 0)),
        _layer_vec(layer, d),
    ]
    out_tile = pl.BlockSpec((tm, d), lambda i, j: (i, 0))

    def call_body(x_hbm, pre_hbm, wg_hbm, wu_hbm, wd_hbm, post_hbm, o_hbm, h_ref):
        tile_body = functools.partial(_ffn_body, h_ref=h_ref)
        pltpu.emit_pipeline(tile_body, grid=(m // tm, f // tf), in_specs=tile_specs, out_specs=[out_tile])(
            x_hbm, pre_hbm, wg_hbm, wu_hbm, wd_hbm, post_hbm, o_hbm)

    hbm = pl.BlockSpec(memory_space=pl.ANY)
    return pl.pallas_call(
        call_body,
        in_specs=[hbm] * 6,
        out_specs=hbm,
        out_shape=jax.ShapeDtypeStruct((m, d), F32),
        scratch_shapes=[pltpu.VMEM((tm, d), BF16)],
        compiler_params=pltpu.CompilerParams(vmem_limit_bytes=V7X_VMEM_LIMIT_BYTES),
        name="ffn",
    )(x, pre_g, wg, wu, wd, post_g)


def _inproj_body(x_hbm, g_ref, w_ref, o_ref, h_ref, x_ref, x_sem):
    i = pl.program_id(0)
    j = pl.program_id(1)
    tm = x_ref.shape[0]

    def x_copy(tile):
        return pltpu.make_async_copy(x_hbm.at[pl.ds(tile * tm, tm), :], x_ref, x_sem)

    def column_tile(first):
        if first:
            x_copy(i).wait()
            for rows in _row_chunks(tm):
                h_ref[rows, :] = _rms(x_ref[rows, :], g_ref[...]).astype(BF16)
        o_ref[...] = _dot(h_ref[...], w_ref[...].astype(BF16))

    pl.when((i == 0) & (j == 0))(lambda: x_copy(0).start())
    pl.when(j == 0)(lambda: column_tile(True))
    pl.when(j > 0)(lambda: column_tile(False))
    pl.when((j == 1) & (i + 1 < pl.num_programs(0)))(lambda: x_copy(i + 1).start())


def _inproj(x, g, w, *, layer, tm=INPROJ_ROW_TILE, tn=INPROJ_COL_TILE):
    m, d = x.shape
    n = w.shape[2]
    assert n // tn >= 2
    return pl.pallas_call(
        _inproj_body,
        grid=(m // tm, n // tn),
        in_specs=[
            pl.BlockSpec(memory_space=pl.ANY),
            _layer_vec(layer, d),
            pl.BlockSpec((None, d, tn), lambda i, j: (layer, 0, j)),
        ],
        out_specs=pl.BlockSpec((tm, tn), lambda i, j: (i, j)),
        out_shape=jax.ShapeDtypeStruct((m, n), F32),
        scratch_shapes=[pltpu.VMEM((tm, d), BF16), pltpu.VMEM((tm, d), F32), pltpu.SemaphoreType.DMA(())],
        compiler_params=_params("arbitrary", "arbitrary"),
        name="inproj",
    )(x, g, w)


def _conv_body(a_ref, g_ref, ap_ref, gp_ref, w_ref, b_ref, lng_ref, lnb_ref, o_ref,
               u_ref, y_ref, *, ts, rows_per_acc):
    nck = u_ref.shape[0]
    row_chunks = [slice(r, r + CONV_NORM_ROWS) for r in range(0, ts, CONV_NORM_ROWS)]

    def store_glu(u, dst_start):
        for c in range(nck):
            u_ref[c, dst_start:dst_start + u.shape[0], :] = u[:, c * LANES:(c + 1) * LANES]

    up = ap_ref[...] * jax.nn.sigmoid(gp_ref[...])
    store_glu(jnp.where(pl.program_id(1) > 0, up, 0.0), 0)
    for rows in row_chunks:
        store_glu(a_ref[rows, :] * jax.nn.sigmoid(g_ref[rows, :]), CONV_HALO + rows.start)

    first = CONV_HALO - (CONV_WIDTH - 1)

    def chunk(c, carry):
        for r in range(ts // rows_per_acc):
            r0 = r * rows_per_acc
            acc = jnp.zeros((rows_per_acc, LANES), F32)
            for w in range(CONV_WIDTH):
                acc = acc + u_ref[c, r0 + first + w:r0 + first + w + rows_per_acc, :] * w_ref[c, w:w + 1, :]
            y_ref[c, r0:r0 + rows_per_acc, :] = acc
        return carry

    lax.fori_loop(0, nck, chunk, 0)

    for rows in row_chunks:
        y = jnp.concatenate([y_ref[c, rows, :] for c in range(nck)], axis=-1) + b_ref[...]
        mu = jnp.mean(y, axis=-1, keepdims=True)
        yc = y - mu
        yn = yc * lax.rsqrt(jnp.mean(yc * yc, axis=-1, keepdims=True) + EPS) * lng_ref[...] + lnb_ref[...]
        o_ref[rows, :] = _silu(yn).astype(BF16)


def _conv(proj, w3, b, ln_g, ln_b, *, layer, batch, seq, ch, ts=CONV_ROW_TILE, rows_per_acc=CONV_ACC_ROWS):
    m = proj.shape[0]
    nck = ch // LANES
    tiles = seq // ts
    halo_per_tile = ts // CONV_HALO

    def cur(col):
        return pl.BlockSpec((ts, ch), lambda bi, i: (bi * tiles + i, col))

    def prev(col):
        return pl.BlockSpec(
            (CONV_HALO, ch),
            lambda bi, i: (jnp.maximum((bi * tiles + i) * halo_per_tile - 1, 0), col))

    vec = _layer_vec(layer, ch)
    return pl.pallas_call(
        functools.partial(_conv_body, ts=ts, rows_per_acc=rows_per_acc),
        grid=(batch, tiles),
        in_specs=[cur(0), cur(1), prev(0), prev(1),
                  pl.BlockSpec((None, nck, CONV_WIDTH, LANES), lambda bi, i: (layer, 0, 0, 0)),
                  vec, vec, vec],
        out_specs=pl.BlockSpec((ts, ch), lambda bi, i: (bi * tiles + i, 0)),
        out_shape=jax.ShapeDtypeStruct((m, ch), BF16),
        scratch_shapes=[pltpu.VMEM((nck, CONV_HALO + ts, LANES), F32),
                        pltpu.VMEM((nck, ts, LANES), F32)],
        compiler_params=_params("parallel", "parallel"),
        name="conv",
    )(proj, proj, proj, proj, w3, b, ln_g, ln_b)


def _moba_body(q_ref, k_ref, v_ref, cos_ref, sin_ref, o_ref, kb_ref, vt_ref, km_ref, s_ref, m_ref, p_ref,
               *, nb, n_sel, scale, heads_per_step):
    blk = MOBA_BLOCK
    hd = ATT_HEAD_DIM
    half = ATT_ROPE_DIMS // 2

    def head_lanes(g):
        return slice(g * hd, (g + 1) * hd)

    def prepare_block(g, j):
        rows = slice(j * blk, (j + 1) * blk)
        kj = _rope(k_ref[rows, head_lanes(g)], cos_ref[rows, :], sin_ref[rows, :], half, ATT_HEAD_DIM)
        kb_ref[g, rows, :] = kj.astype(BF16)
        km_ref[g, j:j + 1, :] = jnp.mean(kj, axis=0, keepdims=True)
        vt_ref[g, 0:hd, rows] = v_ref[rows, head_lanes(g)].T.astype(BF16)
        vt_ref[g, hd:, rows] = jnp.ones((vt_ref.shape[1] - hd, blk), BF16)

    kc = MOBA_SOFTMAX_ROWS
    kidx = lax.broadcasted_iota(jnp.int32, (kc, blk), 0)
    qidx = lax.broadcasted_iota(jnp.int32, (kc, blk), 1)

    offs = [blk * i * (i + 1) // 2 for i in range(nb + 1)]
    bias_of = {}

    def scores(g, i):
        rows = slice(i * blk, (i + 1) * blk)
        nk = (i + 1) * blk
        q = _rope(q_ref[rows, head_lanes(g)], cos_ref[rows, :], sin_ref[rows, :], half, ATT_HEAD_DIM)
        s_ref[g, offs[i]:offs[i + 1], :] = _dot_nt(kb_ref[g, 0:nk, :], (q * (scale * LOG2_E)).astype(BF16))
        if i > n_sel:
            gate = _dot_nt(km_ref[g], q, precision=lax.Precision.HIGHEST)
            brow = lax.broadcasted_iota(jnp.int32, (nb, blk), 0)
            rank = jnp.zeros((nb, blk), jnp.int32)
            for jp in range(i):
                gj = gate[jp:jp + 1, :]
                rank = rank + jnp.where((gj > gate) | ((gj == gate) & (jp < brow)), 1, 0)
            bias_of[g, i] = jnp.where(rank < n_sel, 0.0, MASK_VALUE)

    def mask_and_max(g, i):
        m = None
        for c in range((i + 1) * blk // kc):
            keys = slice(offs[i] + c * kc, offs[i] + (c + 1) * kc)
            j = (c * kc) // blk
            sc = s_ref[g, keys, :]
            if j == i:
                sc = jnp.where(kidx + (c * kc - i * blk) <= qidx, sc, MASK_VALUE)
                s_ref[g, keys, :] = sc
            elif (g, i) in bias_of:
                sc = sc + bias_of[g, i][j:j + 1, :]
                s_ref[g, keys, :] = sc
            cm = jnp.max(sc, axis=0, keepdims=True)
            m = cm if m is None else jnp.maximum(m, cm)
        m_ref[g, i] = m

    def weights(g, i):
        m = m_ref[g, i]
        for c in range((i + 1) * blk // kc):
            keys = slice(offs[i] + c * kc, offs[i] + (c + 1) * kc)
            p_ref[g, keys, :] = jnp.exp2(s_ref[g, keys, :] - m).astype(BF16)

    def values(g, i):
        rows = slice(i * blk, (i + 1) * blk)
        acc = _dot(vt_ref[g, :, 0:(i + 1) * blk], p_ref[g, offs[i]:offs[i + 1], :])
        o_ref[rows, head_lanes(g)] = (acc[0:hd, :] / acc[hd:hd + 1, :]).T.astype(BF16)

    for phase in (prepare_block, scores, mask_and_max, weights, values):
        for g in range(heads_per_step):
            for i in range(nb):
                phase(g, i)


def _moba(proj, cos_t, sin_t, *, batch, seq, heads, q_col, k_col, v_col,
          heads_per_step=MOBA_HEADS_PER_STEP):
    m = proj.shape[0]
    hd = ATT_HEAD_DIM
    hw = heads_per_step * hd
    nb = seq // MOBA_BLOCK
    n_sel = min(MOBA_TOPK, nb - 1)
    score_rows = MOBA_BLOCK * nb * (nb + 1) // 2
    table = pl.BlockSpec((seq, hd), lambda b, h: (0, 0))

    def head_group(col):
        return pl.BlockSpec((seq, hw), lambda b, h: (b, col // hw + h))

    g = heads_per_step
    return pl.pallas_call(
        functools.partial(_moba_body, nb=nb, n_sel=n_sel, scale=hd ** -0.5, heads_per_step=g),
        grid=(batch, heads // g),
        in_specs=[head_group(q_col), head_group(k_col), head_group(v_col), table, table],
        out_specs=pl.BlockSpec((seq, hw), lambda b, h: (b, h)),
        out_shape=jax.ShapeDtypeStruct((m, heads * hd), BF16),
        scratch_shapes=[
            pltpu.VMEM((g, seq, hd), BF16),
            pltpu.VMEM((g, hd + BF16_SUBLANES, seq), BF16),
            pltpu.VMEM((g, nb, hd), F32),
            pltpu.VMEM((g, score_rows, MOBA_BLOCK), F32),
            pltpu.VMEM((g, nb, 1, MOBA_BLOCK), F32),
            pltpu.VMEM((g, score_rows, MOBA_BLOCK), BF16),
        ],
        compiler_params=_params("parallel", "parallel"),
        name="moba",
    )(proj, proj, proj, cos_t, sin_t)


def _ret_body(q_ref, k_ref, v_ref, gr_ref, cos_ref, sin_ref, dm_ref, qd_ref, kd_ref, cd_ref,
              ng_ref, o_ref, st_ref, *, heads, dk, dv, chunks):
    c = RET_CHUNK
    per_group = LANES // dk

    @pl.when(pl.program_id(1) == 0)
    def _():
        st_ref[...] = jnp.zeros_like(st_ref)

    lane_head = lax.broadcasted_iota(jnp.int32, (c, LANES), 1) // dk
    rows_of = lambda cc: slice(cc * c, (cc + 1) * c)
    groups = range(heads // per_group)
    units = [(hg, cc, hh) for hg in groups for cc in range(chunks) for hh in range(per_group)]
    head = lambda u: u[0] * per_group + u[2]
    qr, kb, krt = {}, {}, {}
    for hg in groups:
        lanes = slice(hg * LANES, (hg + 1) * LANES)
        for cc in range(chunks):
            cos, sin = cos_ref[rows_of(cc), :], sin_ref[rows_of(cc), :]
            qr[hg, cc] = _rope(q_ref[rows_of(cc), lanes], cos, sin, dk // 2, dk)
            kr = _rope(k_ref[rows_of(cc), lanes], cos, sin, dk // 2, dk) * (dk ** -0.5)
            kb[hg, cc] = kr.astype(BF16)
            krt[hg, cc] = kr.T
    qh = {u: jnp.where(lane_head == u[2], qr[u[0], u[1]], 0.0) for u in units}
    vb = {u: v_ref[rows_of(u[1]), head(u) * dv:(head(u) + 1) * dv].astype(BF16) for u in units}
    inner = {u: _dot_nt(qh[u].astype(BF16), kb[u[0], u[1]]) * dm_ref[head(u)] for u in units}
    ro = {u: _dot(inner[u].astype(BF16), vb[u]) for u in units}
    kv = {u: _dot((krt[u[0], u[1]][u[2] * dk:(u[2] + 1) * dk, :] * kd_ref[head(u)]).astype(BF16), vb[u])
          for u in units}
    for hg in groups:
        st = st_ref[hg]
        for cc in range(chunks):
            stb = st.astype(BF16)
            new_rows = []
            for hh in range(per_group):
                u = (hg, cc, hh)
                h = head(u)
                vl = slice(h * dv, (h + 1) * dv)
                r = ro[u] + _dot((qh[u] * qd_ref[h]).astype(BF16), stb)
                r = r * lax.rsqrt(jnp.mean(r * r, axis=-1, keepdims=True) + EPS) * ng_ref[:, vl]
                o_ref[rows_of(cc), vl] = (_silu(gr_ref[rows_of(cc), vl]) * r).astype(BF16)
                new_rows.append(st[hh * dk:(hh + 1) * dk, :] * cd_ref[h] + kv[u])
            st = jnp.concatenate(new_rows, axis=0)
        st_ref[hg] = st


def _retention(proj, cos_t, sin_t, dm, qd, kd, cd, norm_g, *, layer, batch, seq, heads, dk, dv,
               q_col, k_col, v_col, g_col, chunks=RET_CHUNKS_PER_TILE):
    m = proj.shape[0]
    ts = chunks * RET_CHUNK
    tiles = seq // ts
    qk_w = heads * dk
    v_w = heads * dv

    def tok(width, col):
        return pl.BlockSpec((ts, width), lambda b, t: (b * tiles + t, col // width))

    def const(shape):
        return pl.BlockSpec(shape, lambda b, t: (0,) * len(shape))

    return pl.pallas_call(
        functools.partial(_ret_body, heads=heads, dk=dk, dv=dv, chunks=chunks),
        grid=(batch, tiles),
        in_specs=[
            tok(qk_w, q_col), tok(qk_w, k_col), tok(v_w, v_col), tok(v_w, g_col),
            pl.BlockSpec((ts, LANES), lambda b, t: (t, 0)),
            pl.BlockSpec((ts, LANES), lambda b, t: (t, 0)),
            const(dm.shape), const(qd.shape), const(kd.shape), const(cd.shape),
            _layer_vec(layer, v_w),
        ],
        out_specs=pl.BlockSpec((ts, v_w), lambda b, t: (b * tiles + t, 0)),
        out_shape=jax.ShapeDtypeStruct((m, v_w), BF16),
        scratch_shapes=[pltpu.VMEM((heads * dk // LANES, LANES, dv), F32)],
        compiler_params=_params("parallel", "arbitrary"),
        name="retention",
    )(proj, proj, proj, proj, cos_t, sin_t, dm, qd, kd, cd, norm_g)


def _merge_body(x_ref, yc_ref, ya_ref, yr_ref, wc_hbm, wa_hbm, wr_hbm, g0_ref, g1_ref, g2_ref,
                gb_ref, wo_hbm, post_ref, o_ref, mg_ref, wc_ref, wa_ref, wr_ref, wo_ref, stage_ref, sems,
                *, tn, layer):
    d = o_ref.shape[1]

    @pl.when(pl.program_id(0) == 0)
    def _():
        rows = stage_ref.shape[1]
        chunks = [(src, dst, r) for src, dst in ((wc_hbm, wc_ref), (wa_hbm, wa_ref), (wr_hbm, wr_ref), (wo_hbm, wo_ref))
                  for r in range(0, dst.shape[0], rows)]

        def chunk_copy(c):
            src, _, r = chunks[c]
            return pltpu.make_async_copy(src.at[layer, pl.ds(r, rows), :], stage_ref.at[c % 2], sems.at[c % 2])

        chunk_copy(0).start()
        for c, (_, dst, r) in enumerate(chunks):
            if c + 1 < len(chunks):
                chunk_copy(c + 1).start()
            chunk_copy(c).wait()
            dst[r:r + rows, :] = stage_ref[c % 2].astype(BF16)

    for t in range(d // tn):
        cols = slice(t * tn, (t + 1) * tn)
        merged = (jax.nn.sigmoid(g0_ref[:, cols] + gb_ref[0:1, cols]) * _dot(yc_ref[...], wc_ref[:, cols])
                  + jax.nn.sigmoid(g1_ref[:, cols] + gb_ref[1:2, cols]) * _dot(ya_ref[...], wa_ref[:, cols])
                  + jax.nn.sigmoid(g2_ref[:, cols] + gb_ref[2:3, cols]) * _dot(yr_ref[...], wr_ref[:, cols]))
        mg_ref[:, cols] = merged.astype(BF16)
    o_ref[...] = x_ref[...] + _rms(_dot(mg_ref[...], wo_ref[...]), post_ref[...])


def _merge(x, yc, ya, yr, wc, wa, wr, proj, gate_b, wo, post_g, *, layer, gates_col,
           tm=MERGE_ROW_TILE, tn=MERGE_COL_CHUNK):
    m, d = x.shape
    hbm = pl.BlockSpec(memory_space=pl.ANY)

    def ytile(a):
        return pl.BlockSpec((tm, a.shape[1]), lambda i: (i, 0))

    def gtile(br):
        return pl.BlockSpec((tm, d), lambda i: (i, gates_col // d + br))

    return pl.pallas_call(
        functools.partial(_merge_body, tn=tn, layer=layer),
        grid=(m // tm,),
        in_specs=[
            pl.BlockSpec((tm, d), lambda i: (i, 0)),
            ytile(yc), ytile(ya), ytile(yr), hbm, hbm, hbm,
            gtile(0), gtile(1), gtile(2),
            pl.BlockSpec((None, N_BRANCH, d), lambda i: (layer, 0, 0)),
            hbm,
            _layer_vec(layer, d),
        ],
        out_specs=pl.BlockSpec((tm, d), lambda i: (i, 0)),
        out_shape=jax.ShapeDtypeStruct((m, d), F32),
        scratch_shapes=[
            pltpu.VMEM((tm, d), BF16),
            pltpu.VMEM(wc.shape[1:], BF16), pltpu.VMEM(wa.shape[1:], BF16), pltpu.VMEM(wr.shape[1:], BF16),
            pltpu.VMEM(wo.shape[1:], BF16),
            pltpu.VMEM((2, MERGE_STAGE_ROWS, d), F32),
            pltpu.SemaphoreType.DMA((2,)),
        ],
        compiler_params=_params("arbitrary"),
        name="merge",
    )(x, yc, ya, yr, wc, wa, wr, proj, proj, proj, gate_b, wo, post_g)


def _rope_tables(seq, n_rot, theta, group):
    half = n_rot // 2
    inv = 1.0 / (theta ** (jnp.arange(half, dtype=F32) / half))
    ang = jnp.arange(seq, dtype=jnp.int32).astype(F32)[:, None] * inv[None, :]
    cos, sin = jnp.cos(ang), jnp.sin(ang)
    rest = group - n_rot
    cos_g = jnp.concatenate([cos, cos, jnp.ones((seq, rest), F32)], axis=-1)
    sin_g = jnp.concatenate([-sin, sin, jnp.zeros((seq, rest), F32)], axis=-1)
    reps = LANES // group
    return jnp.tile(cos_g, (1, reps)), jnp.tile(sin_g, (1, reps))


def _retention_constants(heads):
    c = RET_CHUNK
    log_g = jnp.log1p(-(2.0 ** (-5.0 - jnp.arange(heads, dtype=F32))))
    idx = jnp.arange(c, dtype=F32)
    diff = idx[:, None] - idx[None, :]
    decay_mask = jnp.exp(jnp.where(diff >= 0, log_g[:, None, None] * diff, -jnp.inf))
    q_decay = jnp.broadcast_to(jnp.exp(log_g[:, None] * (idx + 1.0))[:, :, None], (heads, c, LANES))
    k_decay = jnp.exp(log_g[:, None] * (c - 1.0 - idx))[:, None, :]
    chunk_decay = jnp.broadcast_to(jnp.exp(log_g * c)[:, None, None], (heads, 1, LANES))
    return decay_mask, q_decay, k_decay, chunk_decay


def kernel(x, ffn1_pre_g, ffn1_w_gate, ffn1_w_up, ffn1_w_down, ffn1_post_g, mix_pre_g, w_in, conv_dw_w, conv_dw_b, conv_ln_g, conv_ln_b, ret_norm_g, w_br_conv, w_br_att, w_br_ret, gate_b, w_out, mix_post_g, ffn2_pre_g, ffn2_w_gate, ffn2_w_up, ffn2_w_down, ffn2_post_g):
    batch, seq, d = x.shape
    depth = w_in.shape[0]
    conv_ch = conv_dw_w.shape[2]
    att_w = w_br_att.shape[1]
    ret_v_w = w_br_ret.shape[1]
    att_heads = att_w // ATT_HEAD_DIM
    ret_dv = ret_v_w // RET_HEADS
    ret_dk = ret_dv // 2
    ret_qk_w = RET_HEADS * ret_dk
    sizes = [conv_ch, conv_ch, att_w, att_w, att_w, ret_qk_w, ret_qk_w, ret_v_w, ret_v_w, N_BRANCH * d]
    assert sum(sizes) == w_in.shape[2]
    cols = [0]
    for s in sizes[:-1]:
        cols.append(cols[-1] + s)
    (_, _, qa_col, ka_col, va_col, qr_col, kr_col, vr_col, gr_col, gates_col) = cols

    att_cos, att_sin = _rope_tables(seq, ATT_ROPE_DIMS, ATT_ROPE_THETA, ATT_HEAD_DIM)
    ret_cos, ret_sin = _rope_tables(seq, ret_dk, RET_ROT_THETA, ret_dk)
    dm, qd, kd, cd = _retention_constants(RET_HEADS)

    rows = lambda v: v.reshape(depth, 1, -1)
    ffn1 = (rows(ffn1_pre_g), ffn1_w_gate, ffn1_w_up, ffn1_w_down, rows(ffn1_post_g))
    ffn2 = (rows(ffn2_pre_g), ffn2_w_gate, ffn2_w_up, ffn2_w_down, rows(ffn2_post_g))
    conv_w = conv_dw_w.reshape(depth, CONV_WIDTH, conv_ch // LANES, LANES).transpose(0, 2, 1, 3)
    conv_p = (conv_w, rows(conv_dw_b), rows(conv_ln_g), rows(conv_ln_b))
    merge_w = (w_br_conv, w_br_att, w_br_ret)
    gate_b3 = gate_b.reshape(depth, N_BRANCH, d)
    wo = w_out
    mix_pre, mix_post, ret_g = rows(mix_pre_g), rows(mix_post_g), rows(ret_norm_g)

    xf = x.reshape(batch * seq, d)
    for l in range(depth):
        xf = _ffn(xf, *ffn1, layer=l)
        proj = _inproj(xf, mix_pre, w_in, layer=l)
        y_conv = _conv(proj, *conv_p, layer=l, batch=batch, seq=seq, ch=conv_ch)
        y_att = _moba(proj, att_cos, att_sin, batch=batch, seq=seq, heads=att_heads,
                      q_col=qa_col, k_col=ka_col, v_col=va_col)
        y_ret = _retention(proj, ret_cos, ret_sin, dm, qd, kd, cd, ret_g, layer=l,
                           batch=batch, seq=seq, heads=RET_HEADS, dk=ret_dk, dv=ret_dv,
                           q_col=qr_col, k_col=kr_col, v_col=vr_col, g_col=gr_col)
        xf = _merge(xf, y_conv, y_att, y_ret, *merge_w, proj, gate_b3, wo, mix_post,
                    layer=l, gates_col=gates_col)
        xf = _ffn(xf, *ffn2, layer=l)
    return xf.reshape(batch, seq, d)
```

```python
import functools
import math

import jax
import jax.numpy as jnp
from jax import lax
from jax.experimental import pallas as pl
from jax.experimental.pallas import tpu as pltpu

F32 = jnp.float32
BF16 = jnp.bfloat16

EPS = 1e-6
LANES = 128
V7X_VMEM_LIMIT_BYTES = 58 * 1024 * 1024
MASK_VALUE = -1e30
LOG2_E = math.log2(math.e)
BF16_SUBLANES = 16
NORM_ROWS = 16

CONV_WIDTH = 31
CONV_HALO = 32
CONV_NORM_ROWS = 32
ATT_HEAD_DIM = 128
ATT_ROPE_DIMS = ATT_HEAD_DIM // 4
ATT_ROPE_THETA = 500000.0
MOBA_BLOCK = 256
MOBA_TOPK = 3
MOBA_SOFTMAX_ROWS = 128
RET_HEADS = 8
RET_CHUNK = 128
RET_ROT_THETA = 10000.0
N_BRANCH = 3

FFN_ROW_TILE = 1024
FFN_HIDDEN_TILE = 256
INPROJ_ROW_TILE = 2048
INPROJ_COL_TILE = 512
CONV_ROW_TILE = 512
CONV_ACC_ROWS = 128
MOBA_HEADS_PER_STEP = 2
RET_CHUNKS_PER_TILE = 8
MERGE_ROW_TILE = 256
MERGE_COL_CHUNK = 512
MERGE_STAGE_ROWS = 256


def _params(*semantics):
    return pltpu.CompilerParams(dimension_semantics=semantics,
                                vmem_limit_bytes=V7X_VMEM_LIMIT_BYTES)


def _layer_vec(layer, width):
    return pl.BlockSpec((None, 1, width), lambda *_: (layer, 0, 0))


def _rms(x, g):
    return x * lax.rsqrt(jnp.mean(x * x, axis=-1, keepdims=True) + EPS) * g


def _row_chunks(n_rows):
    return [slice(r, r + NORM_ROWS) for r in range(0, n_rows, NORM_ROWS)]


def _silu(x):
    return x * jax.nn.sigmoid(x)


def _dot(a, b):
    return jnp.dot(a, b, preferred_element_type=F32)


def _dot_nt(a, b, precision=None):
    return lax.dot_general(a, b, (((1,), (1,)), ((), ())), precision=precision,
                           preferred_element_type=F32)


def _rope(x, cos, sin_signed, half, group):
    n = x.shape[-1]
    lane = lax.broadcasted_iota(jnp.int32, x.shape, x.ndim - 1)
    first = (lane % group) < half
    partner = jnp.where(first, pltpu.roll(x, n - half, x.ndim - 1), pltpu.roll(x, half, x.ndim - 1))
    return x * cos + partner * sin_signed


def _ffn_body(x_ref, pre_ref, wg_ref, wu_ref, wd_ref, post_ref, o_ref, h_ref):
    j = pl.program_id(1)

    last = pl.num_programs(1) - 1

    def hidden_tile(first=False, final=False):
        if first:
            for rows in _row_chunks(x_ref.shape[0]):
                h_ref[rows, :] = _rms(x_ref[rows, :], pre_ref[...]).astype(BF16)
        h = h_ref[...]
        g = _dot(h, wg_ref[...].astype(BF16))
        u = _dot(h, wu_ref[...].astype(BF16))
        a = (_silu(g) * u).astype(BF16)
        down = _dot(a, wd_ref[...].astype(BF16))
        if first:
            o_ref[...] = down
        else:
            o_ref[...] += down
        if final:
            half_post = 0.5 * post_ref[...]
            for rows in _row_chunks(x_ref.shape[0]):
                o_ref[rows, :] = x_ref[rows, :] + _rms(o_ref[rows, :], half_post)

    pl.when(j == 0)(lambda: hidden_tile(first=True))
    pl.when((j > 0) & (j < last))(hidden_tile)
    pl.when(j == last)(lambda: hidden_tile(final=True))


def _ffn(x, pre_g, wg, wu, wd, post_g, *, layer, tm=FFN_ROW_TILE, tf=FFN_HIDDEN_TILE):
    m, d = x.shape
    f = wg.shape[2]
    tile_specs = [
        pl.BlockSpec((tm, d), lambda i, j: (i, 0)),
        _layer_vec(layer, d),
        pl.BlockSpec((None, d, tf), lambda i, j: (layer, 0, j)),
        pl.BlockSpec((None, d, tf), lambda i, j: (layer, 0, j)),
        pl.BlockSpec((None, tf, d), lambda i, j: (layer, j, 0)),
        _layer_vec(layer, d),
    ]
    out_tile = pl.BlockSpec((tm, d), lambda i, j: (i, 0))

    def call_body(x_hbm, pre_hbm, wg_hbm, wu_hbm, wd_hbm, post_hbm, o_hbm, h_ref):
        tile_body = functools.partial(_ffn_body, h_ref=h_ref)
        pltpu.emit_pipeline(tile_body, grid=(m // tm, f // tf), in_specs=tile_specs, out_specs=[out_tile])(
            x_hbm, pre_hbm, wg_hbm, wu_hbm, wd_hbm, post_hbm, o_hbm)

    hbm = pl.BlockSpec(memory_space=pl.ANY)
    return pl.pallas_call(
        call_body,
        in_specs=[hbm] * 6,
        out_specs=hbm,
        out_shape=jax.ShapeDtypeStruct((m, d), F32),
        scratch_shapes=[pltpu.VMEM((tm, d), BF16)],
        compiler_params=pltpu.CompilerParams(vmem_limit_bytes=V7X_VMEM_LIMIT_BYTES),
        name="ffn",
    )(x, pre_g, wg, wu, wd, post_g)


def _inproj_body(x_hbm, g_ref, w_ref, o_ref, h_ref, x_ref, x_sem):
    i = pl.program_id(0)
    j = pl.program_id(1)
    tm = x_ref.shape[0]

    def x_copy(tile):
        return pltpu.make_async_copy(x_hbm.at[pl.ds(tile * tm, tm), :], x_ref, x_sem)

    def column_tile(first):
        if first:
            x_copy(i).wait()
            for rows in _row_chunks(tm):
                h_ref[rows, :] = _rms(x_ref[rows, :], g_ref[...]).astype(BF16)
        o_ref[...] = _dot(h_ref[...], w_ref[...].astype(BF16))

    pl.when((i == 0) & (j == 0))(lambda: x_copy(0).start())
    pl.when(j == 0)(lambda: column_tile(True))
    pl.when(j > 0)(lambda: column_tile(False))
    pl.when((j == 1) & (i + 1 < pl.num_programs(0)))(lambda: x_copy(i + 1).start())


def _inproj(x, g, w, *, layer, tm=INPROJ_ROW_TILE, tn=INPROJ_COL_TILE):
    m, d = x.shape
    n = w.shape[2]
    assert n // tn >= 2
    tile_specs = [_layer_vec(layer, d), pl.BlockSpec((None, d, tn), lambda i, j: (layer, 0, j))]
    out_tile = pl.BlockSpec((tm, tn), lambda i, j: (i, j))

    def call_body(x_hbm, g_hbm, w_hbm, o_hbm, h_ref, x_ref, x_sem):
        def tile_body(g_ref, w_ref, o_ref):
            _inproj_body(x_hbm, g_ref, w_ref, o_ref, h_ref, x_ref, x_sem)

        pltpu.emit_pipeline(tile_body, grid=(m // tm, n // tn), in_specs=tile_specs, out_specs=[out_tile])(
            g_hbm, w_hbm, o_hbm)

    hbm = pl.BlockSpec(memory_space=pl.ANY)
    return pl.pallas_call(
        call_body,
        in_specs=[hbm] * 3,
        out_specs=hbm,
        out_shape=jax.ShapeDtypeStruct((m, n), F32),
        scratch_shapes=[pltpu.VMEM((tm, d), BF16), pltpu.VMEM((tm, d), F32), pltpu.SemaphoreType.DMA(())],
        compiler_params=pltpu.CompilerParams(vmem_limit_bytes=V7X_VMEM_LIMIT_BYTES),
        name="inproj",
    )(x, g, w)


def _conv_body(a_ref, g_ref, ap_ref, gp_ref, w_ref, b_ref, lng_ref, lnb_ref, o_ref,
               u_ref, y_ref, *, ts, rows_per_acc):
    nck = u_ref.shape[0]
    row_chunks = [slice(r, r + CONV_NORM_ROWS) for r in range(0, ts, CONV_NORM_ROWS)]

    def store_glu(u, dst_start):
        for c in range(nck):
            u_ref[c, dst_start:dst_start + u.shape[0], :] = u[:, c * LANES:(c + 1) * LANES]

    up = ap_ref[...] * jax.nn.sigmoid(gp_ref[...])
    store_glu(jnp.where(pl.program_id(1) > 0, up, 0.0), 0)
    for rows in row_chunks:
        store_glu(a_ref[rows, :] * jax.nn.sigmoid(g_ref[rows, :]), CONV_HALO + rows.start)

    first = CONV_HALO - (CONV_WIDTH - 1)

    def chunk(c, carry):
        for r in range(ts // rows_per_acc):
            r0 = r * rows_per_acc
            acc = jnp.zeros((rows_per_acc, LANES), F32)
            for w in range(CONV_WIDTH):
                acc = acc + u_ref[c, r0 + first + w:r0 + first + w + rows_per_acc, :] * w_ref[c, w:w + 1, :]
            y_ref[c, r0:r0 + rows_per_acc, :] = acc
        return carry

    lax.fori_loop(0, nck, chunk, 0)

    for rows in row_chunks:
        y = jnp.concatenate([y_ref[c, rows, :] for c in range(nck)], axis=-1) + b_ref[...]
        mu = jnp.mean(y, axis=-1, keepdims=True)
        yc = y - mu
        yn = yc * lax.rsqrt(jnp.mean(yc * yc, axis=-1, keepdims=True) + EPS) * lng_ref[...] + lnb_ref[...]
        o_ref[rows, :] = _silu(yn).astype(BF16)


def _conv(proj, w3, b, ln_g, ln_b, *, layer, batch, seq, ch, ts=CONV_ROW_TILE, rows_per_acc=CONV_ACC_ROWS):
    m = proj.shape[0]
    nck = ch // LANES
    tiles = seq // ts
    halo_per_tile = ts // CONV_HALO

    def cur(col):
        return pl.BlockSpec((ts, ch), lambda bi, i: (bi * tiles + i, col))

    def prev(col):
        return pl.BlockSpec(
            (CONV_HALO, ch),
            lambda bi, i: (jnp.maximum((bi * tiles + i) * halo_per_tile - 1, 0), col))

    vec = _layer_vec(layer, ch)
    return pl.pallas_call(
        functools.partial(_conv_body, ts=ts, rows_per_acc=rows_per_acc),
        grid=(batch, tiles),
        in_specs=[cur(0), cur(1), prev(0), prev(1),
                  pl.BlockSpec((None, nck, CONV_WIDTH, LANES), lambda bi, i: (layer, 0, 0, 0)),
                  vec, vec, vec],
        out_specs=pl.BlockSpec((ts, ch), lambda bi, i: (bi * tiles + i, 0)),
        out_shape=jax.ShapeDtypeStruct((m, ch), BF16),
        scratch_shapes=[pltpu.VMEM((nck, CONV_HALO + ts, LANES), F32),
                        pltpu.VMEM((nck, ts, LANES), F32)],
        compiler_params=_params("parallel", "parallel"),
        name="conv",
    )(proj, proj, proj, proj, w3, b, ln_g, ln_b)


def _moba_body(q_ref, k_ref, v_ref, cos_ref, sin_ref, o_ref, kb_ref, vt_ref, km_ref, s_ref, m_ref, p_ref,
               *, nb, n_sel, scale, heads_per_step):
    blk = MOBA_BLOCK
    hd = ATT_HEAD_DIM
    half = ATT_ROPE_DIMS // 2

    def head_lanes(g):
        return slice(g * hd, (g + 1) * hd)

    def prepare_block(g, j):
        rows = slice(j * blk, (j + 1) * blk)
        kj = _rope(k_ref[rows, head_lanes(g)], cos_ref[rows, :], sin_ref[rows, :], half, ATT_HEAD_DIM)
        kb_ref[g, rows, :] = kj.astype(BF16)
        km_ref[g, j:j + 1, :] = jnp.mean(kj, axis=0, keepdims=True)
        vt_ref[g, 0:hd, rows] = v_ref[rows, head_lanes(g)].T.astype(BF16)
        vt_ref[g, hd:, rows] = jnp.ones((vt_ref.shape[1] - hd, blk), BF16)

    kc = MOBA_SOFTMAX_ROWS
    kidx = lax.broadcasted_iota(jnp.int32, (kc, blk), 0)
    qidx = lax.broadcasted_iota(jnp.int32, (kc, blk), 1)

    offs = [blk * i * (i + 1) // 2 for i in range(nb + 1)]
    bias_of = {}

    def scores(g, i):
        rows = slice(i * blk, (i + 1) * blk)
        nk = (i + 1) * blk
        q = _rope(q_ref[rows, head_lanes(g)], cos_ref[rows, :], sin_ref[rows, :], half, ATT_HEAD_DIM)
        s_ref[g, offs[i]:offs[i + 1], :] = _dot_nt(kb_ref[g, 0:nk, :], (q * (scale * LOG2_E)).astype(BF16))
        if i > n_sel:
            gate = _dot_nt(km_ref[g], q, precision=lax.Precision.HIGHEST)
            brow = lax.broadcasted_iota(jnp.int32, (nb, blk), 0)
            rank = jnp.zeros((nb, blk), jnp.int32)
            for jp in range(i):
                gj = gate[jp:jp + 1, :]
                rank = rank + jnp.where((gj > gate) | ((gj == gate) & (jp < brow)), 1, 0)
            bias_of[g, i] = jnp.where(rank < n_sel, 0.0, MASK_VALUE)

    def mask_and_max(g, i):
        m = None
        for c in range((i + 1) * blk // kc):
            keys = slice(offs[i] + c * kc, offs[i] + (c + 1) * kc)
            j = (c * kc) // blk
            sc = s_ref[g, keys, :]
            if j == i:
                sc = jnp.where(kidx + (c * kc - i * blk) <= qidx, sc, MASK_VALUE)
                s_ref[g, keys, :] = sc
            elif (g, i) in bias_of:
                sc = sc + bias_of[g, i][j:j + 1, :]
                s_ref[g, keys, :] = sc
            cm = jnp.max(sc, axis=0, keepdims=True)
            m = cm if m is None else jnp.maximum(m, cm)
        m_ref[g, i] = m

    def weights(g, i):
        m = m_ref[g, i]
        for c in range((i + 1) * blk // kc):
            keys = slice(offs[i] + c * kc, offs[i] + (c + 1) * kc)
            p_ref[g, keys, :] = jnp.exp2(s_ref[g, keys, :] - m).astype(BF16)

    def values(g, i):
        rows = slice(i * blk, (i + 1) * blk)
        acc = _dot(vt_ref[g, :, 0:(i + 1) * blk], p_ref[g, offs[i]:offs[i + 1], :])
        o_ref[rows, head_lanes(g)] = (acc[0:hd, :] / acc[hd:hd + 1, :]).T.astype(BF16)

    for phase in (prepare_block, scores, mask_and_max, weights, values):
        for g in range(heads_per_step):
            for i in range(nb):
                phase(g, i)


def _moba(proj, cos_t, sin_t, *, batch, seq, heads, q_col, k_col, v_col,
          heads_per_step=MOBA_HEADS_PER_STEP):
    m = proj.shape[0]
    hd = ATT_HEAD_DIM
    hw = heads_per_step * hd
    nb = seq // MOBA_BLOCK
    n_sel = min(MOBA_TOPK, nb - 1)
    score_rows = MOBA_BLOCK * nb * (nb + 1) // 2
    table = pl.BlockSpec((seq, hd), lambda b, h: (0, 0))

    def head_group(col):
        return pl.BlockSpec((seq, hw), lambda b, h: (b, col // hw + h))

    g = heads_per_step
    return pl.pallas_call(
        functools.partial(_moba_body, nb=nb, n_sel=n_sel, scale=hd ** -0.5, heads_per_step=g),
        grid=(batch, heads // g),
        in_specs=[head_group(q_col), head_group(k_col), head_group(v_col), table, table],
        out_specs=pl.BlockSpec((seq, hw), lambda b, h: (b, h)),
        out_shape=jax.ShapeDtypeStruct((m, heads * hd), BF16),
        scratch_shapes=[
            pltpu.VMEM((g, seq, hd), BF16),
            pltpu.VMEM((g, hd + BF16_SUBLANES, seq), BF16),
            pltpu.VMEM((g, nb, hd), F32),
            pltpu.VMEM((g, score_rows, MOBA_BLOCK), F32),
            pltpu.VMEM((g, nb, 1, MOBA_BLOCK), F32),
            pltpu.VMEM((g, score_rows, MOBA_BLOCK), BF16),
        ],
        compiler_params=_params("parallel", "parallel"),
        name="moba",
    )(proj, proj, proj, cos_t, sin_t)


def _ret_body(q_ref, k_ref, v_ref, gr_ref, cos_ref, sin_ref, dm_ref, qd_ref, kd_ref, cd_ref,
              ng_ref, o_ref, st_ref, *, heads, dk, dv, chunks):
    c = RET_CHUNK
    per_group = LANES // dk

    @pl.when(pl.program_id(1) == 0)
    def _():
        st_ref[...] = jnp.zeros_like(st_ref)

    lane_head = lax.broadcasted_iota(jnp.int32, (c, LANES), 1) // dk
    rows_of = lambda cc: slice(cc * c, (cc + 1) * c)
    groups = range(heads // per_group)
    units = [(hg, cc, hh) for hg in groups for cc in range(chunks) for hh in range(per_group)]
    head = lambda u: u[0] * per_group + u[2]
    qr, kb, krt = {}, {}, {}
    for hg in groups:
        lanes = slice(hg * LANES, (hg + 1) * LANES)
        for cc in range(chunks):
            cos, sin = cos_ref[rows_of(cc), :], sin_ref[rows_of(cc), :]
            qr[hg, cc] = _rope(q_ref[rows_of(cc), lanes], cos, sin, dk // 2, dk)
            kr = _rope(k_ref[rows_of(cc), lanes], cos, sin, dk // 2, dk) * (dk ** -0.5)
            kb[hg, cc] = kr.astype(BF16)
            krt[hg, cc] = kr.T
    qh = {u: jnp.where(lane_head == u[2], qr[u[0], u[1]], 0.0) for u in units}
    vb = {u: v_ref[rows_of(u[1]), head(u) * dv:(head(u) + 1) * dv].astype(BF16) for u in units}
    inner = {u: _dot_nt(qh[u].astype(BF16), kb[u[0], u[1]]) * dm_ref[head(u)] for u in units}
    ro = {u: _dot(inner[u].astype(BF16), vb[u]) for u in units}
    kv = {u: _dot((krt[u[0], u[1]][u[2] * dk:(u[2] + 1) * dk, :] * kd_ref[head(u)]).astype(BF16), vb[u])
          for u in units}
    for hg in groups:
        st = st_ref[hg]
        for cc in range(chunks):
            stb = st.astype(BF16)
            new_rows = []
            for hh in range(per_group):
                u = (hg, cc, hh)
                h = head(u)
                vl = slice(h * dv, (h + 1) * dv)
                r = ro[u] + _dot((qh[u] * qd_ref[h]).astype(BF16), stb)
                r = r * lax.rsqrt(jnp.mean(r * r, axis=-1, keepdims=True) + EPS) * ng_ref[:, vl]
                o_ref[rows_of(cc), vl] = (_silu(gr_ref[rows_of(cc), vl]) * r).astype(BF16)
                new_rows.append(st[hh * dk:(hh + 1) * dk, :] * cd_ref[h] + kv[u])
            st = jnp.concatenate(new_rows, axis=0)
        st_ref[hg] = st


def _retention(proj, cos_t, sin_t, dm, qd, kd, cd, norm_g, *, layer, batch, seq, heads, dk, dv,
               q_col, k_col, v_col, g_col, chunks=RET_CHUNKS_PER_TILE):
    m = proj.shape[0]
    ts = chunks * RET_CHUNK
    tiles = seq // ts
    qk_w = heads * dk
    v_w = heads * dv

    def tok(width, col):
        return pl.BlockSpec((ts, width), lambda b, t: (b * tiles + t, col // width))

    def const(shape):
        return pl.BlockSpec(shape, lambda b, t: (0,) * len(shape))

    return pl.pallas_call(
        functools.partial(_ret_body, heads=heads, dk=dk, dv=dv, chunks=chunks),
        grid=(batch, tiles),
        in_specs=[
            tok(qk_w, q_col), tok(qk_w, k_col), tok(v_w, v_col), tok(v_w, g_col),
            pl.BlockSpec((ts, LANES), lambda b, t: (t, 0)),
            pl.BlockSpec((ts, LANES), lambda b, t: (t, 0)),
            const(dm.shape), const(qd.shape), const(kd.shape), const(cd.shape),
            _layer_vec(layer, v_w),
        ],
        out_specs=pl.BlockSpec((ts, v_w), lambda b, t: (b * tiles + t, 0)),
        out_shape=jax.ShapeDtypeStruct((m, v_w), BF16),
        scratch_shapes=[pltpu.VMEM((heads * dk // LANES, LANES, dv), F32)],
        compiler_params=_params("parallel", "arbitrary"),
        name="retention",
    )(proj, proj, proj, proj, cos_t, sin_t, dm, qd, kd, cd, norm_g)


def _merge_body(x_ref, yc_ref, ya_ref, yr_ref, wc_hbm, wa_hbm, wr_hbm, g0_ref, g1_ref, g2_ref,
                gb_ref, wo_hbm, post_ref, o_ref, mg_ref, wc_ref, wa_ref, wr_ref, wo_ref, stage_ref, sems,
                *, tn, layer):
    d = o_ref.shape[1]

    @pl.when(pl.program_id(0) == 0)
    def _():
        rows = stage_ref.shape[1]
        chunks = [(src, dst, r) for src, dst in ((wc_hbm, wc_ref), (wa_hbm, wa_ref), (wr_hbm, wr_ref), (wo_hbm, wo_ref))
                  for r in range(0, dst.shape[0], rows)]

        def chunk_copy(c):
            src, _, r = chunks[c]
            return pltpu.make_async_copy(src.at[layer, pl.ds(r, rows), :], stage_ref.at[c % 2], sems.at[c % 2])

        chunk_copy(0).start()
        for c, (_, dst, r) in enumerate(chunks):
            if c + 1 < len(chunks):
                chunk_copy(c + 1).start()
            chunk_copy(c).wait()
            dst[r:r + rows, :] = stage_ref[c % 2].astype(BF16)

    for t in range(d // tn):
        cols = slice(t * tn, (t + 1) * tn)
        merged = (jax.nn.sigmoid(g0_ref[:, cols] + gb_ref[0:1, cols]) * _dot(yc_ref[...], wc_ref[:, cols])
                  + jax.nn.sigmoid(g1_ref[:, cols] + gb_ref[1:2, cols]) * _dot(ya_ref[...], wa_ref[:, cols])
                  + jax.nn.sigmoid(g2_ref[:, cols] + gb_ref[2:3, cols]) * _dot(yr_ref[...], wr_ref[:, cols]))
        mg_ref[:, cols] = merged.astype(BF16)
    o_ref[...] = x_ref[...] + _rms(_dot(mg_ref[...], wo_ref[...]), post_ref[...])


def _merge(x, yc, ya, yr, wc, wa, wr, proj, gate_b, wo, post_g, *, layer, gates_col,
           tm=MERGE_ROW_TILE, tn=MERGE_COL_CHUNK):
    m, d = x.shape
    hbm = pl.BlockSpec(memory_space=pl.ANY)

    def ytile(a):
        return pl.BlockSpec((tm, a.shape[1]), lambda i: (i, 0))

    def gtile(br):
        return pl.BlockSpec((tm, d), lambda i: (i, gates_col // d + br))

    return pl.pallas_call(
        functools.partial(_merge_body, tn=tn, layer=layer),
        grid=(m // tm,),
        in_specs=[
            pl.BlockSpec((tm, d), lambda i: (i, 0)),
            ytile(yc), ytile(ya), ytile(yr), hbm, hbm, hbm,
            gtile(0), gtile(1), gtile(2),
            pl.BlockSpec((None, N_BRANCH, d), lambda i: (layer, 0, 0)),
            hbm,
            _layer_vec(layer, d),
        ],
        out_specs=pl.BlockSpec((tm, d), lambda i: (i, 0)),
        out_shape=jax.ShapeDtypeStruct((m, d), F32),
        scratch_shapes=[
            pltpu.VMEM((tm, d), BF16),
            pltpu.VMEM(wc.shape[1:], BF16), pltpu.VMEM(wa.shape[1:], BF16), pltpu.VMEM(wr.shape[1:], BF16),
            pltpu.VMEM(wo.shape[1:], BF16),
            pltpu.VMEM((2, MERGE_STAGE_ROWS, d), F32),
            pltpu.SemaphoreType.DMA((2,)),
        ],
        compiler_params=_params("arbitrary"),
        name="merge",
    )(x, yc, ya, yr, wc, wa, wr, proj, proj, proj, gate_b, wo, post_g)


def _rope_tables(seq, n_rot, theta, group):
    half = n_rot // 2
    inv = 1.0 / (theta ** (jnp.arange(half, dtype=F32) / half))
    ang = jnp.arange(seq, dtype=jnp.int32).astype(F32)[:, None] * inv[None, :]
    cos, sin = jnp.cos(ang), jnp.sin(ang)
    rest = group - n_rot
    cos_g = jnp.concatenate([cos, cos, jnp.ones((seq, rest), F32)], axis=-1)
    sin_g = jnp.concatenate([-sin, sin, jnp.zeros((seq, rest), F32)], axis=-1)
    reps = LANES // group
    return jnp.tile(cos_g, (1, reps)), jnp.tile(sin_g, (1, reps))


def _retention_constants(heads):
    c = RET_CHUNK
    log_g = jnp.log1p(-(2.0 ** (-5.0 - jnp.arange(heads, dtype=F32))))
    idx = jnp.arange(c, dtype=F32)
    diff = idx[:, None] - idx[None, :]
    decay_mask = jnp.exp(jnp.where(diff >= 0, log_g[:, None, None] * diff, -jnp.inf))
    q_decay = jnp.broadcast_to(jnp.exp(log_g[:, None] * (idx + 1.0))[:, :, None], (heads, c, LANES))
    k_decay = jnp.exp(log_g[:, None] * (c - 1.0 - idx))[:, None, :]
    chunk_decay = jnp.broadcast_to(jnp.exp(log_g * c)[:, None, None], (heads, 1, LANES))
    return decay_mask, q_decay, k_decay, chunk_decay


def kernel(x, ffn1_pre_g, ffn1_w_gate, ffn1_w_up, ffn1_w_down, ffn1_post_g, mix_pre_g, w_in, conv_dw_w, conv_dw_b, conv_ln_g, conv_ln_b, ret_norm_g, w_br_conv, w_br_att, w_br_ret, gate_b, w_out, mix_post_g, ffn2_pre_g, ffn2_w_gate, ffn2_w_up, ffn2_w_down, ffn2_post_g):
    batch, seq, d = x.shape
    depth = w_in.shape[0]
    conv_ch = conv_dw_w.shape[2]
    att_w = w_br_att.shape[1]
    ret_v_w = w_br_ret.shape[1]
    att_heads = att_w // ATT_HEAD_DIM
    ret_dv = ret_v_w // RET_HEADS
    ret_dk = ret_dv // 2
    ret_qk_w = RET_HEADS * ret_dk
    sizes = [conv_ch, conv_ch, att_w, att_w, att_w, ret_qk_w, ret_qk_w, ret_v_w, ret_v_w, N_BRANCH * d]
    assert sum(sizes) == w_in.shape[2]
    cols = [0]
    for s in sizes[:-1]:
        cols.append(cols[-1] + s)
    (_, _, qa_col, ka_col, va_col, qr_col, kr_col, vr_col, gr_col, gates_col) = cols

    att_cos, att_sin = _rope_tables(seq, ATT_ROPE_DIMS, ATT_ROPE_THETA, ATT_HEAD_DIM)
    ret_cos, ret_sin = _rope_tables(seq, ret_dk, RET_ROT_THETA, ret_dk)
    dm, qd, kd, cd = _retention_constants(RET_HEADS)

    rows = lambda v: v.reshape(depth, 1, -1)
    ffn1 = (rows(ffn1_pre_g), ffn1_w_gate, ffn1_w_up, ffn1_w_down, rows(ffn1_post_g))
    ffn2 = (rows(ffn2_pre_g), ffn2_w_gate, ffn2_w_up, ffn2_w_down, rows(ffn2_post_g))
    conv_w = conv_dw_w.reshape(depth, CONV_WIDTH, conv_ch // LANES, LANES).transpose(0, 2, 1, 3)
    conv_p = (conv_w, rows(conv_dw_b), rows(conv_ln_g), rows(conv_ln_b))
    merge_w = (w_br_conv, w_br_att, w_br_ret)
    gate_b3 = gate_b.reshape(depth, N_BRANCH, d)
    wo = w_out
    mix_pre, mix_post, ret_g = rows(mix_pre_g), rows(mix_post_g), rows(ret_norm_g)

    xf = x.reshape(batch * seq, d)
    for l in range(depth):
        xf = _ffn(xf, *ffn1, layer=l)
        proj = _inproj(xf, mix_pre, w_in, layer=l)
        y_conv = _conv(proj, *conv_p, layer=l, batch=batch, seq=seq, ch=conv_ch)
        y_att = _moba(proj, att_cos, att_sin, batch=batch, seq=seq, heads=att_heads,
                      q_col=qa_col, k_col=ka_col, v_col=va_col)
        y_ret = _retention(proj, ret_cos, ret_sin, dm, qd, kd, cd, ret_g, layer=l,
                           batch=batch, seq=seq, heads=RET_HEADS, dk=ret_dk, dv=ret_dv,
                           q_col=qr_col, k_col=kr_col, v_col=vr_col, g_col=gr_col)
        xf = _merge(xf, y_conv, y_att, y_ret, *merge_w, proj, gate_b3, wo, mix_post,
                    layer=l, gates_col=gates_col)
        xf = _ffn(xf, *ffn2, layer=l)
    return xf.reshape(batch, seq, d)
```
